```python
import math
import jax, jax.numpy as jnp
from jax import lax
import numpy as np

D_MODEL = 2048
BATCH = 2
SEQ = 8192
DEPTH = 2
DEC_BATCH = 4
DEC_SEQ = 2048
PAST_LEN = 128

GRID_W = 64
HEAD_DIM = 128
NA_HEADS = 8
NA_WIN_ROWS = 8
NA_WIN_COLS = 16
MLA_HEADS = 8
MLA_Q_RANK = 512
MLA_KV_RANK = 512
MLA_NOPE = 128
MLA_ROPE = 64
MLA_V = 128
DIL_HEADS = 16
DIL_PAIRS = ((128, 1), (512, 4), (2048, 16))
DIL_HALF = 1024
Q_BLOCK = 128

WA = NA_HEADS * HEAD_DIM
WB = MLA_HEADS * MLA_V
WC = DIL_HEADS * HEAD_DIM
IN0_SIZES = (WA, WA, WA, WA, MLA_Q_RANK, MLA_KV_RANK, MLA_ROPE, WB)
IN0_WIDTH = 4 * WA + MLA_Q_RANK + MLA_KV_RANK + MLA_ROPE + WB
IN1_WIDTH = 4 * WC

ROPE_THETA = 10000.0
ALPHA = (2 * DEPTH) ** 0.25
BETA = (8 * DEPTH) ** -0.25
LN_EPS = 1e-5
RMS_EPS = 1e-6
NEG = -1e30

kernel_name = 'hybrid_na_mla_dilated_encoder'


def split_cols(t, sizes):
    idx, acc = [], 0
    for n in sizes[:-1]:
        acc += n
        idx.append(acc)
    return jnp.split(t, idx, axis=-1)


def layer_norm(x, g, b):
    xf = x.astype(jnp.float32)
    mu = xf.mean(-1, keepdims=True)
    var = jnp.square(xf - mu).mean(-1, keepdims=True)
    return ((xf - mu) * lax.rsqrt(var + LN_EPS) * g + b).astype(x.dtype)


def rms_norm(x, g):
    xf = x.astype(jnp.float32)
    return (xf * lax.rsqrt(jnp.square(xf).mean(-1, keepdims=True) + RMS_EPS) * g).astype(x.dtype)


def rope(x):
    s, d = x.shape[1], x.shape[-1]
    half = d // 2
    inv = ROPE_THETA ** (-jnp.arange(half, dtype=jnp.float32) / half)
    ang = jnp.arange(s, dtype=jnp.float32)[:, None] * inv[None, :]
    cos = jnp.cos(ang)[None, :, None, :]
    sin = jnp.sin(ang)[None, :, None, :]
    xf = x.astype(jnp.float32)
    x1, x2 = xf[..., :half], xf[..., half:]
    return jnp.concatenate([x1 * cos - x2 * sin, x2 * cos + x1 * sin], -1).astype(x.dtype)


def neighbourhood_attention(q, k, v, rpb):
    b, s, h, dh = q.shape
    rows = s // GRID_W
    kh = min(NA_WIN_ROWS, rows)
    kw = NA_WIN_COLS
    qg = q.reshape(b, rows, GRID_W, h, dh)
    kg = k.reshape(b, rows, GRID_W, h, dh)
    vg = v.reshape(b, rows, GRID_W, h, dh)
    col = jnp.arange(GRID_W)
    col_start = jnp.clip(col - kw // 2, 0, GRID_W - kw)
    col_idx = col_start[:, None] + jnp.arange(kw)[None, :]
    dc = col_idx - col[:, None] + (NA_WIN_COLS - 1)

    def one_row(r):
        r0 = jnp.clip(r - kh // 2, 0, rows - kh)
        qr = lax.dynamic_index_in_dim(qg, r, axis=1, keepdims=False)
        kr = lax.dynamic_slice_in_dim(kg, r0, kh, axis=1)
        vr = lax.dynamic_slice_in_dim(vg, r0, kh, axis=1)
        kn = kr[:, :, col_idx]
        vn = vr[:, :, col_idx]
        sc = jnp.einsum('bqhd,biqjhd->bhqij', qr, kn).astype(jnp.float32)
        dr = r0 + jnp.arange(kh) - r + (NA_WIN_ROWS - 1)
        bias = rpb[:, dr[:, None, None], dc[None, :, :]]
        sc = sc + bias.transpose(0, 2, 1, 3)[None].astype(jnp.float32)
        p = jax.nn.softmax(sc.reshape(b, h, GRID_W, kh * kw), axis=-1)
        p = p.reshape(b, h, GRID_W, kh, kw).astype(v.dtype)
        return jnp.einsum('bhqij,biqjhd->bqhd', p, vn)

    out = lax.map(one_row, jnp.arange(rows))
    return out.transpose(1, 0, 2, 3, 4).reshape(b, s, h * dh)


def dense_attention_blocks(q, k, v):
    b, s, h, dq = q.shape
    dv = v.shape[-1]
    nb = s // Q_BLOCK
    qb = q.reshape(b, nb, Q_BLOCK, h, dq).transpose(1, 0, 2, 3, 4)

    def attend(qi):
        sc = jnp.einsum('bqhd,bkhd->bhqk', qi, k).astype(jnp.float32)
        p = jax.nn.softmax(sc, axis=-1).astype(v.dtype)
        return jnp.einsum('bhqk,bkhd->bqhd', p, v)

    out = lax.map(attend, qb)
    return out.transpose(1, 0, 2, 3, 4).reshape(b, s, h * dv)


def mla_attention(q_lat, kv_lat, k_rope, q_norm_g, w_q_up, kv_norm_g, w_kv_up):
    b, s, _ = q_lat.shape
    q = (rms_norm(q_lat, q_norm_g) @ w_q_up).reshape(b, s, MLA_HEADS, MLA_NOPE + MLA_ROPE)
    q = jnp.concatenate([q[..., :MLA_NOPE], rope(q[..., MLA_NOPE:])], -1)
    kv = (rms_norm(kv_lat, kv_norm_g) @ w_kv_up).reshape(b, s, MLA_HEADS, MLA_NOPE + MLA_V)
    k_nope, v = kv[..., :MLA_NOPE], kv[..., MLA_NOPE:]
    k_pe = rope(k_rope[:, :, None, :])
    k = jnp.concatenate([k_nope, jnp.broadcast_to(k_pe, (b, s, MLA_HEADS, MLA_ROPE))], -1)
    return dense_attention_blocks(q * (MLA_NOPE + MLA_ROPE) ** -0.5, k, v)


def dilated_attention(q, k, v):
    b, s, h, dh = q.shape
    nb = s // Q_BLOCK
    pad = ((0, 0), (DIL_HALF, DIL_HALF), (0, 0), (0, 0))
    kp, vp = jnp.pad(k, pad), jnp.pad(v, pad)
    valid = jnp.pad(jnp.ones((s,), dtype=bool), (DIL_HALF, DIL_HALF))
    band_len = Q_BLOCK + 2 * DIL_HALF
    qb = q.reshape(b, nb, Q_BLOCK, h, dh).transpose(1, 0, 2, 3, 4)

    def one_block(args):
        i, qi = args
        s0 = i * Q_BLOCK
        kband = lax.dynamic_slice_in_dim(kp, s0, band_len, axis=1)
        vband = lax.dynamic_slice_in_dim(vp, s0, band_len, axis=1)
        mband = lax.dynamic_slice_in_dim(valid, s0, band_len)
        nums, dens, maxs = [], [], []
        for window, dil in DIL_PAIRS:
            half = window // 2
            n_side = half // dil
            lo = DIL_HALF - half
            ln = Q_BLOCK + 2 * half
            nq, nk = Q_BLOCK // dil, ln // dil
            kr = kband[:, lo:lo + ln].reshape(b, nk, dil, h, dh)
            vr = vband[:, lo:lo + ln].reshape(b, nk, dil, h, dh)
            mr = mband[lo:lo + ln].reshape(nk, dil)
            qr = qi.reshape(b, nq, dil, h, dh)
            sc = jnp.einsum('bqrhd,bkrhd->bhrqk', qr, kr).astype(jnp.float32)
            off = jnp.arange(nk)[None, :] - jnp.arange(nq)[:, None] - n_side
            mask = (jnp.abs(off) <= n_side)[None] & mr.T[:, None, :]
            sc = jnp.where(mask[None, None], sc, NEG)
            m = sc.max(-1, keepdims=True)
            e = jnp.exp(sc - m)
            den = e.sum(-1)
            num = jnp.einsum('bhrqk,bkrhd->bqrhd', e.astype(vr.dtype), vr).astype(jnp.float32)
            nums.append(num.reshape(b, Q_BLOCK, h, dh))
            dens.append(den.transpose(0, 3, 2, 1).reshape(b, Q_BLOCK, h))
            maxs.append(m[..., 0].transpose(0, 3, 2, 1).reshape(b, Q_BLOCK, h))
        ms = jnp.stack(maxs)
        w = jnp.exp(ms - ms.max(0))
        num_tot = (w[..., None] * jnp.stack(nums)).sum(0)
        den_tot = (w * jnp.stack(dens)).sum(0)
        return (num_tot / den_tot[..., None]).astype(q.dtype)

    out = lax.map(one_block, (jnp.arange(nb), qb))
    return out.transpose(1, 0, 2, 3, 4).reshape(b, s, h * dh)


def layer_ab(x, w_in, rpb, q_norm_g, w_q_up, kv_norm_g, w_kv_up, w_out, ln_g, ln_b):
    b, s, _ = x.shape
    qa, ka, va, ga, q_lat, kv_lat, k_rope, gb = split_cols(x @ w_in, IN0_SIZES)
    heads = lambda t: t.reshape(b, s, NA_HEADS, HEAD_DIM)
    ya = neighbourhood_attention(heads(qa) * HEAD_DIM ** -0.5, heads(ka), heads(va), rpb) * jax.nn.silu(ga)
    yb = mla_attention(q_lat, kv_lat, k_rope, q_norm_g, w_q_up, kv_norm_g, w_kv_up) * jax.nn.silu(gb)
    y = jnp.concatenate([ya, yb], -1) @ w_out
    return layer_norm(ALPHA * x + y, ln_g, ln_b)


def layer_c(x, w_in, w_out, ln_g, ln_b):
    b, s, _ = x.shape
    qc, kc, vc, gc = split_cols(x @ w_in, (WC, WC, WC, WC))
    heads = lambda t: t.reshape(b, s, DIL_HEADS, HEAD_DIM)
    q = rope(heads(qc)) * HEAD_DIM ** -0.5
    k = rope(heads(kc))
    y = (dilated_attention(q, k, heads(vc)) * jax.nn.silu(gc)) @ w_out
    return layer_norm(ALPHA * x + y, ln_g, ln_b)


def trunk(x, ab_params, c_params):
    for layer in range(DEPTH):
        if layer % 2 == 0:
            x = layer_ab(x, *ab_params)
        else:
            x = layer_c(x, *c_params)
    return x


def setup_inputs(seed: int = 0) -> dict:
    key = jax.random.key(seed)
    ks = jax.random.split(key, 16)
    f32 = jnp.float32
    nrm = lambda k, shape, scale: jax.random.normal(k, shape, f32) * scale
    return {
        'x_prompt': nrm(ks[0], (BATCH, SEQ, D_MODEL), 1.0),
        'x_sample': nrm(ks[1], (DEC_BATCH, DEC_SEQ, D_MODEL), 1.0),
        'ab_w_in': nrm(ks[2], (D_MODEL, IN0_WIDTH), D_MODEL ** -0.5),
        'ab_rpb': nrm(ks[3], (NA_HEADS, 2 * NA_WIN_ROWS - 1, 2 * NA_WIN_COLS - 1), 0.1),
        'ab_q_norm_g': 1.0 + nrm(ks[4], (MLA_Q_RANK,), 0.01),
        'ab_w_q_up': nrm(ks[5], (MLA_Q_RANK, MLA_HEADS * (MLA_NOPE + MLA_ROPE)), MLA_Q_RANK ** -0.5),
        'ab_kv_norm_g': 1.0 + nrm(ks[6], (MLA_KV_RANK,), 0.01),
        'ab_w_kv_up': nrm(ks[7], (MLA_KV_RANK, MLA_HEADS * (MLA_NOPE + MLA_V)), MLA_KV_RANK ** -0.5),
        'ab_w_out': nrm(ks[8], (WA + WB, D_MODEL), BETA * (WA + WB) ** -0.5),
        'ab_ln_g': 1.0 + nrm(ks[9], (D_MODEL,), 0.01),
        'ab_ln_b': nrm(ks[10], (D_MODEL,), 0.01),
        'c_w_in': nrm(ks[11], (D_MODEL, IN1_WIDTH), D_MODEL ** -0.5),
        'c_w_out': nrm(ks[12], (WC, D_MODEL), BETA * WC ** -0.5),
        'c_ln_g': 1.0 + nrm(ks[13], (D_MODEL,), 0.01),
        'c_ln_b': nrm(ks[14], (D_MODEL,), 0.01),
    }


def reference(x_prompt, x_sample, ab_w_in, ab_rpb, ab_q_norm_g, ab_w_q_up, ab_kv_norm_g, ab_w_kv_up,
              ab_w_out, ab_ln_g, ab_ln_b, c_w_in, c_w_out, c_ln_g, c_ln_b):
    ab_params = (ab_w_in, ab_rpb, ab_q_norm_g, ab_w_q_up, ab_kv_norm_g, ab_w_kv_up, ab_w_out, ab_ln_g, ab_ln_b)
    c_params = (c_w_in, c_w_out, c_ln_g, c_ln_b)
    y_prompt = trunk(x_prompt, ab_params, c_params)
    y_sample = trunk(x_sample, ab_params, c_params)
    return (y_prompt, y_sample)
```

```python
import functools

import jax
import jax.numpy as jnp
from jax import lax
from jax.experimental import pallas as pl
from jax.experimental.pallas import tpu as pltpu

F32 = jnp.float32
BF16 = jnp.bfloat16

D_MODEL = 2048
DEPTH = 2
GRID_W = 64
HEAD_DIM = 128
NA_HEADS = 8
NA_WIN_ROWS = 8
NA_WIN_COLS = 16
MLA_HEADS = 8
MLA_Q_RANK = 512
MLA_KV_RANK = 512
MLA_NOPE = 128
MLA_ROPE = 64
MLA_V = 128
DIL_HEADS = 16
DIL_PAIRS = ((128, 1), (512, 4), (2048, 16))
DIL_HALF = 1024
WA = NA_HEADS * HEAD_DIM
WB = MLA_HEADS * MLA_V
WC = DIL_HEADS * HEAD_DIM
ROPE_THETA = 10000.0
ALPHA = (2 * DEPTH) ** 0.25
LN_EPS = 1e-5
RMS_EPS = 1e-6
NEG = -1e30

LANES = 128
MLA_QK_PAD = 2 * LANES
IN0_PAD_WIDTH = 4 * WA + MLA_Q_RANK + MLA_KV_RANK + WB + LANES
VMEM_LIMIT = 56 * 1024 * 1024

C0_Q, C0_K, C0_V, C0_G = 0, 8, 16, 24
C0_QLAT, C0_KVLAT = 8, 9
C0_GB = 40
C0_KROPE = 48

NA_QROWS = 8
NA_KROWS = 16
NA_TQ = NA_QROWS * GRID_W
NA_TK = NA_KROWS * GRID_W


def _params(n_axes):
    return pltpu.CompilerParams(dimension_semantics=("arbitrary",) * n_axes,
                                vmem_limit_bytes=VMEM_LIMIT)


def _proj_f32_kernel(x_ref, w_ref, o_ref, xb_ref):
    @pl.when(pl.program_id(1) == 0)
    def _():
        xb_ref[...] = x_ref[...].astype(BF16)

    o_ref[...] = jnp.dot(xb_ref[...], w_ref[...], preferred_element_type=F32).astype(o_ref.dtype)


def _proj_f32(x, w, tm, tn):
    m, k = x.shape
    n = w.shape[1]
    return pl.pallas_call(
        _proj_f32_kernel,
        grid=(m // tm, n // tn),
        in_specs=[pl.BlockSpec((tm, k), lambda i, j: (i, 0)),
                  pl.BlockSpec((k, tn), lambda i, j: (0, j))],
        out_specs=pl.BlockSpec((tm, tn), lambda i, j: (i, j)),
        out_shape=jax.ShapeDtypeStruct((m, n), BF16),
        scratch_shapes=[pltpu.VMEM((tm, k), BF16)],
        compiler_params=_params(2),
    )(x, w)


def _proj_rope_kernel(x_ref, w_ref, cos_ref, sin_ref, o_ref, *, n_rope_blocks):
    y = jnp.dot(x_ref[...], w_ref[...], preferred_element_type=F32)
    j = pl.program_id(1)

    @pl.when(j < n_rope_blocks)
    def _():
        c = cos_ref[...]
        s = sin_ref[...]
        for h in range(y.shape[1] // LANES):
            yh = y[:, h * LANES:(h + 1) * LANES]
            o_ref[:, h * LANES:(h + 1) * LANES] = (
                yh * c + pltpu.roll(yh, LANES // 2, 1) * s).astype(o_ref.dtype)

    @pl.when(j >= n_rope_blocks)
    def _():
        o_ref[...] = y.astype(o_ref.dtype)


def _proj_rope(x, w, cos_t, sin_t, seq, tm, tn, rope_cols):
    m, k = x.shape
    n = w.shape[1]
    pos_blocks = seq // tm
    return pl.pallas_call(
        functools.partial(_proj_rope_kernel, n_rope_blocks=rope_cols // tn),
        grid=(m // tm, n // tn),
        in_specs=[pl.BlockSpec((tm, k), lambda i, j: (i, 0)),
                  pl.BlockSpec((k, tn), lambda i, j: (0, j)),
                  pl.BlockSpec((tm, LANES), lambda i, j: (i % pos_blocks, 0)),
                  pl.BlockSpec((tm, LANES), lambda i, j: (i % pos_blocks, 0))],
        out_specs=pl.BlockSpec((tm, tn), lambda i, j: (i, j)),
        out_shape=jax.ShapeDtypeStruct((m, n), BF16),
        compiler_params=_params(2),
    )(x, w, cos_t, sin_t)


def _na_bias_kernel(rpb_ref, o_ref, t_ref):
    h = pl.program_id(0)
    n_dc = 2 * NA_WIN_COLS - 1
    c = lax.broadcasted_iota(jnp.int32, (GRID_W, GRID_W), 0)
    kc = lax.broadcasted_iota(jnp.int32, (GRID_W, GRID_W), 1)
    d = kc - c + (NA_WIN_COLS - 1)
    cs = jnp.clip(c - NA_WIN_COLS // 2, 0, GRID_W - NA_WIN_COLS)
    col_ok = (kc >= cs) & (kc < cs + NA_WIN_COLS)
    for dr in range(2 * NA_WIN_ROWS - 1):
        t = jnp.zeros((GRID_W, GRID_W), F32)
        for dd in range(n_dc):
            t = jnp.where(d == dd, rpb_ref[h, dr * n_dc + dd], t)
        t_ref[dr] = jnp.where(col_ok, t, NEG)
    neg_blk = jnp.full((GRID_W, GRID_W), NEG, F32)
    for typ in range(3):
        for a in range(NA_QROWS):
            if typ == 0:
                lo, dr0 = max(a - 4, 0), 7 - a
            elif typ == 1:
                lo, dr0 = a, 3 - a
            else:
                lo, dr0 = 8 + min(a - 4, 0), -1 - a
            for kb in range(NA_KROWS):
                ok = lo <= kb < lo + NA_WIN_ROWS
                blk = t_ref[kb + dr0] if ok else neg_blk
                o_ref[0, typ, a * GRID_W:(a + 1) * GRID_W, kb * GRID_W:(kb + 1) * GRID_W] = blk


def _na_bias(rpb):
    rpb2 = rpb.reshape(NA_HEADS, -1)
    return pl.pallas_call(
        _na_bias_kernel,
        grid=(NA_HEADS,),
        in_specs=[pl.BlockSpec(memory_space=pltpu.SMEM)],
        out_specs=pl.BlockSpec((1, 3, NA_TQ, NA_TK), lambda h: (h, 0, 0, 0)),
        out_shape=jax.ShapeDtypeStruct((NA_HEADS, 3, NA_TQ, NA_TK), F32),
        scratch_shapes=[pltpu.VMEM((2 * NA_WIN_ROWS - 1, GRID_W, GRID_W), F32)],
        compiler_params=_params(1),
    )(rpb2)


def _na_kernel(q_ref, k0_ref, k1_ref, k2_ref, k3_ref, v0_ref, v1_ref, v2_ref, v3_ref,
               g_ref, b_ref, o_ref):
    q = q_ref[...]
    dn = (((1,), (1,)), ((), ()))
    s = jnp.concatenate(
        [lax.dot_general(q, kr[...], dn, preferred_element_type=F32)
         for kr in (k0_ref, k1_ref, k2_ref, k3_ref)], axis=1)
    s = s + b_ref[0, 0]
    m = jnp.max(s, axis=1, keepdims=True)
    p = jnp.exp(s - m)
    l = jnp.sum(p, axis=1, keepdims=True)
    pb = p.astype(BF16)
    kq = NA_TK // 4
    o = jnp.zeros((NA_TQ, HEAD_DIM), F32)
    for i, vr in enumerate((v0_ref, v1_ref, v2_ref, v3_ref)):
        o = o + jnp.dot(pb[:, i * kq:(i + 1) * kq], vr[...], preferred_element_type=F32)
    g = g_ref[...].astype(F32)
    o_ref[...] = (o / l * (g * jax.nn.sigmoid(g))).astype(o_ref.dtype)


def _na_attention(proj, bias, batch, seq):
    rows = seq // GRID_W
    nj = rows // NA_QROWS
    kq = NA_TK // 4
    parts_per_seq = seq // kq

    def kbase(j):
        return jnp.clip(2 * j - 1, 0, rows // 4 - 4)

    def kv_spec(col0, part):
        return pl.BlockSpec((kq, LANES),
                            lambda h, j, b: (b * parts_per_seq + kbase(j) + part, col0 + h))

    def btype(j):
        return jnp.where(j == 0, 0, jnp.where(j == nj - 1, 2, 1))

    tok_spec = lambda col0: pl.BlockSpec((NA_TQ, LANES), lambda h, j, b: (b * nj + j, col0 + h))
    in_specs = ([tok_spec(C0_Q)] + [kv_spec(C0_K, i) for i in range(4)]
                + [kv_spec(C0_V, i) for i in range(4)] + [tok_spec(C0_G)]
                + [pl.BlockSpec((1, 1, NA_TQ, NA_TK), lambda h, j, b: (h, btype(j), 0, 0))])
    return pl.pallas_call(
        _na_kernel,
        grid=(NA_HEADS, nj, batch),
        in_specs=in_specs,
        out_specs=pl.BlockSpec((NA_TQ, LANES), lambda h, j, b: (b * nj + j, h)),
        out_shape=jax.ShapeDtypeStruct((batch * seq, WA), BF16),
        compiler_params=_params(3),
    )(*([proj] * 10), bias)


def _rms(lat_ref, g_ref):
    x = lat_ref[...].astype(F32)
    ms = jnp.mean(x * x, axis=1, keepdims=True)
    return (x * lax.rsqrt(ms + RMS_EPS) * g_ref[...]).astype(BF16)


def _rope64(t, cos_t, sin_t):
    lane = lax.broadcasted_iota(jnp.int32, t.shape, 1)
    half = MLA_ROPE // 2
    partner = jnp.where(lane < half, pltpu.roll(t, LANES - half, 1), pltpu.roll(t, half, 1))
    return t * cos_t + partner * sin_t


def _mla_q_up_kernel(lat_ref, g_ref, w_ref, cos_ref, sin_ref, o_ref):
    y = jnp.dot(_rms(lat_ref, g_ref), w_ref[...], preferred_element_type=F32)
    c = cos_ref[...]
    s = sin_ref[...]
    for h in range(MLA_HEADS):
        base = h * MLA_QK_PAD
        o_ref[:, base:base + LANES] = y[:, base:base + LANES].astype(o_ref.dtype)
        o_ref[:, base + LANES:base + 2 * LANES] = _rope64(
            y[:, base + LANES:base + 2 * LANES], c, s).astype(o_ref.dtype)


def _mla_kv_up_kernel(lat_ref, g_ref, w_ref, kr_ref, cos_ref, sin_ref, k_ref, v_ref):
    y = jnp.dot(_rms(lat_ref, g_ref), w_ref[...], preferred_element_type=F32)
    k_pe = _rope64(kr_ref[...].astype(F32), cos_ref[...], sin_ref[...]).astype(k_ref.dtype)
    for h in range(MLA_HEADS):
        base = h * (MLA_NOPE + MLA_V)
        k_ref[:, h * MLA_QK_PAD:h * MLA_QK_PAD + LANES] = y[:, base:base + MLA_NOPE].astype(k_ref.dtype)
        k_ref[:, h * MLA_QK_PAD + LANES:(h + 1) * MLA_QK_PAD] = k_pe
        v_ref[:, h * MLA_V:(h + 1) * MLA_V] = y[:, base + MLA_NOPE:base + MLA_NOPE + MLA_V].astype(v_ref.dtype)


def _mla_up(proj, q_g, w_q, kv_g, w_kv, cos_t, sin_t, seq, tm):
    m = proj.shape[0]
    pos_blocks = seq // tm
    tab = pl.BlockSpec((tm, LANES), lambda i: (i % pos_blocks, 0))
    full = lambda a: pl.BlockSpec(a.shape, lambda i: (0,) * a.ndim)
    q_all = pl.pallas_call(
        _mla_q_up_kernel,
        grid=(m // tm,),
        in_specs=[pl.BlockSpec((tm, MLA_Q_RANK), lambda i: (i, C0_QLAT)), full(q_g), full(w_q), tab, tab],
        out_specs=pl.BlockSpec((tm, MLA_HEADS * MLA_QK_PAD), lambda i: (i, 0)),
        out_shape=jax.ShapeDtypeStruct((m, MLA_HEADS * MLA_QK_PAD), BF16),
        compiler_params=_params(1),
    )(proj, q_g, w_q, cos_t, sin_t)
    k_all, v_all = pl.pallas_call(
        _mla_kv_up_kernel,
        grid=(m // tm,),
        in_specs=[pl.BlockSpec((tm, MLA_KV_RANK), lambda i: (i, C0_KVLAT)), full(kv_g), full(w_kv),
                  pl.BlockSpec((tm, LANES), lambda i: (i, C0_KROPE)), tab, tab],
        out_specs=[pl.BlockSpec((tm, MLA_HEADS * MLA_QK_PAD), lambda i: (i, 0)),
                   pl.BlockSpec((tm, WB), lambda i: (i, 0))],
        out_shape=[jax.ShapeDtypeStruct((m, MLA_HEADS * MLA_QK_PAD), BF16),
                   jax.ShapeDtypeStruct((m, WB), BF16)],
        compiler_params=_params(1),
    )(proj, kv_g, w_kv, proj, cos_t, sin_t)
    return q_all, k_all, v_all


def _mla_attn_kernel(q_ref, k_ref, v_ref, g_ref, o_ref, *, tk):
    q = q_ref[...]
    tq = q.shape[0]
    dn = (((1,), (1,)), ((), ()))

    def body(c, carry):
        m, l, acc = carry
        start = pl.multiple_of(c * tk, tk)
        k = k_ref[pl.ds(start, tk), :]
        v = v_ref[pl.ds(start, tk), :]
        s = lax.dot_general(q, k, dn, preferred_element_type=F32)
        m_new = jnp.maximum(m, jnp.max(s, axis=1, keepdims=True))
        alpha = jnp.exp(m - m_new)
        p = jnp.exp(s - m_new)
        l = alpha * l + jnp.sum(p, axis=1, keepdims=True)
        acc = alpha * acc + jnp.dot(p.astype(BF16), v, preferred_element_type=F32)
        return m_new, l, acc

    m0 = jnp.full((tq, 1), NEG, F32)
    l0 = jnp.zeros((tq, 1), F32)
    a0 = jnp.zeros((tq, MLA_V), F32)
    _, l, acc = lax.fori_loop(0, k_ref.shape[0] // tk, body, (m0, l0, a0))
    g = g_ref[...].astype(F32)
    o_ref[...] = (acc / l * (g * jax.nn.sigmoid(g))).astype(o_ref.dtype)


def _mla_attention(q_all, k_all, v_all, proj, batch, seq, tq, tk):
    nq = seq // tq
    return pl.pallas_call(
        functools.partial(_mla_attn_kernel, tk=tk),
        grid=(batch, MLA_HEADS, nq),
        in_specs=[pl.BlockSpec((tq, MLA_QK_PAD), lambda b, h, i: (b * nq + i, h)),
                  pl.BlockSpec((seq, MLA_QK_PAD), lambda b, h, i: (b, h)),
                  pl.BlockSpec((seq, MLA_V), lambda b, h, i: (b, h)),
                  pl.BlockSpec((tq, LANES), lambda b, h, i: (b * nq + i, C0_GB + h))],
        out_specs=pl.BlockSpec((tq, MLA_V), lambda b, h, i: (b * nq + i, h)),
        out_shape=jax.ShapeDtypeStruct((batch * seq, WB), BF16),
        compiler_params=_params(3),
    )(q_all, k_all, v_all, proj)


def _out_ln_kernel(y1_ref, y2_ref, w_ref, x_ref, g_ref, b_ref, o_ref, ob_ref):
    half = y1_ref.shape[1]
    y = jnp.dot(y1_ref[...], w_ref[:half, :], preferred_element_type=F32)
    y = y + jnp.dot(y2_ref[...], w_ref[half:, :], preferred_element_type=F32)
    z = ALPHA * x_ref[...] + y
    mu = jnp.mean(z, axis=1, keepdims=True)
    zc = z - mu
    var = jnp.mean(zc * zc, axis=1, keepdims=True)
    out = zc * lax.rsqrt(var + LN_EPS) * g_ref[...] + b_ref[...]
    o_ref[...] = out
    ob_ref[...] = out.astype(ob_ref.dtype)


def _out_ln(y1, y1_col, y2, y2_col, w, x, g, b, tm):
    m, d = x.shape
    half = w.shape[0] // 2
    full = lambda a: pl.BlockSpec(a.shape, lambda i: (0,) * a.ndim)
    return pl.pallas_call(
        _out_ln_kernel,
        grid=(m // tm,),
        in_specs=[pl.BlockSpec((tm, half), lambda i: (i, y1_col)),
                  pl.BlockSpec((tm, half), lambda i: (i, y2_col)),
                  full(w), pl.BlockSpec((tm, d), lambda i: (i, 0)), full(g), full(b)],
        out_specs=[pl.BlockSpec((tm, d), lambda i: (i, 0)), pl.BlockSpec((tm, d), lambda i: (i, 0))],
        out_shape=[jax.ShapeDtypeStruct((m, d), F32), jax.ShapeDtypeStruct((m, d), BF16)],
        compiler_params=_params(1),
    )(y1, y2, w, x, g, b)


DIL_TQ = 256
DIL_NBLK = 2 * DIL_HALF // DIL_TQ + 1


def _dil_bias_table():
    i = jnp.arange(DIL_TQ, dtype=jnp.int32)
    blk = jnp.arange(DIL_NBLK, dtype=jnp.int32) - DIL_NBLK // 2
    d = blk[:, None, None] * DIL_TQ + i[None, None, :] - i[None, :, None]
    cnt = jnp.zeros(d.shape, F32)
    for window, dil in DIL_PAIRS:
        cnt = cnt + ((jnp.abs(d) <= window // 2) & (d % dil == 0)).astype(F32)
    return jnp.where(cnt > 0, jnp.log(jnp.maximum(cnt, 1.0)), NEG)


def _dil_attn_kernel(q_ref, k_ref, v_ref, g_ref, bias_ref, o_ref):
    qi = pl.program_id(2)
    nkb = k_ref.shape[0] // DIL_TQ
    q = q_ref[...]
    dn = (((1,), (1,)), ((), ()))
    side = DIL_NBLK // 2

    def body(c, carry):
        m, l, acc = carry
        start = pl.multiple_of((qi - side + c) * DIL_TQ, DIL_TQ)
        k = k_ref[pl.ds(start, DIL_TQ), :]
        v = v_ref[pl.ds(start, DIL_TQ), :]
        s = lax.dot_general(q, k, dn, preferred_element_type=F32) + bias_ref[c]
        m_new = jnp.maximum(m, jnp.max(s, axis=1, keepdims=True))
        alpha = jnp.exp(m - m_new)
        p = jnp.exp(s - m_new)
        l = alpha * l + jnp.sum(p, axis=1, keepdims=True)
        acc = alpha * acc + jnp.dot(p.astype(BF16), v, preferred_element_type=F32)
        return m_new, l, acc

    lo = jnp.maximum(0, side - qi)
    hi = jnp.minimum(DIL_NBLK, nkb + side - qi)
    m0 = jnp.full((DIL_TQ, 1), NEG, F32)
    l0 = jnp.zeros((DIL_TQ, 1), F32)
    a0 = jnp.zeros((DIL_TQ, HEAD_DIM), F32)
    _, l, acc = lax.fori_loop(lo, hi, body, (m0, l0, a0))
    g = g_ref[...].astype(F32)
    o_ref[...] = (acc / l * (g * jax.nn.sigmoid(g))).astype(o_ref.dtype)


def _dil_attention(proj, bias, batch, seq):
    nq = seq // DIL_TQ
    return pl.pallas_call(
        _dil_attn_kernel,
        grid=(batch, DIL_HEADS, nq),
        in_specs=[pl.BlockSpec((DIL_TQ, LANES), lambda b, h, i: (b * nq + i, h)),
                  pl.BlockSpec((seq, LANES), lambda b, h, i: (b, DIL_HEADS + h)),
                  pl.BlockSpec((seq, LANES), lambda b, h, i: (b, 2 * DIL_HEADS + h)),
                  pl.BlockSpec((DIL_TQ, LANES), lambda b, h, i: (b * nq + i, 3 * DIL_HEADS + h)),
                  pl.BlockSpec(bias.shape, lambda b, h, i: (0, 0, 0))],
        out_specs=pl.BlockSpec((DIL_TQ, LANES), lambda b, h, i: (b * nq + i, h)),
        out_shape=jax.ShapeDtypeStruct((batch * seq, WC), BF16),
        compiler_params=_params(3),
    )(proj, proj, proj, proj, bias)


def _rope_tables(seq, half):
    inv = ROPE_THETA ** (-jnp.arange(half, dtype=F32) / half)
    ang = jnp.arange(seq, dtype=F32)[:, None] * inv[None, :]
    cos, sin = jnp.cos(ang), jnp.sin(ang)
    pad = jnp.zeros((seq, LANES - 2 * half), F32)
    return jnp.concatenate([cos, cos, pad], 1), jnp.concatenate([-sin, sin, pad], 1)


def _prep_weights(ab_w_in, ab_w_q_up, ab_w_kv_up, ab_w_out, c_w_in, c_w_out):
    qa, ka, va, ga, q_lat, kv_lat, k_rope, gb = jnp.split(
        ab_w_in, [WA, 2 * WA, 3 * WA, 4 * WA, 4 * WA + MLA_Q_RANK,
                  4 * WA + MLA_Q_RANK + MLA_KV_RANK, 4 * WA + MLA_Q_RANK + MLA_KV_RANK + MLA_ROPE], axis=1)
    w_in0 = jnp.concatenate(
        [qa * HEAD_DIM ** -0.5, ka, va, ga, q_lat, kv_lat, gb, k_rope,
         jnp.zeros((D_MODEL, LANES - MLA_ROPE), F32)], axis=1).astype(BF16)
    wq = (ab_w_q_up * (MLA_NOPE + MLA_ROPE) ** -0.5).reshape(MLA_Q_RANK, MLA_HEADS, MLA_NOPE + MLA_ROPE)
    wq = jnp.pad(wq, ((0, 0), (0, 0), (0, MLA_QK_PAD - MLA_NOPE - MLA_ROPE)))
    w_q = wq.reshape(MLA_Q_RANK, MLA_HEADS * MLA_QK_PAD).astype(BF16)
    w_kv = ab_w_kv_up.astype(BF16)
    qc, rest = c_w_in[:, :WC], c_w_in[:, WC:]
    w_in1 = jnp.concatenate([qc * HEAD_DIM ** -0.5, rest], axis=1).astype(BF16)
    return w_in0, w_q, w_kv, ab_w_out.astype(BF16), w_in1, c_w_out.astype(BF16)


def _trunk(x, w, tabs):
    batch, seq, _ = x.shape
    (w_in0, w_q, w_kv, w_out0, w_in1, w_out1, na_bias, q_g, kv_g,
     ln0_g, ln0_b, ln1_g, ln1_b, dil_bias) = w
    cos64, sin64, cos128, sin128 = tabs
    x2 = x.reshape(batch * seq, D_MODEL)
    proj0 = _proj_f32(x2, w_in0, tm=1024, tn=896)
    ya = _na_attention(proj0, na_bias, batch, seq)
    q_all, k_all, v_all = _mla_up(proj0, q_g, w_q, kv_g, w_kv, cos64, sin64, seq, tm=512)
    yb = _mla_attention(q_all, k_all, v_all, proj0, batch, seq, tq=512, tk=512)
    x1, x1b = _out_ln(ya, 0, yb, 0, w_out0, x2, ln0_g, ln0_b, tm=256)
    proj1 = _proj_rope(x1b, w_in1, cos128, sin128, seq, tm=1024, tn=1024, rope_cols=2 * WC)
    yc = _dil_attention(proj1, dil_bias, batch, seq)
    y, _ = _out_ln(yc, 0, yc, 1, w_out1, x1, ln1_g, ln1_b, tm=256)
    return y.reshape(batch, seq, D_MODEL)


def kernel(x_prompt, x_sample, ab_w_in, ab_rpb, ab_q_norm_g, ab_w_q_up, ab_kv_norm_g, ab_w_kv_up,
           ab_w_out, ab_ln_g, ab_ln_b, c_w_in, c_w_out, c_ln_g, c_ln_b):
    w_in0, w_q, w_kv, w_out0, w_in1, w_out1 = _prep_weights(
        ab_w_in, ab_w_q_up, ab_w_kv_up, ab_w_out, c_w_in, c_w_out)
    row = lambda a: a.reshape(1, -1).astype(F32)
    w = (w_in0, w_q, w_kv, w_out0, w_in1, w_out1, _na_bias(ab_rpb), row(ab_q_norm_g), row(ab_kv_norm_g),
         row(ab_ln_g), row(ab_ln_b), row(c_ln_g), row(c_ln_b), _dil_bias_table())
    outs = []
    for x in (x_prompt, x_sample):
        seq = x.shape[1]
        tabs = _rope_tables(seq, MLA_ROPE // 2) + _rope_tables(seq, HEAD_DIM // 2)
        outs.append(_trunk(x, w, tabs))
    return tuple(outs)
```

```python
import functools

import jax
import jax.numpy as jnp
from jax import lax
from jax.experimental import pallas as pl
from jax.experimental.pallas import tpu as pltpu

F32 = jnp.float32
BF16 = jnp.bfloat16

D_MODEL = 2048
DEPTH = 2
GRID_W = 64
HEAD_DIM = 128
NA_HEADS = 8
NA_WIN_ROWS = 8
NA_WIN_COLS = 16
MLA_HEADS = 8
MLA_Q_RANK = 512
MLA_KV_RANK = 512
MLA_NOPE = 128
MLA_ROPE = 64
MLA_V = 128
DIL_HEADS = 16
DIL_PAIRS = ((128, 1), (512, 4), (2048, 16))
DIL_HALF = 1024
WA = NA_HEADS * HEAD_DIM
WB = MLA_HEADS * MLA_V
WC = DIL_HEADS * HEAD_DIM
ROPE_THETA = 10000.0
ALPHA = (2 * DEPTH) ** 0.25
LN_EPS = 1e-5
RMS_EPS = 1e-6
NEG = -1e30
LOG2E = 1.4426950408889634

LANES = 128
MLA_QK_PAD = 2 * LANES
IN0_PAD_WIDTH = 4 * WA + MLA_Q_RANK + MLA_KV_RANK + WB + LANES
VMEM_LIMIT = 56 * 1024 * 1024

C0_Q, C0_K, C0_V, C0_G = 0, 8, 16, 24
C0_QLAT, C0_KVLAT = 8, 9
C0_GB = 40
C0_KROPE = 48

NA_QROWS = 8
NA_KROWS = 16
NA_TQ = NA_QROWS * GRID_W
NA_TK = NA_KROWS * GRID_W


def _params(n_axes):
    return pltpu.CompilerParams(dimension_semantics=("arbitrary",) * n_axes,
                                vmem_limit_bytes=VMEM_LIMIT)


def _proj_f32_kernel(x_ref, w_ref, o_ref, xb_ref):
    @pl.when(pl.program_id(1) == 0)
    def _():
        xb_ref[...] = x_ref[...].astype(BF16)

    o_ref[...] = jnp.dot(xb_ref[...], w_ref[...], preferred_element_type=F32).astype(o_ref.dtype)


def _proj_f32(x, w, tm, tn):
    m, k = x.shape
    n = w.shape[1]
    return pl.pallas_call(
        _proj_f32_kernel,
        grid=(m // tm, n // tn),
        in_specs=[pl.BlockSpec((tm, k), lambda i, j: (i, 0)),
                  pl.BlockSpec((k, tn), lambda i, j: (0, j))],
        out_specs=pl.BlockSpec((tm, tn), lambda i, j: (i, j)),
        out_shape=jax.ShapeDtypeStruct((m, n), BF16),
        scratch_shapes=[pltpu.VMEM((tm, k), BF16)],
        compiler_params=_params(2),
    )(x, w)


def _proj_rope_kernel(x_ref, w_ref, cos_ref, sin_ref, o_ref, *, n_rope_blocks):
    y = jnp.dot(x_ref[...], w_ref[...], preferred_element_type=F32)
    j = pl.program_id(1)

    heads = y.shape[1] // LANES

    @pl.when(j < n_rope_blocks)
    def _():
        c = cos_ref[...]
        s = sin_ref[...]
        for h in range(heads):
            yh = y[:, h * LANES:(h + 1) * LANES]
            o_ref[h] = (yh * c + pltpu.roll(yh, LANES // 2, 1) * s).astype(o_ref.dtype)

    @pl.when(j >= n_rope_blocks)
    def _():
        for h in range(heads):
            o_ref[h] = y[:, h * LANES:(h + 1) * LANES].astype(o_ref.dtype)


def _proj_rope(x, w, cos_t, sin_t, seq, tm, tn, rope_cols):
    m, k = x.shape
    n = w.shape[1]
    pos_blocks = seq // tm
    return pl.pallas_call(
        functools.partial(_proj_rope_kernel, n_rope_blocks=rope_cols // tn),
        grid=(m // tm, n // tn),
        in_specs=[pl.BlockSpec((tm, k), lambda i, j: (i, 0)),
                  pl.BlockSpec((k, tn), lambda i, j: (0, j)),
                  pl.BlockSpec((tm, LANES), lambda i, j: (i % pos_blocks, 0)),
                  pl.BlockSpec((tm, LANES), lambda i, j: (i % pos_blocks, 0))],
        out_specs=pl.BlockSpec((tn // LANES, tm, LANES), lambda i, j: (j, i, 0)),
        out_shape=jax.ShapeDtypeStruct((n // LANES, m, LANES), BF16),
        compiler_params=_params(2),
    )(x, w, cos_t, sin_t)


def _na_bias_kernel(rpb_ref, o_ref, t_ref):
    h = pl.program_id(0)
    n_dc = 2 * NA_WIN_COLS - 1
    c = lax.broadcasted_iota(jnp.int32, (GRID_W, GRID_W), 0)
    kc = lax.broadcasted_iota(jnp.int32, (GRID_W, GRID_W), 1)
    d = kc - c + (NA_WIN_COLS - 1)
    cs = jnp.clip(c - NA_WIN_COLS // 2, 0, GRID_W - NA_WIN_COLS)
    col_ok = (kc >= cs) & (kc < cs + NA_WIN_COLS)
    for dr in range(2 * NA_WIN_ROWS - 1):
        t = jnp.zeros((GRID_W, GRID_W), F32)
        for dd in range(n_dc):
            t = jnp.where(d == dd, rpb_ref[h, dr * n_dc + dd] * LOG2E, t)
        t_ref[dr] = jnp.where(col_ok, t, NEG)
    neg_blk = jnp.full((GRID_W, GRID_W), NEG, F32)
    for typ in range(3):
        for a in range(NA_QROWS):
            if typ == 0:
                lo, dr0 = max(a - 4, 0), 7 - a
            elif typ == 1:
                lo, dr0 = a, 3 - a
            else:
                lo, dr0 = 8 + min(a - 4, 0), -1 - a
            for kb in range(NA_KROWS):
                ok = lo <= kb < lo + NA_WIN_ROWS
                blk = t_ref[kb + dr0] if ok else neg_blk
                o_ref[0, typ, a * GRID_W:(a + 1) * GRID_W, kb * GRID_W:(kb + 1) * GRID_W] = blk


def _na_bias(rpb):
    rpb2 = rpb.reshape(NA_HEADS, -1)
    return pl.pallas_call(
        _na_bias_kernel,
        grid=(NA_HEADS,),
        in_specs=[pl.BlockSpec(memory_space=pltpu.SMEM)],
        out_specs=pl.BlockSpec((1, 3, NA_TQ, NA_TK), lambda h: (h, 0, 0, 0)),
        out_shape=jax.ShapeDtypeStruct((NA_HEADS, 3, NA_TQ, NA_TK), F32),
        scratch_shapes=[pltpu.VMEM((2 * NA_WIN_ROWS - 1, GRID_W, GRID_W), F32)],
        compiler_params=_params(1),
    )(rpb2)


def _na_kernel(q_ref, k0_ref, k1_ref, k2_ref, k3_ref, v0_ref, v1_ref, v2_ref, v3_ref,
               g_ref, b_ref, o_ref):
    q = q_ref[...]
    dn = (((1,), (1,)), ((), ()))
    s = jnp.concatenate(
        [lax.dot_general(q, kr[...], dn, preferred_element_type=F32)
         for kr in (k0_ref, k1_ref, k2_ref, k3_ref)], axis=1)
    s = s + b_ref[0, 0]
    m = jnp.max(s, axis=1, keepdims=True)
    p = jnp.exp2(s - m)
    l = jnp.sum(p, axis=1, keepdims=True)
    pb = p.astype(BF16)
    kq = NA_TK // 4
    o = jnp.zeros((NA_TQ, HEAD_DIM), F32)
    for i, vr in enumerate((v0_ref, v1_ref, v2_ref, v3_ref)):
        o = o + jnp.dot(pb[:, i * kq:(i + 1) * kq], vr[...], preferred_element_type=F32)
    g = g_ref[...].astype(F32)
    o_ref[...] = (o / l * (g * jax.nn.sigmoid(g))).astype(o_ref.dtype)


def _na_attention(proj, bias, batch, seq):
    rows = seq // GRID_W
    nj = rows // NA_QROWS
    kq = NA_TK // 4
    parts_per_seq = seq // kq

    def kbase(j):
        return jnp.clip(2 * j - 1, 0, rows // 4 - 4)

    def kv_spec(col0, part):
        return pl.BlockSpec((kq, LANES),
                            lambda h, j, b: (b * parts_per_seq + kbase(j) + part, col0 + h))

    def btype(j):
        return jnp.where(j == 0, 0, jnp.where(j == nj - 1, 2, 1))

    tok_spec = lambda col0: pl.BlockSpec((NA_TQ, LANES), lambda h, j, b: (b * nj + j, col0 + h))
    in_specs = ([tok_spec(C0_Q)] + [kv_spec(C0_K, i) for i in range(4)]
                + [kv_spec(C0_V, i) for i in range(4)] + [tok_spec(C0_G)]
                + [pl.BlockSpec((1, 1, NA_TQ, NA_TK), lambda h, j, b: (h, btype(j), 0, 0))])
    return pl.pallas_call(
        _na_kernel,
        grid=(NA_HEADS, nj, batch),
        in_specs=in_specs,
        out_specs=pl.BlockSpec((NA_TQ, LANES), lambda h, j, b: (b * nj + j, h)),
        out_shape=jax.ShapeDtypeStruct((batch * seq, WA), BF16),
        compiler_params=_params(3),
    )(*([proj] * 10), bias)


def _rms(lat_ref, g_ref):
    x = lat_ref[...].astype(F32)
    ms = jnp.mean(x * x, axis=1, keepdims=True)
    return (x * lax.rsqrt(ms + RMS_EPS) * g_ref[...]).astype(BF16)


def _rope64(t, cos_t, sin_t):
    lane = lax.broadcasted_iota(jnp.int32, t.shape, 1)
    half = MLA_ROPE // 2
    partner = jnp.where(lane < half, pltpu.roll(t, LANES - half, 1), pltpu.roll(t, half, 1))
    return t * cos_t + partner * sin_t


def _mla_q_up_kernel(lat_ref, g_ref, w_ref, cos_ref, sin_ref, o_ref):
    y = jnp.dot(_rms(lat_ref, g_ref), w_ref[...], preferred_element_type=F32)
    c = cos_ref[...]
    s = sin_ref[...]
    for h in range(MLA_HEADS):
        base = h * MLA_QK_PAD
        o_ref[:, base:base + LANES] = y[:, base:base + LANES].astype(o_ref.dtype)
        o_ref[:, base + LANES:base + 2 * LANES] = _rope64(
            y[:, base + LANES:base + 2 * LANES], c, s).astype(o_ref.dtype)


def _mla_kv_up_kernel(lat_ref, g_ref, w_ref, kr_ref, cos_ref, sin_ref, k_ref, v_ref):
    y = jnp.dot(_rms(lat_ref, g_ref), w_ref[...], preferred_element_type=F32)
    k_pe = _rope64(kr_ref[...].astype(F32), cos_ref[...], sin_ref[...]).astype(k_ref.dtype)
    for h in range(MLA_HEADS):
        base = h * (MLA_NOPE + MLA_V)
        k_ref[:, h * MLA_QK_PAD:h * MLA_QK_PAD + LANES] = y[:, base:base + MLA_NOPE].astype(k_ref.dtype)
        k_ref[:, h * MLA_QK_PAD + LANES:(h + 1) * MLA_QK_PAD] = k_pe
        v_ref[:, h * MLA_V:(h + 1) * MLA_V] = y[:, base + MLA_NOPE:base + MLA_NOPE + MLA_V].astype(v_ref.dtype)


def _mla_up(proj, q_g, w_q, kv_g, w_kv, cos_t, sin_t, seq, tm):
    m = proj.shape[0]
    pos_blocks = seq // tm
    tab = pl.BlockSpec((tm, LANES), lambda i: (i % pos_blocks, 0))
    full = lambda a: pl.BlockSpec(a.shape, lambda i: (0,) * a.ndim)
    q_all = pl.pallas_call(
        _mla_q_up_kernel,
        grid=(m // tm,),
        in_specs=[pl.BlockSpec((tm, MLA_Q_RANK), lambda i: (i, C0_QLAT)), full(q_g), full(w_q), tab, tab],
        out_specs=pl.BlockSpec((tm, MLA_HEADS * MLA_QK_PAD), lambda i: (i, 0)),
        out_shape=jax.ShapeDtypeStruct((m, MLA_HEADS * MLA_QK_PAD), BF16),
        compiler_params=_params(1),
    )(proj, q_g, w_q, cos_t, sin_t)
    k_all, v_all = pl.pallas_call(
        _mla_kv_up_kernel,
        grid=(m // tm,),
        in_specs=[pl.BlockSpec((tm, MLA_KV_RANK), lambda i: (i, C0_KVLAT)), full(kv_g), full(w_kv),
                  pl.BlockSpec((tm, LANES), lambda i: (i, C0_KROPE)), tab, tab],
        out_specs=[pl.BlockSpec((tm, MLA_HEADS * MLA_QK_PAD), lambda i: (i, 0)),
                   pl.BlockSpec((tm, WB), lambda i: (i, 0))],
        out_shape=[jax.ShapeDtypeStruct((m, MLA_HEADS * MLA_QK_PAD), BF16),
                   jax.ShapeDtypeStruct((m, WB), BF16)],
        compiler_params=_params(1),
    )(proj, kv_g, w_kv, proj, cos_t, sin_t)
    return q_all, k_all, v_all


def _mla_attn_kernel(q_ref, k_ref, v_ref, g_ref, o_ref, *, tk):
    q = q_ref[...]
    tq = q.shape[0]
    dn = (((1,), (1,)), ((), ()))

    def body(c, carry):
        m, l, acc = carry
        start = pl.multiple_of(c * tk, tk)
        k = k_ref[pl.ds(start, tk), :]
        v = v_ref[pl.ds(start, tk), :]
        s = lax.dot_general(q, k, dn, preferred_element_type=F32)
        m_new = jnp.maximum(m, jnp.max(s, axis=1, keepdims=True))
        alpha = jnp.exp2(m - m_new)
        p = jnp.exp2(s - m_new)
        l = alpha * l + jnp.sum(p, axis=1, keepdims=True)
        acc = alpha * acc + jnp.dot(p.astype(BF16), v, preferred_element_type=F32)
        return m_new, l, acc

    m0 = jnp.full((tq, 1), NEG, F32)
    l0 = jnp.zeros((tq, 1), F32)
    a0 = jnp.zeros((tq, MLA_V), F32)
    _, l, acc = lax.fori_loop(0, k_ref.shape[0] // tk, body, (m0, l0, a0))
    g = g_ref[...].astype(F32)
    o_ref[...] = (acc / l * (g * jax.nn.sigmoid(g))).astype(o_ref.dtype)


def _mla_attention(q_all, k_all, v_all, proj, batch, seq, tq, tk):
    nq = seq // tq
    return pl.pallas_call(
        functools.partial(_mla_attn_kernel, tk=tk),
        grid=(batch, MLA_HEADS, nq),
        in_specs=[pl.BlockSpec((tq, MLA_QK_PAD), lambda b, h, i: (b * nq + i, h)),
                  pl.BlockSpec((seq, MLA_QK_PAD), lambda b, h, i: (b, h)),
                  pl.BlockSpec((seq, MLA_V), lambda b, h, i: (b, h)),
                  pl.BlockSpec((tq, LANES), lambda b, h, i: (b * nq + i, C0_GB + h))],
        out_specs=pl.BlockSpec((tq, MLA_V), lambda b, h, i: (b * nq + i, h)),
        out_shape=jax.ShapeDtypeStruct((batch * seq, WB), BF16),
        compiler_params=_params(3),
    )(q_all, k_all, v_all, proj)


def _out_ln_kernel(y1_ref, y2_ref, w_ref, x_ref, g_ref, b_ref, o_ref, ob_ref):
    half = y1_ref.shape[1]
    y = jnp.dot(y1_ref[...], w_ref[:half, :], preferred_element_type=F32)
    y = y + jnp.dot(y2_ref[...], w_ref[half:, :], preferred_element_type=F32)
    z = ALPHA * x_ref[...] + y
    mu = jnp.mean(z, axis=1, keepdims=True)
    zc = z - mu
    var = jnp.mean(zc * zc, axis=1, keepdims=True)
    out = zc * lax.rsqrt(var + LN_EPS) * g_ref[...] + b_ref[...]
    o_ref[...] = out
    ob_ref[...] = out.astype(ob_ref.dtype)


def _out_ln(y1, y1_col, y2, y2_col, w, x, g, b, tm):
    m, d = x.shape
    half = w.shape[0] // 2
    full = lambda a: pl.BlockSpec(a.shape, lambda i: (0,) * a.ndim)
    return pl.pallas_call(
        _out_ln_kernel,
        grid=(m // tm,),
        in_specs=[pl.BlockSpec((tm, half), lambda i: (i, y1_col)),
                  pl.BlockSpec((tm, half), lambda i: (i, y2_col)),
                  full(w), pl.BlockSpec((tm, d), lambda i: (i, 0)), full(g), full(b)],
        out_specs=[pl.BlockSpec((tm, d), lambda i: (i, 0)), pl.BlockSpec((tm, d), lambda i: (i, 0))],
        out_shape=[jax.ShapeDtypeStruct((m, d), F32), jax.ShapeDtypeStruct((m, d), BF16)],
        compiler_params=_params(1),
    )(y1, y2, w, x, g, b)


DIL_DILS = tuple(d for _, d in DIL_PAIRS)
DIL_SIDE = DIL_PAIRS[0][0] // 2 // DIL_PAIRS[0][1]
DIL_TQ = 128
DIL_TK = DIL_TQ + 2 * DIL_SIDE
DIL_SB = DIL_TQ * DIL_DILS[-1]
assert all(w // 2 // d == DIL_SIDE for w, d in DIL_PAIRS)


def _dil_kernel(*refs, nsb):
    q_refs = refs[0:3]
    k_refs = [refs[3 + 3 * p:6 + 3 * p] for p in range(3)]
    v_refs = [refs[12 + 3 * p:15 + 3 * p] for p in range(3)]
    g_ref, o_ref, bias_s, o_s, l_s = refs[21:]
    sb = pl.program_id(2)
    dn = (((1,), (1,)), ((), ()))

    qi = lax.broadcasted_iota(jnp.int32, (DIL_TQ, DIL_TK), 0)
    kj = lax.broadcasted_iota(jnp.int32, (DIL_TQ, DIL_TK), 1)
    base = jnp.where((kj >= qi) & (kj <= qi + 2 * DIL_SIDE), 0.0, NEG).astype(F32)
    lo = jnp.where(kj < DIL_SIDE, NEG, 0.0) * (sb == 0).astype(F32)
    hi = jnp.where(kj >= DIL_TK - DIL_SIDE, NEG, 0.0) * (sb == nsb - 1).astype(F32)
    bias_s[0] = base
    bias_s[1] = base + lo
    bias_s[2] = base + hi
    bias_s[3] = base + lo + hi

    def window(trio, lanes, c, n):
        main, prev, nxt = trio
        if n == 1:
            return jnp.concatenate([prev[:, lanes], main[:, lanes], nxt[:, lanes]], axis=0), 3
        if c == 0:
            return jnp.concatenate([prev[:, lanes], main[0:DIL_TK - DIL_SIDE, lanes]], axis=0), 1
        if c == n - 1:
            return jnp.concatenate([main[n * DIL_TQ - (DIL_TK - DIL_SIDE):n * DIL_TQ, lanes],
                                    nxt[:, lanes]], axis=0), 2
        return main[c * DIL_TQ - DIL_SIDE:c * DIL_TQ + DIL_TQ + DIL_SIDE, lanes], 0

    def tile(q, kw, vw, mask_id, pair, rows):
        s = lax.dot_general(q, kw, dn, preferred_element_type=F32) + bias_s[mask_id]
        m = jnp.max(s, axis=1, keepdims=True)
        p = jnp.exp2(s - m)
        l = jnp.sum(p, axis=1, keepdims=True)
        pv = jnp.dot(p.astype(BF16), vw, preferred_element_type=F32)
        o_s[pair, rows, :] = pv * (1.0 / l)
        l_s[pair, rows, :] = jnp.broadcast_to(m + jnp.log2(l), (DIL_TQ, LANES))

    for pair, dil in enumerate(DIL_DILS):
        n = DIL_SB // dil // DIL_TQ
        for r in range(dil):
            lanes = slice(r * LANES, (r + 1) * LANES)
            for c in range(n):
                kw, mask_id = window(k_refs[pair], lanes, c, n)
                vw, _ = window(v_refs[pair], lanes, c, n)
                q = q_refs[pair][c * DIL_TQ:(c + 1) * DIL_TQ, lanes]
                start = (c * DIL_TQ) * dil + r
                rows = pl.ds(start, DIL_TQ, stride=dil) if dil > 1 else pl.ds(start, DIL_TQ)
                tile(q, kw, vw, mask_id, pair, rows)

    def combine(c, carry):
        rows = pl.ds(pl.multiple_of(c * DIL_TQ, DIL_TQ), DIL_TQ)
        lse = [l_s[pair, rows, :] for pair in range(3)]
        top = jnp.maximum(jnp.maximum(lse[0], lse[1]), lse[2])
        w = [jnp.exp2(x - top) for x in lse]
        num = w[0] * o_s[0, rows, :] + w[1] * o_s[1, rows, :] + w[2] * o_s[2, rows, :]
        g = g_ref[rows, :].astype(F32)
        o_ref[rows, :] = (num / (w[0] + w[1] + w[2]) * (g * jax.nn.sigmoid(g))).astype(o_ref.dtype)
        return carry

    lax.fori_loop(0, DIL_SB // DIL_TQ, combine, 0)


def _dil_attention(proj, batch, seq):
    heads, tokens, _ = proj.shape
    nsb = seq // DIL_SB
    halo = DIL_SIDE
    views, specs_main, specs_halo = [], [], []
    for dil in DIL_DILS:
        rows, width = tokens // dil, dil * LANES
        views.append(proj.reshape(heads, rows, width))
        main_rows = DIL_SB // dil
        per_main = main_rows // halo
        last = rows // halo - 1

        def main_spec(h0, main_rows=main_rows, width=width):
            return pl.BlockSpec((None, main_rows, width), lambda b, h, s: (h0 + h, b * nsb + s, 0))

        def prev_spec(h0, per_main=per_main, width=width):
            return pl.BlockSpec((None, halo, width),
                                lambda b, h, s: (h0 + h, jnp.maximum((b * nsb + s) * per_main - 1, 0), 0))

        def next_spec(h0, per_main=per_main, width=width, last=last):
            return pl.BlockSpec((None, halo, width),
                                lambda b, h, s: (h0 + h, jnp.minimum((b * nsb + s + 1) * per_main, last), 0))

        specs_main.append(main_spec)
        specs_halo.append((prev_spec, next_spec))
    in_specs = [specs_main[p](0) for p in range(3)]
    operands = list(views)
    for h0 in (DIL_HEADS, 2 * DIL_HEADS):
        for p in range(3):
            in_specs += [specs_main[p](h0), specs_halo[p][0](h0), specs_halo[p][1](h0)]
            operands += [views[p]] * 3
    in_specs.append(specs_main[0](3 * DIL_HEADS))
    operands.append(views[0])
    return pl.pallas_call(
        functools.partial(_dil_kernel, nsb=nsb),
        grid=(batch, DIL_HEADS, nsb),
        in_specs=in_specs,
        out_specs=pl.BlockSpec((DIL_SB, LANES), lambda b, h, s: (b * nsb + s, h)),
        out_shape=jax.ShapeDtypeStruct((tokens, WC), BF16),
        scratch_shapes=[pltpu.VMEM((4, DIL_TQ, DIL_TK), F32),
                        pltpu.VMEM((3, DIL_SB, LANES), F32),
                        pltpu.VMEM((3, DIL_SB, LANES), F32)],
        compiler_params=_params(3),
    )(*operands)


def _rope_tables(seq, half):
    inv = ROPE_THETA ** (-jnp.arange(half, dtype=F32) / half)
    ang = jnp.arange(seq, dtype=F32)[:, None] * inv[None, :]
    cos, sin = jnp.cos(ang), jnp.sin(ang)
    pad = jnp.zeros((seq, LANES - 2 * half), F32)
    return jnp.concatenate([cos, cos, pad], 1), jnp.concatenate([-sin, sin, pad], 1)


def _prep_weights(ab_w_in, ab_w_q_up, ab_w_kv_up, ab_w_out, c_w_in, c_w_out):
    qa, ka, va, ga, q_lat, kv_lat, k_rope, gb = jnp.split(
        ab_w_in, [WA, 2 * WA, 3 * WA, 4 * WA, 4 * WA + MLA_Q_RANK,
                  4 * WA + MLA_Q_RANK + MLA_KV_RANK, 4 * WA + MLA_Q_RANK + MLA_KV_RANK + MLA_ROPE], axis=1)
    w_in0 = jnp.concatenate(
        [qa * (HEAD_DIM ** -0.5 * LOG2E), ka, va, ga, q_lat, kv_lat, gb, k_rope,
         jnp.zeros((D_MODEL, LANES - MLA_ROPE), F32)], axis=1).astype(BF16)
    wq = (ab_w_q_up * ((MLA_NOPE + MLA_ROPE) ** -0.5 * LOG2E)).reshape(
        MLA_Q_RANK, MLA_HEADS, MLA_NOPE + MLA_ROPE)
    wq = jnp.pad(wq, ((0, 0), (0, 0), (0, MLA_QK_PAD - MLA_NOPE - MLA_ROPE)))
    w_q = wq.reshape(MLA_Q_RANK, MLA_HEADS * MLA_QK_PAD).astype(BF16)
    w_kv = ab_w_kv_up.astype(BF16)
    qc, rest = c_w_in[:, :WC], c_w_in[:, WC:]
    w_in1 = jnp.concatenate([qc * (HEAD_DIM ** -0.5 * LOG2E), rest], axis=1).astype(BF16)
    return w_in0, w_q, w_kv, ab_w_out.astype(BF16), w_in1, c_w_out.astype(BF16)


def _trunk(x, w, tabs):
    batch, seq, _ = x.shape
    (w_in0, w_q, w_kv, w_out0, w_in1, w_out1, na_bias, q_g, kv_g,
     ln0_g, ln0_b, ln1_g, ln1_b) = w
    cos64, sin64, cos128, sin128 = tabs
    x2 = x.reshape(batch * seq, D_MODEL)
    proj0 = _proj_f32(x2, w_in0, tm=1024, tn=896)
    ya = _na_attention(proj0, na_bias, batch, seq)
    q_all, k_all, v_all = _mla_up(proj0, q_g, w_q, kv_g, w_kv, cos64, sin64, seq, tm=512)
    yb = _mla_attention(q_all, k_all, v_all, proj0, batch, seq, tq=512, tk=512)
    x1, x1b = _out_ln(ya, 0, yb, 0, w_out0, x2, ln0_g, ln0_b, tm=256)
    proj1 = _proj_rope(x1b, w_in1, cos128, sin128, seq, tm=1024, tn=1024, rope_cols=2 * WC)
    yc = _dil_attention(proj1, batch, seq)
    y, _ = _out_ln(yc, 0, yc, 1, w_out1, x1, ln1_g, ln1_b, tm=256)
    return y.reshape(batch, seq, D_MODEL)


def kernel(x_prompt, x_sample, ab_w_in, ab_rpb, ab_q_norm_g, ab_w_q_up, ab_kv_norm_g, ab_w_kv_up,
           ab_w_out, ab_ln_g, ab_ln_b, c_w_in, c_w_out, c_ln_g, c_ln_b):
    w_in0, w_q, w_kv, w_out0, w_in1, w_out1 = _prep_weights(
        ab_w_in, ab_w_q_up, ab_w_kv_up, ab_w_out, c_w_in, c_w_out)
    row = lambda a: a.reshape(1, -1).astype(F32)
    w = (w_in0, w_q, w_kv, w_out0, w_in1, w_out1, _na_bias(ab_rpb), row(ab_q_norm_g), row(ab_kv_norm_g),
         row(ab_ln_g), row(ab_ln_b), row(c_ln_g), row(c_ln_b))
    outs = []
    for x in (x_prompt, x_sample):
        seq = x.shape[1]
        tabs = _rope_tables(seq, MLA_ROPE // 2) + _rope_tables(seq, HEAD_DIM // 2)
        outs.append(_trunk(x, w, tabs))
    return tuple(outs)
```

```python
import functools

import jax
import jax.numpy as jnp
from jax import lax
from jax.experimental import pallas as pl
from jax.experimental.pallas import tpu as pltpu

F32 = jnp.float32
BF16 = jnp.bfloat16

D_MODEL = 2048
DEPTH = 2
GRID_W = 64
HEAD_DIM = 128
NA_HEADS = 8
NA_WIN_ROWS = 8
NA_WIN_COLS = 16
MLA_HEADS = 8
MLA_Q_RANK = 512
MLA_KV_RANK = 512
MLA_NOPE = 128
MLA_ROPE = 64
MLA_V = 128
DIL_HEADS = 16
DIL_PAIRS = ((128, 1), (512, 4), (2048, 16))
DIL_HALF = 1024
WA = NA_HEADS * HEAD_DIM
WB = MLA_HEADS * MLA_V
WC = DIL_HEADS * HEAD_DIM
ROPE_THETA = 10000.0
ALPHA = (2 * DEPTH) ** 0.25
LN_EPS = 1e-5
RMS_EPS = 1e-6
NEG = -1e30
LOG2E = 1.4426950408889634

LANES = 128
MLA_QK_PAD = 2 * LANES
IN0_PAD_WIDTH = 4 * WA + MLA_Q_RANK + MLA_KV_RANK + WB + LANES
VMEM_LIMIT = 56 * 1024 * 1024

C0_Q, C0_K, C0_V, C0_G = 0, 8, 16, 24
C0_QLAT, C0_KVLAT = 8, 9
C0_GB = 40
C0_KROPE = 48

NA_QROWS = 8
NA_KROWS = 16
NA_TQ = NA_QROWS * GRID_W
NA_TK = NA_KROWS * GRID_W


def _params(n_axes):
    return pltpu.CompilerParams(dimension_semantics=("arbitrary",) * n_axes,
                                vmem_limit_bytes=VMEM_LIMIT)


def _proj_f32_kernel(x_ref, w_ref, o_ref, xb_ref):
    @pl.when(pl.program_id(1) == 0)
    def _():
        xb_ref[...] = x_ref[...].astype(BF16)

    o_ref[...] = jnp.dot(xb_ref[...], w_ref[...], preferred_element_type=F32).astype(o_ref.dtype)


def _proj_f32(x, w, tm, tn):
    m, k = x.shape
    n = w.shape[1]
    return pl.pallas_call(
        _proj_f32_kernel,
        grid=(m // tm, n // tn),
        in_specs=[pl.BlockSpec((tm, k), lambda i, j: (i, 0)),
                  pl.BlockSpec((k, tn), lambda i, j: (0, j))],
        out_specs=pl.BlockSpec((tm, tn), lambda i, j: (i, j)),
        out_shape=jax.ShapeDtypeStruct((m, n), BF16),
        scratch_shapes=[pltpu.VMEM((tm, k), BF16)],
        compiler_params=_params(2),
    )(x, w)


def _proj_bf16_kernel(x_ref, w_ref, o_ref):
    o_ref[...] = jnp.dot(x_ref[...], w_ref[...], preferred_element_type=F32).astype(o_ref.dtype)


def _proj_bf16(x, w, tm, tn):
    m, k = x.shape
    n = w.shape[1]
    return pl.pallas_call(
        _proj_bf16_kernel,
        grid=(m // tm, n // tn),
        in_specs=[pl.BlockSpec((tm, k), lambda i, j: (i, 0)),
                  pl.BlockSpec((k, tn), lambda i, j: (0, j))],
        out_specs=pl.BlockSpec((tm, tn), lambda i, j: (i, j)),
        out_shape=jax.ShapeDtypeStruct((m, n), BF16),
        compiler_params=_params(2),
    )(x, w)


def _proj_heads_kernel(x_ref, w_ref, cos_ref, sin_ref, *rest, rope, dils):
    o_refs, scr = rest[:len(dils)], rest[len(dils)]
    y = jnp.dot(x_ref[...], w_ref[...], preferred_element_type=F32)
    tm = y.shape[0]
    for h in range(y.shape[1] // LANES):
        yh = y[:, h * LANES:(h + 1) * LANES]
        if rope:
            yh = yh * cos_ref[...] + pltpu.roll(yh, LANES // 2, 1) * sin_ref[...]
        scr[h] = yh
        for o_ref, d in zip(o_refs, dils):
            if d == 1:
                o_ref[h] = yh.astype(o_ref.dtype)
                continue
            for r in range(d):
                o_ref[h, :, r * LANES:(r + 1) * LANES] = (
                    scr[h, pl.ds(r, tm // d, stride=d), :].astype(o_ref.dtype))


def _proj_heads(x, w, cos_t, sin_t, seq, tm, tn, rope, dils):
    m, k = x.shape
    n = w.shape[1]
    pos_blocks = seq // tm
    hb = tn // LANES
    return pl.pallas_call(
        functools.partial(_proj_heads_kernel, rope=rope, dils=dils),
        grid=(m // tm, n // tn),
        in_specs=[pl.BlockSpec((tm, k), lambda i, j: (i, 0)),
                  pl.BlockSpec((k, tn), lambda i, j: (0, j)),
                  pl.BlockSpec((tm, LANES), lambda i, j: (i % pos_blocks, 0)),
                  pl.BlockSpec((tm, LANES), lambda i, j: (i % pos_blocks, 0))],
        out_specs=[pl.BlockSpec((hb, tm // d, d * LANES), lambda i, j: (j, i, 0)) for d in dils],
        out_shape=[jax.ShapeDtypeStruct((n // LANES, m // d, d * LANES), BF16) for d in dils],
        scratch_shapes=[pltpu.VMEM((hb, tm, LANES), F32)],
        compiler_params=_params(2),
    )(x, w, cos_t, sin_t)


def _na_bias_kernel(rpb_ref, o_ref, t_ref):
    h = pl.program_id(0)
    n_dc = 2 * NA_WIN_COLS - 1
    c = lax.broadcasted_iota(jnp.int32, (GRID_W, GRID_W), 0)
    kc = lax.broadcasted_iota(jnp.int32, (GRID_W, GRID_W), 1)
    d = kc - c + (NA_WIN_COLS - 1)
    cs = jnp.clip(c - NA_WIN_COLS // 2, 0, GRID_W - NA_WIN_COLS)
    col_ok = (kc >= cs) & (kc < cs + NA_WIN_COLS)
    for dr in range(2 * NA_WIN_ROWS - 1):
        t = jnp.zeros((GRID_W, GRID_W), F32)
        for dd in range(n_dc):
            t = jnp.where(d == dd, rpb_ref[h, dr * n_dc + dd] * LOG2E, t)
        t_ref[dr] = jnp.where(col_ok, t, NEG)
    neg_blk = jnp.full((GRID_W, GRID_W), NEG, F32)
    for typ in range(3):
        for a in range(NA_QROWS):
            if typ == 0:
                lo, dr0 = max(a - 4, 0), 7 - a
            elif typ == 1:
                lo, dr0 = a, 3 - a
            else:
                lo, dr0 = 8 + min(a - 4, 0), -1 - a
            for kb in range(NA_KROWS):
                ok = lo <= kb < lo + NA_WIN_ROWS
                blk = t_ref[kb + dr0] if ok else neg_blk
                o_ref[0, typ, a * GRID_W:(a + 1) * GRID_W, kb * GRID_W:(kb + 1) * GRID_W] = blk


def _na_bias(rpb):
    rpb2 = rpb.reshape(NA_HEADS, -1)
    return pl.pallas_call(
        _na_bias_kernel,
        grid=(NA_HEADS,),
        in_specs=[pl.BlockSpec(memory_space=pltpu.SMEM)],
        out_specs=pl.BlockSpec((1, 3, NA_TQ, NA_TK), lambda h: (h, 0, 0, 0)),
        out_shape=jax.ShapeDtypeStruct((NA_HEADS, 3, NA_TQ, NA_TK), F32),
        scratch_shapes=[pltpu.VMEM((2 * NA_WIN_ROWS - 1, GRID_W, GRID_W), F32)],
        compiler_params=_params(1),
    )(rpb2)


def _na_kernel(q_ref, k0_ref, k1_ref, k2_ref, k3_ref, v0_ref, v1_ref, v2_ref, v3_ref,
               g_ref, b_ref, o_ref):
    q = q_ref[...]
    dn = (((1,), (1,)), ((), ()))
    s = jnp.concatenate(
        [lax.dot_general(q, kr[...], dn, preferred_element_type=F32)
         for kr in (k0_ref, k1_ref, k2_ref, k3_ref)], axis=1)
    s = s + b_ref[0, 0]
    m = jnp.max(s, axis=1, keepdims=True)
    p = jnp.exp2(s - m)
    l = jnp.sum(p, axis=1, keepdims=True)
    pb = p.astype(BF16)
    kq = NA_TK // 4
    o = jnp.zeros((NA_TQ, HEAD_DIM), F32)
    for i, vr in enumerate((v0_ref, v1_ref, v2_ref, v3_ref)):
        o = o + jnp.dot(pb[:, i * kq:(i + 1) * kq], vr[...], preferred_element_type=F32)
    g = g_ref[...].astype(F32)
    o_ref[...] = (o / l * (g * jax.nn.sigmoid(g))).astype(o_ref.dtype)


def _na_attention(proj, bias, batch, seq):
    rows = seq // GRID_W
    nj = rows // NA_QROWS
    kq = NA_TK // 4
    parts_per_seq = seq // kq

    def kbase(j):
        return jnp.clip(2 * j - 1, 0, rows // 4 - 4)

    def kv_spec(col0, part):
        return pl.BlockSpec((kq, LANES),
                            lambda h, j, b: (b * parts_per_seq + kbase(j) + part, col0 + h))

    def btype(j):
        return jnp.where(j == 0, 0, jnp.where(j == nj - 1, 2, 1))

    tok_spec = lambda col0: pl.BlockSpec((NA_TQ, LANES), lambda h, j, b: (b * nj + j, col0 + h))
    in_specs = ([tok_spec(C0_Q)] + [kv_spec(C0_K, i) for i in range(4)]
                + [kv_spec(C0_V, i) for i in range(4)] + [tok_spec(C0_G)]
                + [pl.BlockSpec((1, 1, NA_TQ, NA_TK), lambda h, j, b: (h, btype(j), 0, 0))])
    return pl.pallas_call(
        _na_kernel,
        grid=(NA_HEADS, nj, batch),
        in_specs=in_specs,
        out_specs=pl.BlockSpec((NA_TQ, LANES), lambda h, j, b: (b * nj + j, h)),
        out_shape=jax.ShapeDtypeStruct((batch * seq, WA), BF16),
        compiler_params=_params(3),
    )(*([proj] * 10), bias)


def _rms(lat_ref, g_ref):
    x = lat_ref[...].astype(F32)
    ms = jnp.mean(x * x, axis=1, keepdims=True)
    return (x * lax.rsqrt(ms + RMS_EPS) * g_ref[...]).astype(BF16)


def _rope64(t, cos_t, sin_t):
    lane = lax.broadcasted_iota(jnp.int32, t.shape, 1)
    half = MLA_ROPE // 2
    partner = jnp.where(lane < half, pltpu.roll(t, LANES - half, 1), pltpu.roll(t, half, 1))
    return t * cos_t + partner * sin_t


def _mla_q_up_kernel(lat_ref, g_ref, w_ref, cos_ref, sin_ref, o_ref):
    y = jnp.dot(_rms(lat_ref, g_ref), w_ref[...], preferred_element_type=F32)
    c = cos_ref[...]
    s = sin_ref[...]
    for h in range(MLA_HEADS):
        base = h * MLA_QK_PAD
        o_ref[:, base:base + LANES] = y[:, base:base + LANES].astype(o_ref.dtype)
        o_ref[:, base + LANES:base + 2 * LANES] = _rope64(
            y[:, base + LANES:base + 2 * LANES], c, s).astype(o_ref.dtype)


def _mla_kv_up_kernel(lat_ref, g_ref, w_ref, kr_ref, cos_ref, sin_ref, k_ref, vt_ref):
    y = jnp.dot(_rms(lat_ref, g_ref), w_ref[...], preferred_element_type=F32)
    k_pe = _rope64(kr_ref[...].astype(F32), cos_ref[...], sin_ref[...]).astype(k_ref.dtype)
    for h in range(MLA_HEADS):
        base = h * (MLA_NOPE + MLA_V)
        k_ref[:, h * MLA_QK_PAD:h * MLA_QK_PAD + LANES] = y[:, base:base + MLA_NOPE].astype(k_ref.dtype)
        k_ref[:, h * MLA_QK_PAD + LANES:(h + 1) * MLA_QK_PAD] = k_pe
        vt_ref[h * MLA_V:(h + 1) * MLA_V, :] = y[:, base + MLA_NOPE:base + MLA_NOPE + MLA_V].T.astype(vt_ref.dtype)


def _mla_up(proj, q_g, w_q, kv_g, w_kv, cos_t, sin_t, seq, tm):
    m = proj.shape[0]
    pos_blocks = seq // tm
    tab = pl.BlockSpec((tm, LANES), lambda i: (i % pos_blocks, 0))
    full = lambda a: pl.BlockSpec(a.shape, lambda i: (0,) * a.ndim)
    q_all = pl.pallas_call(
        _mla_q_up_kernel,
        grid=(m // tm,),
        in_specs=[pl.BlockSpec((tm, MLA_Q_RANK), lambda i: (i, C0_QLAT)), full(q_g), full(w_q), tab, tab],
        out_specs=pl.BlockSpec((tm, MLA_HEADS * MLA_QK_PAD), lambda i: (i, 0)),
        out_shape=jax.ShapeDtypeStruct((m, MLA_HEADS * MLA_QK_PAD), BF16),
        compiler_params=_params(1),
    )(proj, q_g, w_q, cos_t, sin_t)
    k_all, vt_all = pl.pallas_call(
        _mla_kv_up_kernel,
        grid=(m // tm,),
        in_specs=[pl.BlockSpec((tm, MLA_KV_RANK), lambda i: (i, C0_KVLAT)), full(kv_g), full(w_kv),
                  pl.BlockSpec((tm, LANES), lambda i: (i, C0_KROPE)), tab, tab],
        out_specs=[pl.BlockSpec((tm, MLA_HEADS * MLA_QK_PAD), lambda i: (i, 0)),
                   pl.BlockSpec((WB, tm), lambda i: (0, i))],
        out_shape=[jax.ShapeDtypeStruct((m, MLA_HEADS * MLA_QK_PAD), BF16),
                   jax.ShapeDtypeStruct((WB, m), BF16)],
        compiler_params=_params(1),
    )(proj, kv_g, w_kv, proj, cos_t, sin_t)
    return q_all, k_all, vt_all


def _mla_attn_kernel(q_ref, k_ref, vt_ref, g_ref, o_ref, s0, s1, m_s, l_s, acc_s, *, tk):
    q = q_ref[...]
    tq = q.shape[0]
    n = k_ref.shape[0] // tk
    dn = (((1,), (1,)), ((), ()))

    def scores(c):
        start = pl.multiple_of(c * tk, tk)
        return lax.dot_general(k_ref[pl.ds(start, tk), :], q, dn, preferred_element_type=F32)

    def update(c, st):
        vt = vt_ref[:, pl.ds(pl.multiple_of(c * tk, tk), tk)]
        m = m_s[...]
        m_new = jnp.maximum(m, jnp.max(st, axis=0, keepdims=True))
        alpha = jnp.exp2(m - m_new)
        p = jnp.exp2(st - m_new)
        m_s[...] = m_new
        l_s[...] = alpha * l_s[...] + jnp.sum(p, axis=0, keepdims=True)
        acc_s[...] = alpha * acc_s[...] + jnp.dot(vt, p.astype(BF16), preferred_element_type=F32)

    m_s[...] = jnp.full(m_s.shape, NEG, F32)
    l_s[...] = jnp.zeros(l_s.shape, F32)
    acc_s[...] = jnp.zeros(acc_s.shape, F32)
    s0[...] = scores(0)

    def body(i, carry):
        c = 2 * i
        s1[...] = scores(c + 1)
        update(c, s0[...])
        s0[...] = scores(jnp.minimum(c + 2, n - 1))
        update(c + 1, s1[...])
        return carry

    lax.fori_loop(0, n // 2, body, 0)
    g = g_ref[...].astype(F32)
    o_ref[...] = ((acc_s[...] * (1.0 / l_s[...])).T * (g * jax.nn.sigmoid(g))).astype(o_ref.dtype)


def _mla_attention(q_all, k_all, vt_all, proj, batch, seq, tq, tk):
    nq = seq // tq
    assert seq % (2 * tk) == 0
    return pl.pallas_call(
        functools.partial(_mla_attn_kernel, tk=tk),
        grid=(batch, MLA_HEADS, nq),
        in_specs=[pl.BlockSpec((tq, MLA_QK_PAD), lambda b, h, i: (b * nq + i, h)),
                  pl.BlockSpec((seq, MLA_QK_PAD), lambda b, h, i: (b, h)),
                  pl.BlockSpec((MLA_V, seq), lambda b, h, i: (h, b)),
                  pl.BlockSpec((tq, LANES), lambda b, h, i: (b * nq + i, C0_GB + h))],
        out_specs=pl.BlockSpec((tq, MLA_V), lambda b, h, i: (b * nq + i, h)),
        out_shape=jax.ShapeDtypeStruct((batch * seq, WB), BF16),
        scratch_shapes=[pltpu.VMEM((tk, tq), F32), pltpu.VMEM((tk, tq), F32), pltpu.VMEM((1, tq), F32),
                        pltpu.VMEM((1, tq), F32), pltpu.VMEM((MLA_V, tq), F32)],
        compiler_params=_params(3),
    )(q_all, k_all, vt_all, proj)


def _out_ln_kernel(y1_ref, y2_ref, w_ref, x_ref, g_ref, b_ref, o_ref, ob_ref):
    half = y1_ref.shape[1]
    y = jnp.dot(y1_ref[...], w_ref[:half, :], preferred_element_type=F32)
    y = y + jnp.dot(y2_ref[...], w_ref[half:, :], preferred_element_type=F32)
    z = ALPHA * x_ref[...] + y
    mu = jnp.mean(z, axis=1, keepdims=True)
    zc = z - mu
    var = jnp.mean(zc * zc, axis=1, keepdims=True)
    out = zc * lax.rsqrt(var + LN_EPS) * g_ref[...] + b_ref[...]
    o_ref[...] = out
    ob_ref[...] = out.astype(ob_ref.dtype)


def _out_ln(y1, y1_col, y2, y2_col, w, x, g, b, tm):
    m, d = x.shape
    half = w.shape[0] // 2
    full = lambda a: pl.BlockSpec(a.shape, lambda i: (0,) * a.ndim)
    return pl.pallas_call(
        _out_ln_kernel,
        grid=(m // tm,),
        in_specs=[pl.BlockSpec((tm, half), lambda i: (i, y1_col)),
                  pl.BlockSpec((tm, half), lambda i: (i, y2_col)),
                  full(w), pl.BlockSpec((tm, d), lambda i: (i, 0)), full(g), full(b)],
        out_specs=[pl.BlockSpec((tm, d), lambda i: (i, 0)), pl.BlockSpec((tm, d), lambda i: (i, 0))],
        out_shape=[jax.ShapeDtypeStruct((m, d), F32), jax.ShapeDtypeStruct((m, d), BF16)],
        compiler_params=_params(1),
    )(y1, y2, w, x, g, b)


DIL_DILS = tuple(d for _, d in DIL_PAIRS)
DIL_SIDE = DIL_PAIRS[0][0] // 2 // DIL_PAIRS[0][1]
DIL_TQ = 128
DIL_TK = DIL_TQ + 2 * DIL_SIDE
DIL_SB = DIL_TQ * DIL_DILS[-1]
assert all(w // 2 // d == DIL_SIDE for w, d in DIL_PAIRS)


def _dil_kernel(*refs, nsb):
    q_refs = refs[0:3]
    k_refs = [refs[3 + 3 * p:6 + 3 * p] for p in range(3)]
    v_refs = [refs[12 + 3 * p:15 + 3 * p] for p in range(3)]
    g_ref, o_ref, bias_s, o_s, l_s = refs[21:]
    sb = pl.program_id(2)
    dn = (((1,), (1,)), ((), ()))

    qi = lax.broadcasted_iota(jnp.int32, (DIL_TQ, DIL_TK), 0)
    kj = lax.broadcasted_iota(jnp.int32, (DIL_TQ, DIL_TK), 1)
    base = jnp.where((kj >= qi) & (kj <= qi + 2 * DIL_SIDE), 0.0, NEG).astype(F32)
    lo = jnp.where(kj < DIL_SIDE, NEG, 0.0) * (sb == 0).astype(F32)
    hi = jnp.where(kj >= DIL_TK - DIL_SIDE, NEG, 0.0) * (sb == nsb - 1).astype(F32)
    bias_s[0] = base
    bias_s[1] = base + lo
    bias_s[2] = base + hi
    bias_s[3] = base + lo + hi

    def window(trio, lanes, c, n):
        main, prev, nxt = trio
        if n == 1:
            return jnp.concatenate([prev[:, lanes], main[:, lanes], nxt[:, lanes]], axis=0), 3
        if c == 0:
            return jnp.concatenate([prev[:, lanes], main[0:DIL_TK - DIL_SIDE, lanes]], axis=0), 1
        if c == n - 1:
            return jnp.concatenate([main[n * DIL_TQ - (DIL_TK - DIL_SIDE):n * DIL_TQ, lanes],
                                    nxt[:, lanes]], axis=0), 2
        return main[c * DIL_TQ - DIL_SIDE:c * DIL_TQ + DIL_TQ + DIL_SIDE, lanes], 0

    def tile(q, kw, vw, mask_id, pair, rows):
        s = lax.dot_general(q, kw, dn, preferred_element_type=F32) + bias_s[mask_id]
        m = jnp.max(s, axis=1, keepdims=True)
        p = jnp.exp2(s - m)
        l = jnp.sum(p, axis=1, keepdims=True)
        pv = jnp.dot(p.astype(BF16), vw, preferred_element_type=F32)
        o_s[pair, rows, :] = pv * (1.0 / l)
        l_s[pair, rows, :] = jnp.broadcast_to(m + jnp.log2(l), (DIL_TQ, LANES))

    for pair, dil in enumerate(DIL_DILS):
        n = DIL_SB // dil // DIL_TQ
        for r in range(dil):
            lanes = slice(r * LANES, (r + 1) * LANES)
            for c in range(n):
                kw, mask_id = window(k_refs[pair], lanes, c, n)
                vw, _ = window(v_refs[pair], lanes, c, n)
                q = q_refs[pair][c * DIL_TQ:(c + 1) * DIL_TQ, lanes]
                start = (c * DIL_TQ) * dil + r
                rows = pl.ds(start, DIL_TQ, stride=dil) if dil > 1 else pl.ds(start, DIL_TQ)
                tile(q, kw, vw, mask_id, pair, rows)

    def combine(c, carry):
        rows = pl.ds(pl.multiple_of(c * DIL_TQ, DIL_TQ), DIL_TQ)
        lse = [l_s[pair, rows, :] for pair in range(3)]
        top = jnp.maximum(jnp.maximum(lse[0], lse[1]), lse[2])
        w = [jnp.exp2(x - top) for x in lse]
        num = w[0] * o_s[0, rows, :] + w[1] * o_s[1, rows, :] + w[2] * o_s[2, rows, :]
        g = g_ref[rows, :].astype(F32)
        o_ref[rows, :] = (num / (w[0] + w[1] + w[2]) * (g * jax.nn.sigmoid(g))).astype(o_ref.dtype)
        return carry

    lax.fori_loop(0, DIL_SB // DIL_TQ, combine, 0)


def _dil_attention(qk_views, v_views, gate, batch, seq):
    tokens = gate.shape[0]
    nsb = seq // DIL_SB
    halo = DIL_SIDE
    specs_main, specs_halo = [], []
    for dil in DIL_DILS:
        rows, width = tokens // dil, dil * LANES
        main_rows = DIL_SB // dil
        per_main = main_rows // halo
        last = rows // halo - 1

        def main_spec(h0, main_rows=main_rows, width=width):
            return pl.BlockSpec((None, main_rows, width), lambda b, h, s: (h0 + h, b * nsb + s, 0))

        def prev_spec(h0, per_main=per_main, width=width):
            return pl.BlockSpec((None, halo, width),
                                lambda b, h, s: (h0 + h, jnp.maximum((b * nsb + s) * per_main - 1, 0), 0))

        def next_spec(h0, per_main=per_main, width=width, last=last):
            return pl.BlockSpec((None, halo, width),
                                lambda b, h, s: (h0 + h, jnp.minimum((b * nsb + s + 1) * per_main, last), 0))

        specs_main.append(main_spec)
        specs_halo.append((prev_spec, next_spec))
    in_specs = [specs_main[p](0) for p in range(3)]
    operands = list(qk_views)
    for h0, views in ((DIL_HEADS, qk_views), (0, v_views)):
        for p in range(3):
            in_specs += [specs_main[p](h0), specs_halo[p][0](h0), specs_halo[p][1](h0)]
            operands += [views[p]] * 3
    in_specs.append(pl.BlockSpec((DIL_SB, LANES), lambda b, h, s: (b * nsb + s, h)))
    operands.append(gate)
    return pl.pallas_call(
        functools.partial(_dil_kernel, nsb=nsb),
        grid=(batch, DIL_HEADS, nsb),
        in_specs=in_specs,
        out_specs=pl.BlockSpec((DIL_SB, LANES), lambda b, h, s: (b * nsb + s, h)),
        out_shape=jax.ShapeDtypeStruct((tokens, WC), BF16),
        scratch_shapes=[pltpu.VMEM((4, DIL_TQ, DIL_TK), F32),
                        pltpu.VMEM((3, DIL_SB, LANES), F32),
                        pltpu.VMEM((3, DIL_SB, LANES), F32)],
        compiler_params=_params(3),
    )(*operands)


def _rope_tables(seq, half):
    inv = ROPE_THETA ** (-jnp.arange(half, dtype=F32) / half)
    ang = jnp.arange(seq, dtype=F32)[:, None] * inv[None, :]
    cos, sin = jnp.cos(ang), jnp.sin(ang)
    pad = jnp.zeros((seq, LANES - 2 * half), F32)
    return jnp.concatenate([cos, cos, pad], 1), jnp.concatenate([-sin, sin, pad], 1)


def _prep_weights(ab_w_in, ab_w_q_up, ab_w_kv_up, ab_w_out, c_w_in, c_w_out):
    qa, ka, va, ga, q_lat, kv_lat, k_rope, gb = jnp.split(
        ab_w_in, [WA, 2 * WA, 3 * WA, 4 * WA, 4 * WA + MLA_Q_RANK,
                  4 * WA + MLA_Q_RANK + MLA_KV_RANK, 4 * WA + MLA_Q_RANK + MLA_KV_RANK + MLA_ROPE], axis=1)
    w_in0 = jnp.concatenate(
        [qa * (HEAD_DIM ** -0.5 * LOG2E), ka, va, ga, q_lat, kv_lat, gb, k_rope,
         jnp.zeros((D_MODEL, LANES - MLA_ROPE), F32)], axis=1).astype(BF16)
    wq = (ab_w_q_up * ((MLA_NOPE + MLA_ROPE) ** -0.5 * LOG2E)).reshape(
        MLA_Q_RANK, MLA_HEADS, MLA_NOPE + MLA_ROPE)
    wq = jnp.pad(wq, ((0, 0), (0, 0), (0, MLA_QK_PAD - MLA_NOPE - MLA_ROPE)))
    w_q = wq.reshape(MLA_Q_RANK, MLA_HEADS * MLA_QK_PAD).astype(BF16)
    w_kv = ab_w_kv_up.astype(BF16)
    qc, kc, vc, gc = (c_w_in[:, i * WC:(i + 1) * WC] for i in range(4))
    w_qk1 = jnp.concatenate([qc * (HEAD_DIM ** -0.5 * LOG2E), kc], axis=1).astype(BF16)
    w_in1 = (w_qk1, vc.astype(BF16), gc.astype(BF16))
    return w_in0, w_q, w_kv, ab_w_out.astype(BF16), w_in1, c_w_out.astype(BF16)


def _trunk(x, w, tabs):
    batch, seq, _ = x.shape
    (w_in0, w_q, w_kv, w_out0, w_in1, w_out1, na_bias, q_g, kv_g,
     ln0_g, ln0_b, ln1_g, ln1_b) = w
    cos64, sin64, cos128, sin128 = tabs
    x2 = x.reshape(batch * seq, D_MODEL)
    proj0 = _proj_f32(x2, w_in0, tm=1024, tn=896)
    ya = _na_attention(proj0, na_bias, batch, seq)
    q_all, k_all, vt_all = _mla_up(proj0, q_g, w_q, kv_g, w_kv, cos64, sin64, seq, tm=512)
    yb = _mla_attention(q_all, k_all, vt_all, proj0, batch, seq, tq=512, tk=512)
    x1, x1b = _out_ln(ya, 0, yb, 0, w_out0, x2, ln0_g, ln0_b, tm=256)
    qk1 = _proj_heads(x1b, w_in1[0], cos128, sin128, seq, tm=1024, tn=1024, rope=True, dils=DIL_DILS)
    v1 = _proj_heads(x1b, w_in1[1], cos128, sin128, seq, tm=1024, tn=1024, rope=False, dils=DIL_DILS)
    gate1 = _proj_bf16(x1b, w_in1[2], tm=1024, tn=1024)
    yc = _dil_attention(qk1, v1, gate1, batch, seq)
    y, _ = _out_ln(yc, 0, yc, 1, w_out1, x1, ln1_g, ln1_b, tm=256)
    return y.reshape(batch, seq, D_MODEL)


def kernel(x_prompt, x_sample, ab_w_in, ab_rpb, ab_q_norm_g, ab_w_q_up, ab_kv_norm_g, ab_w_kv_up,
           ab_w_out, ab_ln_g, ab_ln_b, c_w_in, c_w_out, c_ln_g, c_ln_b):
    w_in0, w_q, w_kv, w_out0, w_in1, w_out1 = _prep_weights(
        ab_w_in, ab_w_q_up, ab_w_kv_up, ab_w_out, c_w_in, c_w_out)
    row = lambda a: a.reshape(1, -1).astype(F32)
    w = (w_in0, w_q, w_kv, w_out0, w_in1, w_out1, _na_bias(ab_rpb), row(ab_q_norm_g), row(ab_kv_norm_g),
         row(ab_ln_g), row(ab_ln_b), row(c_ln_g), row(c_ln_b))
    outs = []
    for x in (x_prompt, x_sample):
        seq = x.shape[1]
        tabs = _rope_tables(seq, MLA_ROPE // 2) + _rope_tables(seq, HEAD_DIM // 2)
        outs.append(_trunk(x, w, tabs))
    return tuple(outs)
```

```python
import functools

import jax
import jax.numpy as jnp
from jax import lax
from jax.experimental import pallas as pl
from jax.experimental.pallas import tpu as pltpu

F32 = jnp.float32
BF16 = jnp.bfloat16

D_MODEL = 2048
DEPTH = 2
GRID_W = 64
HEAD_DIM = 128
NA_HEADS = 8
NA_WIN_ROWS = 8
NA_WIN_COLS = 16
MLA_HEADS = 8
MLA_Q_RANK = 512
MLA_KV_RANK = 512
MLA_NOPE = 128
MLA_ROPE = 64
MLA_V = 128
DIL_HEADS = 16
DIL_PAIRS = ((128, 1), (512, 4), (2048, 16))
DIL_HALF = 1024
WA = NA_HEADS * HEAD_DIM
WB = MLA_HEADS * MLA_V
WC = DIL_HEADS * HEAD_DIM
ROPE_THETA = 10000.0
ALPHA = (2 * DEPTH) ** 0.25
LN_EPS = 1e-5
RMS_EPS = 1e-6
NEG = -1e30
LOG2E = 1.4426950408889634

LANES = 128
MLA_QK_PAD = 2 * LANES
MXU_WIDTH = 256
IN0_WIDTH = 4 * WA + MLA_Q_RANK + MLA_KV_RANK + WB + MLA_ROPE
IN0_PAD_WIDTH = -(-IN0_WIDTH // MXU_WIDTH) * MXU_WIDTH
IN0_TN = IN0_PAD_WIDTH // 5
VMEM_LIMIT = 56 * 1024 * 1024

C0_Q, C0_K, C0_V, C0_G = 0, 8, 16, 24
C0_QLAT, C0_KVLAT = 8, 9
C0_GB = 40
C0_KROPE = 48

NA_QROWS = 8
NA_KROWS = 16
NA_TQ = NA_QROWS * GRID_W
NA_TK = NA_KROWS * GRID_W


def _params(n_axes):
    return pltpu.CompilerParams(dimension_semantics=("arbitrary",) * n_axes,
                                vmem_limit_bytes=VMEM_LIMIT)


def _proj_f32_kernel(x_ref, w_ref, o_ref, xb_ref):
    @pl.when(pl.program_id(1) == 0)
    def _():
        xb_ref[...] = x_ref[...].astype(BF16)

    o_ref[...] = jnp.dot(xb_ref[...], w_ref[...], preferred_element_type=F32).astype(o_ref.dtype)


def _proj_f32(x, w, tm, tn):
    m, k = x.shape
    n = w.shape[1]
    return pl.pallas_call(
        _proj_f32_kernel,
        grid=(m // tm, n // tn),
        in_specs=[pl.BlockSpec((tm, k), lambda i, j: (i, 0)),
                  pl.BlockSpec((k, tn), lambda i, j: (0, j))],
        out_specs=pl.BlockSpec((tm, tn), lambda i, j: (i, j)),
        out_shape=jax.ShapeDtypeStruct((m, n), BF16),
        scratch_shapes=[pltpu.VMEM((tm, k), BF16)],
        compiler_params=_params(2),
    )(x, w)


def _proj_bf16_kernel(x_ref, w_ref, o_ref):
    o_ref[...] = jnp.dot(x_ref[...], w_ref[...], preferred_element_type=F32).astype(o_ref.dtype)


def _proj_bf16(x, w, tm, tn):
    m, k = x.shape
    n = w.shape[1]
    return pl.pallas_call(
        _proj_bf16_kernel,
        grid=(m // tm, n // tn),
        in_specs=[pl.BlockSpec((tm, k), lambda i, j: (i, 0)),
                  pl.BlockSpec((k, tn), lambda i, j: (0, j))],
        out_specs=pl.BlockSpec((tm, tn), lambda i, j: (i, j)),
        out_shape=jax.ShapeDtypeStruct((m, n), BF16),
        compiler_params=_params(2),
    )(x, w)


def _proj_heads_kernel(x_ref, w_ref, cos_ref, sin_ref, *rest, rope, dils):
    o_refs, scr = rest[:len(dils)], rest[len(dils)]
    y = jnp.dot(x_ref[...], w_ref[...], preferred_element_type=F32)
    tm = y.shape[0]
    for h in range(y.shape[1] // LANES):
        yh = y[:, h * LANES:(h + 1) * LANES]
        if rope:
            yh = yh * cos_ref[...] + pltpu.roll(yh, LANES // 2, 1) * sin_ref[...]
        scr[h] = yh
        for o_ref, d in zip(o_refs, dils):
            if d == 1:
                o_ref[h] = yh.astype(o_ref.dtype)
                continue
            for r in range(d):
                o_ref[h, :, r * LANES:(r + 1) * LANES] = (
                    scr[h, pl.ds(r, tm // d, stride=d), :].astype(o_ref.dtype))


def _proj_heads(x, w, cos_t, sin_t, seq, tm, tn, rope, dils):
    m, k = x.shape
    n = w.shape[1]
    pos_blocks = seq // tm
    hb = tn // LANES
    return pl.pallas_call(
        functools.partial(_proj_heads_kernel, rope=rope, dils=dils),
        grid=(m // tm, n // tn),
        in_specs=[pl.BlockSpec((tm, k), lambda i, j: (i, 0)),
                  pl.BlockSpec((k, tn), lambda i, j: (0, j)),
                  pl.BlockSpec((tm, LANES), lambda i, j: (i % pos_blocks, 0)),
                  pl.BlockSpec((tm, LANES), lambda i, j: (i % pos_blocks, 0))],
        out_specs=[pl.BlockSpec((hb, tm // d, d * LANES), lambda i, j: (j, i, 0)) for d in dils],
        out_shape=[jax.ShapeDtypeStruct((n // LANES, m // d, d * LANES), BF16) for d in dils],
        scratch_shapes=[pltpu.VMEM((hb, tm, LANES), F32)],
        compiler_params=_params(2),
    )(x, w, cos_t, sin_t)


def _na_bias_kernel(rpb_ref, o_ref, t_ref):
    h = pl.program_id(0)
    n_dc = 2 * NA_WIN_COLS - 1
    c = lax.broadcasted_iota(jnp.int32, (GRID_W, GRID_W), 0)
    kc = lax.broadcasted_iota(jnp.int32, (GRID_W, GRID_W), 1)
    d = kc - c + (NA_WIN_COLS - 1)
    cs = jnp.clip(c - NA_WIN_COLS // 2, 0, GRID_W - NA_WIN_COLS)
    col_ok = (kc >= cs) & (kc < cs + NA_WIN_COLS)
    for dr in range(2 * NA_WIN_ROWS - 1):
        t = jnp.zeros((GRID_W, GRID_W), F32)
        for dd in range(n_dc):
            t = jnp.where(d == dd, rpb_ref[h, dr * n_dc + dd] * LOG2E, t)
        t_ref[dr] = jnp.where(col_ok, t, NEG)
    neg_blk = jnp.full((GRID_W, GRID_W), NEG, F32)
    for typ in range(3):
        for a in range(NA_QROWS):
            if typ == 0:
                lo, dr0 = max(a - 4, 0), 7 - a
            elif typ == 1:
                lo, dr0 = a, 3 - a
            else:
                lo, dr0 = 8 + min(a - 4, 0), -1 - a
            for kb in range(NA_KROWS):
                ok = lo <= kb < lo + NA_WIN_ROWS
                blk = t_ref[kb + dr0] if ok else neg_blk
                o_ref[0, typ, a * GRID_W:(a + 1) * GRID_W, kb * GRID_W:(kb + 1) * GRID_W] = blk


def _na_bias(rpb):
    rpb2 = rpb.reshape(NA_HEADS, -1)
    return pl.pallas_call(
        _na_bias_kernel,
        grid=(NA_HEADS,),
        in_specs=[pl.BlockSpec(memory_space=pltpu.SMEM)],
        out_specs=pl.BlockSpec((1, 3, NA_TQ, NA_TK), lambda h: (h, 0, 0, 0)),
        out_shape=jax.ShapeDtypeStruct((NA_HEADS, 3, NA_TQ, NA_TK), F32),
        scratch_shapes=[pltpu.VMEM((2 * NA_WIN_ROWS - 1, GRID_W, GRID_W), F32)],
        compiler_params=_params(1),
    )(rpb2)


def _na_kernel(q_ref, k0_ref, k1_ref, k2_ref, k3_ref, v0_ref, v1_ref, v2_ref, v3_ref,
               g_ref, b_ref, o_ref):
    q = q_ref[...]
    dn = (((1,), (1,)), ((), ()))
    k = jnp.concatenate([r[...] for r in (k0_ref, k1_ref, k2_ref, k3_ref)], axis=0)
    v = jnp.concatenate([r[...] for r in (v0_ref, v1_ref, v2_ref, v3_ref)], axis=0)
    s = lax.dot_general(q, k, dn, preferred_element_type=F32) + b_ref[0, 0]
    m = jnp.max(s, axis=1, keepdims=True)
    p = jnp.exp2(s - m)
    l = jnp.sum(p, axis=1, keepdims=True)
    o = jnp.dot(p.astype(BF16), v, preferred_element_type=F32)
    g = g_ref[...].astype(F32)
    o_ref[...] = (o / l * (g * jax.nn.sigmoid(g))).astype(o_ref.dtype)


def _na_attention(proj, bias, batch, seq):
    rows = seq // GRID_W
    nj = rows // NA_QROWS
    kq = NA_TK // 4
    parts_per_seq = seq // kq

    def kbase(j):
        return jnp.clip(2 * j - 1, 0, rows // 4 - 4)

    def kv_spec(col0, part):
        return pl.BlockSpec((kq, LANES),
                            lambda h, j, b: (b * parts_per_seq + kbase(j) + part, col0 + h))

    def btype(j):
        return jnp.where(j == 0, 0, jnp.where(j == nj - 1, 2, 1))

    tok_spec = lambda col0: pl.BlockSpec((NA_TQ, LANES), lambda h, j, b: (b * nj + j, col0 + h))
    in_specs = ([tok_spec(C0_Q)] + [kv_spec(C0_K, i) for i in range(4)]
                + [kv_spec(C0_V, i) for i in range(4)] + [tok_spec(C0_G)]
                + [pl.BlockSpec((1, 1, NA_TQ, NA_TK), lambda h, j, b: (h, btype(j), 0, 0))])
    return pl.pallas_call(
        _na_kernel,
        grid=(NA_HEADS, nj, batch),
        in_specs=in_specs,
        out_specs=pl.BlockSpec((NA_TQ, LANES), lambda h, j, b: (b * nj + j, h)),
        out_shape=jax.ShapeDtypeStruct((batch * seq, WA), BF16),
        compiler_params=_params(3),
    )(*([proj] * 10), bias)


def _rms(lat_ref, g_ref):
    x = lat_ref[...].astype(F32)
    ms = jnp.mean(x * x, axis=1, keepdims=True)
    return (x * lax.rsqrt(ms + RMS_EPS) * g_ref[...]).astype(BF16)


def _rope64(t, cos_t, sin_t):
    lane = lax.broadcasted_iota(jnp.int32, t.shape, 1)
    half = MLA_ROPE // 2
    partner = jnp.where(lane < half, pltpu.roll(t, LANES - half, 1), pltpu.roll(t, half, 1))
    return t * cos_t + partner * sin_t


def _mla_q_up_kernel(lat_ref, g_ref, w_ref, cos_ref, sin_ref, o_ref):
    y = jnp.dot(_rms(lat_ref, g_ref), w_ref[...], preferred_element_type=F32)
    c = cos_ref[...]
    s = sin_ref[...]
    for h in range(MLA_HEADS):
        base = h * MLA_QK_PAD
        o_ref[:, base:base + LANES] = y[:, base:base + LANES].astype(o_ref.dtype)
        o_ref[:, base + LANES:base + 2 * LANES] = _rope64(
            y[:, base + LANES:base + 2 * LANES], c, s).astype(o_ref.dtype)


def _mla_kv_up_kernel(lat_ref, g_ref, w_ref, kr_ref, cos_ref, sin_ref, k_ref, vt_ref):
    y = jnp.dot(_rms(lat_ref, g_ref), w_ref[...], preferred_element_type=F32)
    k_pe = _rope64(kr_ref[...].astype(F32), cos_ref[...], sin_ref[...]).astype(k_ref.dtype)
    for h in range(MLA_HEADS):
        base = h * (MLA_NOPE + MLA_V)
        k_ref[:, h * MLA_QK_PAD:h * MLA_QK_PAD + LANES] = y[:, base:base + MLA_NOPE].astype(k_ref.dtype)
        k_ref[:, h * MLA_QK_PAD + LANES:(h + 1) * MLA_QK_PAD] = k_pe
        vt_ref[h * MLA_V:(h + 1) * MLA_V, :] = y[:, base + MLA_NOPE:base + MLA_NOPE + MLA_V].T.astype(vt_ref.dtype)


def _mla_up(proj, q_g, w_q, kv_g, w_kv, cos_t, sin_t, seq, tm):
    m = proj.shape[0]
    pos_blocks = seq // tm
    tab = pl.BlockSpec((tm, LANES), lambda i: (i % pos_blocks, 0))
    full = lambda a: pl.BlockSpec(a.shape, lambda i: (0,) * a.ndim)
    q_all = pl.pallas_call(
        _mla_q_up_kernel,
        grid=(m // tm,),
        in_specs=[pl.BlockSpec((tm, MLA_Q_RANK), lambda i: (i, C0_QLAT)), full(q_g), full(w_q), tab, tab],
        out_specs=pl.BlockSpec((tm, MLA_HEADS * MLA_QK_PAD), lambda i: (i, 0)),
        out_shape=jax.ShapeDtypeStruct((m, MLA_HEADS * MLA_QK_PAD), BF16),
        compiler_params=_params(1),
    )(proj, q_g, w_q, cos_t, sin_t)
    k_all, vt_all = pl.pallas_call(
        _mla_kv_up_kernel,
        grid=(m // tm,),
        in_specs=[pl.BlockSpec((tm, MLA_KV_RANK), lambda i: (i, C0_KVLAT)), full(kv_g), full(w_kv),
                  pl.BlockSpec((tm, LANES), lambda i: (i, C0_KROPE)), tab, tab],
        out_specs=[pl.BlockSpec((tm, MLA_HEADS * MLA_QK_PAD), lambda i: (i, 0)),
                   pl.BlockSpec((WB, tm), lambda i: (0, i))],
        out_shape=[jax.ShapeDtypeStruct((m, MLA_HEADS * MLA_QK_PAD), BF16),
                   jax.ShapeDtypeStruct((WB, m), BF16)],
        compiler_params=_params(1),
    )(proj, kv_g, w_kv, proj, cos_t, sin_t)
    return q_all, k_all, vt_all


MLA_UNROLL = 8


def _mla_attn_kernel(q_ref, k_ref, vt_ref, g_ref, o_ref, s0, s1, m_s, l_s, acc_s, *, tk):
    q = q_ref[...]
    tq = q.shape[0]
    n = k_ref.shape[0] // tk
    dn = (((1,), (1,)), ((), ()))

    def scores(c):
        start = pl.multiple_of(c * tk, tk)
        return lax.dot_general(k_ref[pl.ds(start, tk), :], q, dn, preferred_element_type=F32)

    def update(c, st):
        vt = vt_ref[:, pl.ds(pl.multiple_of(c * tk, tk), tk)]
        m = m_s[...]
        m_new = jnp.maximum(m, jnp.max(st, axis=0, keepdims=True))
        alpha = jnp.exp2(m - m_new)
        p = jnp.exp2(st - m_new)
        m_s[...] = m_new
        l_s[...] = alpha * l_s[...] + jnp.sum(p, axis=0, keepdims=True)
        acc_s[...] = alpha * acc_s[...] + jnp.dot(vt, p.astype(BF16), preferred_element_type=F32)

    m_s[...] = jnp.full(m_s.shape, NEG, F32)
    l_s[...] = jnp.zeros(l_s.shape, F32)
    acc_s[...] = jnp.zeros(acc_s.shape, F32)
    s0[...] = scores(0)

    bufs = (s0, s1)
    unroll = min(MLA_UNROLL, n)

    def body(i, carry):
        c0 = unroll * i
        for u in range(unroll):
            bufs[(u + 1) % 2][...] = scores(jnp.minimum(c0 + u + 1, n - 1))
            update(c0 + u, bufs[u % 2][...])
        return carry

    lax.fori_loop(0, n // unroll, body, 0)
    g = g_ref[...].astype(F32)
    o_ref[...] = ((acc_s[...] * (1.0 / l_s[...])).T * (g * jax.nn.sigmoid(g))).astype(o_ref.dtype)


def _mla_attention(q_all, k_all, vt_all, proj, batch, seq, tq, tk):
    nq = seq // tq
    n = seq // tk
    assert n % 2 == 0 and n % min(MLA_UNROLL, n) == 0
    return pl.pallas_call(
        functools.partial(_mla_attn_kernel, tk=tk),
        grid=(batch, MLA_HEADS, nq),
        in_specs=[pl.BlockSpec((tq, MLA_QK_PAD), lambda b, h, i: (b * nq + i, h)),
                  pl.BlockSpec((seq, MLA_QK_PAD), lambda b, h, i: (b, h)),
                  pl.BlockSpec((MLA_V, seq), lambda b, h, i: (h, b)),
                  pl.BlockSpec((tq, LANES), lambda b, h, i: (b * nq + i, C0_GB + h))],
        out_specs=pl.BlockSpec((tq, MLA_V), lambda b, h, i: (b * nq + i, h)),
        out_shape=jax.ShapeDtypeStruct((batch * seq, WB), BF16),
        scratch_shapes=[pltpu.VMEM((tk, tq), F32), pltpu.VMEM((tk, tq), F32), pltpu.VMEM((1, tq), F32),
                        pltpu.VMEM((1, tq), F32), pltpu.VMEM((MLA_V, tq), F32)],
        compiler_params=_params(3),
    )(q_all, k_all, vt_all, proj)


def _out_ln_kernel(y1_ref, y2_ref, w_ref, x_ref, g_ref, b_ref, o_ref, ob_ref):
    half = y1_ref.shape[1]
    y = jnp.dot(y1_ref[...], w_ref[:half, :], preferred_element_type=F32)
    y = y + jnp.dot(y2_ref[...], w_ref[half:, :], preferred_element_type=F32)
    z = ALPHA * x_ref[...] + y
    mu = jnp.mean(z, axis=1, keepdims=True)
    zc = z - mu
    var = jnp.mean(zc * zc, axis=1, keepdims=True)
    out = zc * lax.rsqrt(var + LN_EPS) * g_ref[...] + b_ref[...]
    o_ref[...] = out
    ob_ref[...] = out.astype(ob_ref.dtype)


def _out_ln(y1, y1_col, y2, y2_col, w, x, g, b, tm):
    m, d = x.shape
    half = w.shape[0] // 2
    full = lambda a: pl.BlockSpec(a.shape, lambda i: (0,) * a.ndim)
    return pl.pallas_call(
        _out_ln_kernel,
        grid=(m // tm,),
        in_specs=[pl.BlockSpec((tm, half), lambda i: (i, y1_col)),
                  pl.BlockSpec((tm, half), lambda i: (i, y2_col)),
                  full(w), pl.BlockSpec((tm, d), lambda i: (i, 0)), full(g), full(b)],
        out_specs=[pl.BlockSpec((tm, d), lambda i: (i, 0)), pl.BlockSpec((tm, d), lambda i: (i, 0))],
        out_shape=[jax.ShapeDtypeStruct((m, d), F32), jax.ShapeDtypeStruct((m, d), BF16)],
        compiler_params=_params(1),
    )(y1, y2, w, x, g, b)


DIL_DILS = tuple(d for _, d in DIL_PAIRS)
DIL_SIDE = DIL_PAIRS[0][0] // 2 // DIL_PAIRS[0][1]
DIL_TQ = 128
DIL_TK = DIL_TQ + 2 * DIL_SIDE
DIL_SB = DIL_TQ * DIL_DILS[-1]
assert all(w // 2 // d == DIL_SIDE for w, d in DIL_PAIRS)


def _dil_kernel(*refs, nsb):
    q_refs = refs[0:3]
    k_refs = [refs[3 + 3 * p:6 + 3 * p] for p in range(3)]
    v_refs = [refs[12 + 3 * p:15 + 3 * p] for p in range(3)]
    g_ref, o_ref, bias_s, o_s, l_s = refs[21:]
    sb = pl.program_id(2)
    dn = (((1,), (1,)), ((), ()))

    qi = lax.broadcasted_iota(jnp.int32, (DIL_TQ, DIL_TK), 0)
    kj = lax.broadcasted_iota(jnp.int32, (DIL_TQ, DIL_TK), 1)
    base = jnp.where((kj >= qi) & (kj <= qi + 2 * DIL_SIDE), 0.0, NEG).astype(F32)
    lo = jnp.where(kj < DIL_SIDE, NEG, 0.0) * (sb == 0).astype(F32)
    hi = jnp.where(kj >= DIL_TK - DIL_SIDE, NEG, 0.0) * (sb == nsb - 1).astype(F32)
    bias_s[0] = base
    bias_s[1] = base + lo
    bias_s[2] = base + hi
    bias_s[3] = base + lo + hi

    def window(trio, lanes, c, n):
        main, prev, nxt = trio
        if n == 1:
            return jnp.concatenate([prev[:, lanes], main[:, lanes], nxt[:, lanes]], axis=0), 3
        if c == 0:
            return jnp.concatenate([prev[:, lanes], main[0:DIL_TK - DIL_SIDE, lanes]], axis=0), 1
        if c == n - 1:
            return jnp.concatenate([main[n * DIL_TQ - (DIL_TK - DIL_SIDE):n * DIL_TQ, lanes],
                                    nxt[:, lanes]], axis=0), 2
        return main[c * DIL_TQ - DIL_SIDE:c * DIL_TQ + DIL_TQ + DIL_SIDE, lanes], 0

    def tile(q, kw, vw, mask_id, pair, rows):
        s = lax.dot_general(q, kw, dn, preferred_element_type=F32) + bias_s[mask_id]
        m = jnp.max(s, axis=1, keepdims=True)
        p = jnp.exp2(s - m)
        l = jnp.sum(p, axis=1, keepdims=True)
        pv = jnp.dot(p.astype(BF16), vw, preferred_element_type=F32)
        o_s[pair, rows, :] = pv * (1.0 / l)
        l_s[pair, rows, :] = jnp.broadcast_to(m + jnp.log2(l), (DIL_TQ, LANES))

    for pair, dil in enumerate(DIL_DILS):
        n = DIL_SB // dil // DIL_TQ
        for r in range(dil):
            lanes = slice(r * LANES, (r + 1) * LANES)
            for c in range(n):
                kw, mask_id = window(k_refs[pair], lanes, c, n)
                vw, _ = window(v_refs[pair], lanes, c, n)
                q = q_refs[pair][c * DIL_TQ:(c + 1) * DIL_TQ, lanes]
                start = (c * DIL_TQ) * dil + r
                rows = pl.ds(start, DIL_TQ, stride=dil) if dil > 1 else pl.ds(start, DIL_TQ)
                tile(q, kw, vw, mask_id, pair, rows)

    def combine(c, carry):
        rows = pl.ds(pl.multiple_of(c * DIL_TQ, DIL_TQ), DIL_TQ)
        lse = [l_s[pair, rows, :] for pair in range(3)]
        top = jnp.maximum(jnp.maximum(lse[0], lse[1]), lse[2])
        w = [jnp.exp2(x - top) for x in lse]
        num = w[0] * o_s[0, rows, :] + w[1] * o_s[1, rows, :] + w[2] * o_s[2, rows, :]
        g = g_ref[rows, :].astype(F32)
        o_ref[rows, :] = (num / (w[0] + w[1] + w[2]) * (g * jax.nn.sigmoid(g))).astype(o_ref.dtype)
        return carry

    lax.fori_loop(0, DIL_SB // DIL_TQ, combine, 0)


def _dil_attention(qk_views, v_views, gate, batch, seq):
    tokens = gate.shape[0]
    nsb = seq // DIL_SB
    halo = DIL_SIDE
    specs_main, specs_halo = [], []
    for dil in DIL_DILS:
        rows, width = tokens // dil, dil * LANES
        main_rows = DIL_SB // dil
        per_main = main_rows // halo
        last = rows // halo - 1

        def main_spec(h0, main_rows=main_rows, width=width):
            return pl.BlockSpec((None, main_rows, width), lambda b, h, s: (h0 + h, b * nsb + s, 0))

        def prev_spec(h0, per_main=per_main, width=width):
            return pl.BlockSpec((None, halo, width),
                                lambda b, h, s: (h0 + h, jnp.maximum((b * nsb + s) * per_main - 1, 0), 0))

        def next_spec(h0, per_main=per_main, width=width, last=last):
            return pl.BlockSpec((None, halo, width),
                                lambda b, h, s: (h0 + h, jnp.minimum((b * nsb + s + 1) * per_main, last), 0))

        specs_main.append(main_spec)
        specs_halo.append((prev_spec, next_spec))
    in_specs = [specs_main[p](0) for p in range(3)]
    operands = list(qk_views)
    for h0, views in ((DIL_HEADS, qk_views), (0, v_views)):
        for p in range(3):
            in_specs += [specs_main[p](h0), specs_halo[p][0](h0), specs_halo[p][1](h0)]
            operands += [views[p]] * 3
    in_specs.append(pl.BlockSpec((DIL_SB, LANES), lambda b, h, s: (b * nsb + s, h)))
    operands.append(gate)
    return pl.pallas_call(
        functools.partial(_dil_kernel, nsb=nsb),
        grid=(batch, DIL_HEADS, nsb),
        in_specs=in_specs,
        out_specs=pl.BlockSpec((DIL_SB, LANES), lambda b, h, s: (b * nsb + s, h)),
        out_shape=jax.ShapeDtypeStruct((tokens, WC), BF16),
        scratch_shapes=[pltpu.VMEM((4, DIL_TQ, DIL_TK), F32),
                        pltpu.VMEM((3, DIL_SB, LANES), F32),
                        pltpu.VMEM((3, DIL_SB, LANES), F32)],
        compiler_params=_params(3),
    )(*operands)


def _rope_tables(seq, half):
    inv = ROPE_THETA ** (-jnp.arange(half, dtype=F32) / half)
    ang = jnp.arange(seq, dtype=F32)[:, None] * inv[None, :]
    cos, sin = jnp.cos(ang), jnp.sin(ang)
    pad = jnp.zeros((seq, LANES - 2 * half), F32)
    return jnp.concatenate([cos, cos, pad], 1), jnp.concatenate([-sin, sin, pad], 1)


def _prep_weights(ab_w_in, ab_w_q_up, ab_w_kv_up, ab_w_out, c_w_in, c_w_out):
    qa, ka, va, ga, q_lat, kv_lat, k_rope, gb = jnp.split(
        ab_w_in, [WA, 2 * WA, 3 * WA, 4 * WA, 4 * WA + MLA_Q_RANK,
                  4 * WA + MLA_Q_RANK + MLA_KV_RANK, 4 * WA + MLA_Q_RANK + MLA_KV_RANK + MLA_ROPE], axis=1)
    w_in0 = jnp.concatenate(
        [qa * (HEAD_DIM ** -0.5 * LOG2E), ka, va, ga, q_lat, kv_lat, gb, k_rope,
         jnp.zeros((D_MODEL, IN0_PAD_WIDTH - IN0_WIDTH), F32)], axis=1).astype(BF16)
    wq = (ab_w_q_up * ((MLA_NOPE + MLA_ROPE) ** -0.5 * LOG2E)).reshape(
        MLA_Q_RANK, MLA_HEADS, MLA_NOPE + MLA_ROPE)
    wq = jnp.pad(wq, ((0, 0), (0, 0), (0, MLA_QK_PAD - MLA_NOPE - MLA_ROPE)))
    w_q = wq.reshape(MLA_Q_RANK, MLA_HEADS * MLA_QK_PAD).astype(BF16)
    w_kv = ab_w_kv_up.astype(BF16)
    qc, kc, vc, gc = (c_w_in[:, i * WC:(i + 1) * WC] for i in range(4))
    w_qk1 = jnp.concatenate([qc * (HEAD_DIM ** -0.5 * LOG2E), kc], axis=1).astype(BF16)
    w_in1 = (w_qk1, vc.astype(BF16), gc.astype(BF16))
    return w_in0, w_q, w_kv, ab_w_out.astype(BF16), w_in1, c_w_out.astype(BF16)


def _trunk(x, w, tabs):
    batch, seq, _ = x.shape
    (w_in0, w_q, w_kv, w_out0, w_in1, w_out1, na_bias, q_g, kv_g,
     ln0_g, ln0_b, ln1_g, ln1_b) = w
    cos64, sin64, cos128, sin128 = tabs
    x2 = x.reshape(batch * seq, D_MODEL)
    proj0 = _proj_f32(x2, w_in0, tm=1024, tn=IN0_TN)
    ya = _na_attention(proj0, na_bias, batch, seq)
    q_all, k_all, vt_all = _mla_up(proj0, q_g, w_q, kv_g, w_kv, cos64, sin64, seq, tm=512)
    yb = _mla_attention(q_all, k_all, vt_all, proj0, batch, seq, tq=512, tk=512)
    x1, x1b = _out_ln(ya, 0, yb, 0, w_out0, x2, ln0_g, ln0_b, tm=512)
    qk1 = _proj_heads(x1b, w_in1[0], cos128, sin128, seq, tm=1024, tn=1024, rope=True, dils=DIL_DILS)
    v1 = _proj_heads(x1b, w_in1[1], cos128, sin128, seq, tm=1024, tn=1024, rope=False, dils=DIL_DILS)
    gate1 = _proj_bf16(x1b, w_in1[2], tm=1024, tn=1024)
    yc = _dil_attention(qk1, v1, gate1, batch, seq)
    y, _ = _out_ln(yc, 0, yc, 1, w_out1, x1, ln1_g, ln1_b, tm=512)
    return y.reshape(batch, seq, D_MODEL)


def kernel(x_prompt, x_sample, ab_w_in, ab_rpb, ab_q_norm_g, ab_w_q_up, ab_kv_norm_g, ab_w_kv_up,
           ab_w_out, ab_ln_g, ab_ln_b, c_w_in, c_w_out, c_ln_g, c_ln_b):
    w_in0, w_q, w_kv, w_out0, w_in1, w_out1 = _prep_weights(
        ab_w_in, ab_w_q_up, ab_w_kv_up, ab_w_out, c_w_in, c_w_out)
    row = lambda a: a.reshape(1, -1).astype(F32)
    w = (w_in0, w_q, w_kv, w_out0, w_in1, w_out1, _na_bias(ab_rpb), row(ab_q_norm_g), row(ab_kv_norm_g),
         row(ab_ln_g), row(ab_ln_b), row(c_ln_g), row(c_ln_b))
    max_seq = max(x_prompt.shape[1], x_sample.shape[1])
    tabs = _rope_tables(max_seq, MLA_ROPE // 2) + _rope_tables(max_seq, HEAD_DIM // 2)
    return tuple(_trunk(x, w, tabs) for x in (x_prompt, x_sample))
```

```python
import functools
import math

import jax
import jax.numpy as jnp
from jax import lax
from jax.experimental import pallas as pl
from jax.experimental.pallas import tpu as pltpu

F32 = jnp.float32
BF16 = jnp.bfloat16

D_MODEL = 2048
DEPTH = 2
GRID_W = 64
HEAD_DIM = 128
NA_HEADS = 8
NA_WIN_ROWS = 8
NA_WIN_COLS = 16
MLA_HEADS = 8
MLA_Q_RANK = 512
MLA_KV_RANK = 512
MLA_NOPE = 128
MLA_ROPE = 64
MLA_V = 128
DIL_HEADS = 16
DIL_PAIRS = ((128, 1), (512, 4), (2048, 16))
DIL_HALF = 1024
WA = NA_HEADS * HEAD_DIM
WB = MLA_HEADS * MLA_V
WC = DIL_HEADS * HEAD_DIM
ROPE_THETA = 10000.0
ALPHA = (2 * DEPTH) ** 0.25
LN_EPS = 1e-5
RMS_EPS = 1e-6
NEG = -1e30
LOG2E = 1.4426950408889634

LANES = 128
MLA_QK_PAD = 2 * LANES
MXU_WIDTH = 256
IN0_WIDTH = 3 * WA + MLA_Q_RANK + MLA_KV_RANK + WB + MLA_ROPE
IN0_PAD_WIDTH = -(-IN0_WIDTH // MXU_WIDTH) * MXU_WIDTH
IN0_TN = IN0_PAD_WIDTH // 7
VMEM_LIMIT = 56 * 1024 * 1024

C0_Q, C0_K, C0_G = 0, 8, 16
C0_QLAT, C0_KVLAT = 6, 7
C0_GB = 32
C0_KROPE = 40

NA_QROWS = 8
NA_KROWS = 16
NA_TQ = NA_QROWS * GRID_W
NA_TK = NA_KROWS * GRID_W


def _params(n_axes):
    return pltpu.CompilerParams(dimension_semantics=("arbitrary",) * n_axes,
                                vmem_limit_bytes=VMEM_LIMIT)


def _proj_f32_kernel(x_ref, w_ref, o_ref, xb_ref):
    @pl.when(pl.program_id(1) == 0)
    def _():
        xb_ref[...] = x_ref[...].astype(BF16)

    o_ref[...] = jnp.dot(xb_ref[...], w_ref[...], preferred_element_type=F32).astype(o_ref.dtype)


def _proj_f32(x, w, tm, tn):
    m, k = x.shape
    n = w.shape[1]
    return pl.pallas_call(
        _proj_f32_kernel,
        grid=(m // tm, n // tn),
        in_specs=[pl.BlockSpec((tm, k), lambda i, j: (i, 0)),
                  pl.BlockSpec((k, tn), lambda i, j: (0, j))],
        out_specs=pl.BlockSpec((tm, tn), lambda i, j: (i, j)),
        out_shape=jax.ShapeDtypeStruct((m, n), BF16),
        scratch_shapes=[pltpu.VMEM((tm, k), BF16)],
        compiler_params=_params(2),
    )(x, w)


def _proj_t_kernel(w_ref, x_ref, o_ref):
    dn = (((1,), (1,)), ((), ()))
    o_ref[...] = lax.dot_general(w_ref[...], x_ref[...].astype(BF16), dn,
                                 preferred_element_type=F32).astype(o_ref.dtype)


def _proj_t(w_t, x, tm):
    n, k = w_t.shape
    m = x.shape[0]
    return pl.pallas_call(
        _proj_t_kernel,
        grid=(m // tm,),
        in_specs=[pl.BlockSpec((n, k), lambda i: (0, 0)),
                  pl.BlockSpec((tm, k), lambda i: (i, 0))],
        out_specs=pl.BlockSpec((n, tm), lambda i: (0, i)),
        out_shape=jax.ShapeDtypeStruct((n, m), BF16),
        compiler_params=_params(1),
    )(w_t, x)


def _proj_bf16_kernel(x_ref, w_ref, o_ref):
    o_ref[...] = jnp.dot(x_ref[...], w_ref[...], preferred_element_type=F32).astype(o_ref.dtype)


def _proj_bf16(x, w, tm, tn):
    m, k = x.shape
    n = w.shape[1]
    return pl.pallas_call(
        _proj_bf16_kernel,
        grid=(m // tm, n // tn),
        in_specs=[pl.BlockSpec((tm, k), lambda i, j: (i, 0)),
                  pl.BlockSpec((k, tn), lambda i, j: (0, j))],
        out_specs=pl.BlockSpec((tm, tn), lambda i, j: (i, j)),
        out_shape=jax.ShapeDtypeStruct((m, n), BF16),
        compiler_params=_params(2),
    )(x, w)


def _proj_heads_kernel(x_ref, w_ref, cos_ref, sin_ref, *rest, rope, dils):
    o_refs, scrs = rest[:len(dils)], rest[len(dils):]
    y = jnp.dot(x_ref[...], w_ref[...], preferred_element_type=F32)
    tm = y.shape[0]
    for h in range(y.shape[1] // LANES):
        yh = y[:, h * LANES:(h + 1) * LANES]
        if rope:
            yh = yh * cos_ref[...] + pltpu.roll(yh, LANES // 2, 1) * sin_ref[...]
        o_refs[0][h] = yh.astype(o_refs[0].dtype)
        scrs[0][h, 0] = yh
        for lvl in range(1, len(dils)):
            d_prev, d = dils[lvl - 1], dils[lvl]
            q = d // d_prev
            for r_prev in range(d_prev):
                for a in range(q):
                    part = scrs[lvl - 1][h, r_prev, pl.ds(a, tm // d, stride=q), :]
                    r = d_prev * a + r_prev
                    o_refs[lvl][h, :, r * LANES:(r + 1) * LANES] = part.astype(o_refs[lvl].dtype)
                    if lvl + 1 < len(dils):
                        scrs[lvl][h, r] = part


def _proj_heads(x, w, cos_t, sin_t, seq, tm, tn, rope, dils):
    m, k = x.shape
    n = w.shape[1]
    pos_blocks = seq // tm
    hb = tn // LANES
    return pl.pallas_call(
        functools.partial(_proj_heads_kernel, rope=rope, dils=dils),
        grid=(m // tm, n // tn),
        in_specs=[pl.BlockSpec((tm, k), lambda i, j: (i, 0)),
                  pl.BlockSpec((k, tn), lambda i, j: (0, j)),
                  pl.BlockSpec((tm, LANES), lambda i, j: (i % pos_blocks, 0)),
                  pl.BlockSpec((tm, LANES), lambda i, j: (i % pos_blocks, 0))],
        out_specs=[pl.BlockSpec((hb, tm // d, d * LANES), lambda i, j: (j, i, 0)) for d in dils],
        out_shape=[jax.ShapeDtypeStruct((n // LANES, m // d, d * LANES), BF16) for d in dils],
        scratch_shapes=[pltpu.VMEM((hb, d, tm // d, LANES), F32) for d in dils[:-1]],
        compiler_params=_params(2),
    )(x, w, cos_t, sin_t)


def _na_bias_kernel(rpb_ref, o_ref, t_ref):
    h = pl.program_id(0)
    n_dc = 2 * NA_WIN_COLS - 1
    kc = lax.broadcasted_iota(jnp.int32, (GRID_W, GRID_W), 0)
    c = lax.broadcasted_iota(jnp.int32, (GRID_W, GRID_W), 1)
    d = kc - c + (NA_WIN_COLS - 1)
    cs = jnp.clip(c - NA_WIN_COLS // 2, 0, GRID_W - NA_WIN_COLS)
    col_ok = (kc >= cs) & (kc < cs + NA_WIN_COLS)
    for dr in range(2 * NA_WIN_ROWS - 1):
        t = jnp.zeros((GRID_W, GRID_W), F32)
        for dd in range(n_dc):
            t = jnp.where(d == dd, rpb_ref[h, dr * n_dc + dd] * LOG2E, t)
        t_ref[dr] = jnp.where(col_ok, t, NEG)
    neg_blk = jnp.full((GRID_W, GRID_W), NEG, F32)
    for typ in range(3):
        for a in range(NA_QROWS):
            if typ == 0:
                lo, dr0 = max(a - 4, 0), 7 - a
            elif typ == 1:
                lo, dr0 = a, 3 - a
            else:
                lo, dr0 = 8 + min(a - 4, 0), -1 - a
            for kb in range(NA_KROWS):
                ok = lo <= kb < lo + NA_WIN_ROWS
                blk = t_ref[kb + dr0] if ok else neg_blk
                o_ref[0, typ, kb * GRID_W:(kb + 1) * GRID_W, a * GRID_W:(a + 1) * GRID_W] = blk


def _na_bias(rpb):
    rpb2 = rpb.reshape(NA_HEADS, -1)
    return pl.pallas_call(
        _na_bias_kernel,
        grid=(NA_HEADS,),
        in_specs=[pl.BlockSpec(memory_space=pltpu.SMEM)],
        out_specs=pl.BlockSpec((1, 3, NA_TK, NA_TQ), lambda h: (h, 0, 0, 0)),
        out_shape=jax.ShapeDtypeStruct((NA_HEADS, 3, NA_TK, NA_TQ), F32),
        scratch_shapes=[pltpu.VMEM((2 * NA_WIN_ROWS - 1, GRID_W, GRID_W), F32)],
        compiler_params=_params(1),
    )(rpb2)


def _na_kernel(q_ref, k0_ref, k1_ref, k2_ref, k3_ref, vt0_ref, vt1_ref, vt2_ref, vt3_ref,
               g_ref, b_ref, o_ref):
    q = q_ref[...]
    dn = (((1,), (1,)), ((), ()))
    k = jnp.concatenate([r[...] for r in (k0_ref, k1_ref, k2_ref, k3_ref)], axis=0)
    vt = jnp.concatenate([r[...] for r in (vt0_ref, vt1_ref, vt2_ref, vt3_ref)], axis=1)
    st = lax.dot_general(k, q, dn, preferred_element_type=F32) + b_ref[0, 0]
    m = jnp.max(st, axis=0, keepdims=True)
    p = jnp.exp2(st - m)
    l = jnp.sum(p, axis=0, keepdims=True)
    ot = jnp.dot(vt, p.astype(BF16), preferred_element_type=F32)
    g = g_ref[...].astype(F32)
    o_ref[...] = ((ot * (1.0 / l)).T * (g * jax.nn.sigmoid(g))).astype(o_ref.dtype)


def _na_attention(proj, vt_na, bias, batch, seq):
    rows = seq // GRID_W
    nj = rows // NA_QROWS
    kq = NA_TK // 4
    parts_per_seq = seq // kq

    def kbase(j):
        return jnp.clip(2 * j - 1, 0, rows // 4 - 4)

    def kv_spec(col0, part):
        return pl.BlockSpec((kq, LANES),
                            lambda h, j, b: (b * parts_per_seq + kbase(j) + part, col0 + h))

    def btype(j):
        return jnp.where(j == 0, 0, jnp.where(j == nj - 1, 2, 1))

    def vt_spec(part):
        return pl.BlockSpec((HEAD_DIM, kq), lambda h, j, b: (h, b * parts_per_seq + kbase(j) + part))

    tok_spec = lambda col0: pl.BlockSpec((NA_TQ, LANES), lambda h, j, b: (b * nj + j, col0 + h))
    in_specs = ([tok_spec(C0_Q)] + [kv_spec(C0_K, i) for i in range(4)]
                + [vt_spec(i) for i in range(4)] + [tok_spec(C0_G)]
                + [pl.BlockSpec((1, 1, NA_TK, NA_TQ), lambda h, j, b: (h, btype(j), 0, 0))])
    return pl.pallas_call(
        _na_kernel,
        grid=(NA_HEADS, nj, batch),
        in_specs=in_specs,
        out_specs=pl.BlockSpec((NA_TQ, LANES), lambda h, j, b: (b * nj + j, h)),
        out_shape=jax.ShapeDtypeStruct((batch * seq, WA), BF16),
        compiler_params=_params(3),
    )(*([proj] * 5), *([vt_na] * 4), proj, bias)


def _rms(lat_ref, g_ref):
    x = lat_ref[...].astype(F32)
    ms = jnp.mean(x * x, axis=1, keepdims=True)
    return (x * lax.rsqrt(ms + RMS_EPS) * g_ref[...]).astype(BF16)


def _rope64(t, cos_t, sin_t):
    lane = lax.broadcasted_iota(jnp.int32, t.shape, 1)
    half = MLA_ROPE // 2
    partner = jnp.where(lane < half, pltpu.roll(t, LANES - half, 1), pltpu.roll(t, half, 1))
    return t * cos_t + partner * sin_t


def _mla_q_up_kernel(lat_ref, g_ref, w_ref, cos_ref, sin_ref, o_ref):
    y = jnp.dot(_rms(lat_ref, g_ref), w_ref[...], preferred_element_type=F32)
    c = cos_ref[...]
    s = sin_ref[...]
    for h in range(MLA_HEADS):
        base = h * MLA_QK_PAD
        o_ref[:, base:base + LANES] = y[:, base:base + LANES].astype(o_ref.dtype)
        o_ref[:, base + LANES:base + 2 * LANES] = _rope64(
            y[:, base + LANES:base + 2 * LANES], c, s).astype(o_ref.dtype)


def _mla_kv_up_kernel(lat_ref, g_ref, w_ref, kr_ref, cos_ref, sin_ref, k_ref, vt_ref):
    y = jnp.dot(_rms(lat_ref, g_ref), w_ref[...], preferred_element_type=F32)
    k_pe = _rope64(kr_ref[...].astype(F32), cos_ref[...], sin_ref[...]).astype(k_ref.dtype)
    for h in range(MLA_HEADS):
        base = h * (MLA_NOPE + MLA_V)
        k_ref[:, h * MLA_QK_PAD:h * MLA_QK_PAD + LANES] = y[:, base:base + MLA_NOPE].astype(k_ref.dtype)
        k_ref[:, h * MLA_QK_PAD + LANES:(h + 1) * MLA_QK_PAD] = k_pe
        vt_ref[h * MLA_V:(h + 1) * MLA_V, :] = y[:, base + MLA_NOPE:base + MLA_NOPE + MLA_V].T.astype(vt_ref.dtype)


def _mla_up(proj, q_g, w_q, kv_g, w_kv, cos_t, sin_t, seq, tm):
    m = proj.shape[0]
    pos_blocks = seq // tm
    tab = pl.BlockSpec((tm, LANES), lambda i: (i % pos_blocks, 0))
    full = lambda a: pl.BlockSpec(a.shape, lambda i: (0,) * a.ndim)
    q_all = pl.pallas_call(
        _mla_q_up_kernel,
        grid=(m // tm,),
        in_specs=[pl.BlockSpec((tm, MLA_Q_RANK), lambda i: (i, C0_QLAT)), full(q_g), full(w_q), tab, tab],
        out_specs=pl.BlockSpec((tm, MLA_HEADS * MLA_QK_PAD), lambda i: (i, 0)),
        out_shape=jax.ShapeDtypeStruct((m, MLA_HEADS * MLA_QK_PAD), BF16),
        compiler_params=_params(1),
    )(proj, q_g, w_q, cos_t, sin_t)
    k_all, vt_all = pl.pallas_call(
        _mla_kv_up_kernel,
        grid=(m // tm,),
        in_specs=[pl.BlockSpec((tm, MLA_KV_RANK), lambda i: (i, C0_KVLAT)), full(kv_g), full(w_kv),
                  pl.BlockSpec((tm, LANES), lambda i: (i, C0_KROPE)), tab, tab],
        out_specs=[pl.BlockSpec((tm, MLA_HEADS * MLA_QK_PAD), lambda i: (i, 0)),
                   pl.BlockSpec((WB, tm), lambda i: (0, i))],
        out_shape=[jax.ShapeDtypeStruct((m, MLA_HEADS * MLA_QK_PAD), BF16),
                   jax.ShapeDtypeStruct((WB, m), BF16)],
        compiler_params=_params(1),
    )(proj, kv_g, w_kv, proj, cos_t, sin_t)
    return q_all, k_all, vt_all


MLA_UNROLL = 8


def _mla_attn_kernel(q_ref, k_ref, vt_ref, g_ref, o_ref, s0, s1, m_s, l_s, acc_s, *, tk):
    q = q_ref[...]
    tq = q.shape[0]
    n = k_ref.shape[0] // tk
    dn = (((1,), (1,)), ((), ()))

    def scores(c):
        start = pl.multiple_of(c * tk, tk)
        return lax.dot_general(k_ref[pl.ds(start, tk), :], q, dn, preferred_element_type=F32)

    def update(c, st):
        vt = vt_ref[:, pl.ds(pl.multiple_of(c * tk, tk), tk)]
        m = m_s[...]
        m_new = jnp.maximum(m, jnp.max(st, axis=0, keepdims=True))
        alpha = jnp.exp2(m - m_new)
        p = jnp.exp2(st - m_new)
        m_s[...] = m_new
        l_s[...] = alpha * l_s[...] + jnp.sum(p, axis=0, keepdims=True)
        acc_s[...] = alpha * acc_s[...] + jnp.dot(vt, p.astype(BF16), preferred_element_type=F32)

    m_s[...] = jnp.full(m_s.shape, NEG, F32)
    l_s[...] = jnp.zeros(l_s.shape, F32)
    acc_s[...] = jnp.zeros(acc_s.shape, F32)
    s0[...] = scores(0)

    bufs = (s0, s1)
    unroll = math.gcd(MLA_UNROLL, n)

    def body(i, carry):
        c0 = unroll * i
        for u in range(unroll):
            bufs[(u + 1) % 2][...] = scores(jnp.minimum(c0 + u + 1, n - 1))
            update(c0 + u, bufs[u % 2][...])
        return carry

    lax.fori_loop(0, n // unroll, body, 0)
    g = g_ref[...].astype(F32)
    o_ref[...] = ((acc_s[...] * (1.0 / l_s[...])).T * (g * jax.nn.sigmoid(g))).astype(o_ref.dtype)


def _mla_attention(q_all, k_all, vt_all, proj, batch, seq, tq, tk):
    nq = seq // tq
    n = seq // tk
    assert n % 2 == 0 and MLA_UNROLL % 2 == 0
    return pl.pallas_call(
        functools.partial(_mla_attn_kernel, tk=tk),
        grid=(batch, MLA_HEADS, nq),
        in_specs=[pl.BlockSpec((tq, MLA_QK_PAD), lambda b, h, i: (b * nq + i, h)),
                  pl.BlockSpec((seq, MLA_QK_PAD), lambda b, h, i: (b, h)),
                  pl.BlockSpec((MLA_V, seq), lambda b, h, i: (h, b)),
                  pl.BlockSpec((tq, LANES), lambda b, h, i: (b * nq + i, C0_GB + h))],
        out_specs=pl.BlockSpec((tq, MLA_V), lambda b, h, i: (b * nq + i, h)),
        out_shape=jax.ShapeDtypeStruct((batch * seq, WB), BF16),
        scratch_shapes=[pltpu.VMEM((tk, tq), F32), pltpu.VMEM((tk, tq), F32), pltpu.VMEM((1, tq), F32),
                        pltpu.VMEM((1, tq), F32), pltpu.VMEM((MLA_V, tq), F32)],
        compiler_params=_params(3),
    )(q_all, k_all, vt_all, proj)


def _out_ln_kernel(y1_ref, y2_ref, w_ref, x_ref, g_ref, b_ref, o_ref, ob_ref):
    half = y1_ref.shape[1]
    y = jnp.dot(y1_ref[...], w_ref[:half, :], preferred_element_type=F32)
    y = y + jnp.dot(y2_ref[...], w_ref[half:, :], preferred_element_type=F32)
    z = ALPHA * x_ref[...] + y
    mu = jnp.mean(z, axis=1, keepdims=True)
    zc = z - mu
    var = jnp.mean(zc * zc, axis=1, keepdims=True)
    out = zc * lax.rsqrt(var + LN_EPS) * g_ref[...] + b_ref[...]
    o_ref[...] = out
    ob_ref[...] = out.astype(ob_ref.dtype)


def _out_ln(y1, y1_col, y2, y2_col, w, x, g, b, tm):
    m, d = x.shape
    half = w.shape[0] // 2
    full = lambda a: pl.BlockSpec(a.shape, lambda i: (0,) * a.ndim)
    return pl.pallas_call(
        _out_ln_kernel,
        grid=(m // tm,),
        in_specs=[pl.BlockSpec((tm, half), lambda i: (i, y1_col)),
                  pl.BlockSpec((tm, half), lambda i: (i, y2_col)),
                  full(w), pl.BlockSpec((tm, d), lambda i: (i, 0)), full(g), full(b)],
        out_specs=[pl.BlockSpec((tm, d), lambda i: (i, 0)), pl.BlockSpec((tm, d), lambda i: (i, 0))],
        out_shape=[jax.ShapeDtypeStruct((m, d), F32), jax.ShapeDtypeStruct((m, d), BF16)],
        compiler_params=_params(1),
    )(y1, y2, w, x, g, b)


DIL_DILS = tuple(d for _, d in DIL_PAIRS)
DIL_SIDE = DIL_PAIRS[0][0] // 2 // DIL_PAIRS[0][1]
DIL_TQ = 128
DIL_TK = DIL_TQ + 2 * DIL_SIDE
DIL_SB = DIL_TQ * DIL_DILS[-1]
assert all(w // 2 // d == DIL_SIDE for w, d in DIL_PAIRS)


def _dil_kernel(*refs, nsb):
    q_refs = refs[0:3]
    k_refs = [refs[3 + 3 * p:6 + 3 * p] for p in range(3)]
    v_refs = [refs[12 + 3 * p:15 + 3 * p] for p in range(3)]
    g_ref, o_ref, bias_s, o_s, l_s = refs[21:]
    sb = pl.program_id(2)
    dn = (((1,), (1,)), ((), ()))

    qi = lax.broadcasted_iota(jnp.int32, (DIL_TQ, DIL_TK), 0)
    kj = lax.broadcasted_iota(jnp.int32, (DIL_TQ, DIL_TK), 1)
    base = jnp.where((kj >= qi) & (kj <= qi + 2 * DIL_SIDE), 0.0, NEG).astype(F32)
    lo = jnp.where(kj < DIL_SIDE, NEG, 0.0) * (sb == 0).astype(F32)
    hi = jnp.where(kj >= DIL_TK - DIL_SIDE, NEG, 0.0) * (sb == nsb - 1).astype(F32)
    bias_s[0] = base
    bias_s[1] = base + lo
    bias_s[2] = base + hi
    bias_s[3] = base + lo + hi

    def window(trio, lanes, c, n):
        main, prev, nxt = trio
        if n == 1:
            return jnp.concatenate([prev[:, lanes], main[:, lanes], nxt[:, lanes]], axis=0), 3
        if c == 0:
            return jnp.concatenate([prev[:, lanes], main[0:DIL_TK - DIL_SIDE, lanes]], axis=0), 1
        if c == n - 1:
            return jnp.concatenate([main[n * DIL_TQ - (DIL_TK - DIL_SIDE):n * DIL_TQ, lanes],
                                    nxt[:, lanes]], axis=0), 2
        return main[c * DIL_TQ - DIL_SIDE:c * DIL_TQ + DIL_TQ + DIL_SIDE, lanes], 0

    def tile(q, kw, vw, mask_id, pair, rows):
        s = lax.dot_general(q, kw, dn, preferred_element_type=F32) + bias_s[mask_id]
        m = jnp.max(s, axis=1, keepdims=True)
        p = jnp.exp2(s - m)
        l = jnp.sum(p, axis=1, keepdims=True)
        pv = jnp.dot(p.astype(BF16), vw, preferred_element_type=F32)
        o_s[pair, rows, :] = pv * (1.0 / l)
        l_s[pair, rows, :] = jnp.broadcast_to(m + jnp.log2(l), (DIL_TQ, LANES))

    for pair, dil in enumerate(DIL_DILS):
        n = DIL_SB // dil // DIL_TQ
        for r in range(dil):
            lanes = slice(r * LANES, (r + 1) * LANES)
            for c in range(n):
                kw, mask_id = window(k_refs[pair], lanes, c, n)
                vw, _ = window(v_refs[pair], lanes, c, n)
                q = q_refs[pair][c * DIL_TQ:(c + 1) * DIL_TQ, lanes]
                start = (c * DIL_TQ) * dil + r
                rows = pl.ds(start, DIL_TQ, stride=dil) if dil > 1 else pl.ds(start, DIL_TQ)
                tile(q, kw, vw, mask_id, pair, rows)

    def combine(c, carry):
        rows = pl.ds(pl.multiple_of(c * DIL_TQ, DIL_TQ), DIL_TQ)
        lse = [l_s[pair, rows, :] for pair in range(3)]
        top = jnp.maximum(jnp.maximum(lse[0], lse[1]), lse[2])
        w = [jnp.exp2(x - top) for x in lse]
        num = w[0] * o_s[0, rows, :] + w[1] * o_s[1, rows, :] + w[2] * o_s[2, rows, :]
        g = g_ref[rows, :].astype(F32)
        o_ref[rows, :] = (num / (w[0] + w[1] + w[2]) * (g * jax.nn.sigmoid(g))).astype(o_ref.dtype)
        return carry

    lax.fori_loop(0, DIL_SB // DIL_TQ, combine, 0)


def _dil_attention(qk_views, v_views, gate, batch, seq):
    tokens = gate.shape[0]
    nsb = seq // DIL_SB
    halo = DIL_SIDE
    specs_main, specs_halo = [], []
    for dil in DIL_DILS:
        rows, width = tokens // dil, dil * LANES
        main_rows = DIL_SB // dil
        per_main = main_rows // halo
        last = rows // halo - 1

        def main_spec(h0, main_rows=main_rows, width=width):
            return pl.BlockSpec((None, main_rows, width), lambda b, h, s: (h0 + h, b * nsb + s, 0))

        def prev_spec(h0, per_main=per_main, width=width):
            return pl.BlockSpec((None, halo, width),
                                lambda b, h, s: (h0 + h, jnp.maximum((b * nsb + s) * per_main - 1, 0), 0))

        def next_spec(h0, per_main=per_main, width=width, last=last):
            return pl.BlockSpec((None, halo, width),
                                lambda b, h, s: (h0 + h, jnp.minimum((b * nsb + s + 1) * per_main, last), 0))

        specs_main.append(main_spec)
        specs_halo.append((prev_spec, next_spec))
    in_specs = [specs_main[p](0) for p in range(3)]
    operands = list(qk_views)
    for h0, views in ((DIL_HEADS, qk_views), (0, v_views)):
        for p in range(3):
            in_specs += [specs_main[p](h0), specs_halo[p][0](h0), specs_halo[p][1](h0)]
            operands += [views[p]] * 3
    in_specs.append(pl.BlockSpec((DIL_SB, LANES), lambda b, h, s: (b * nsb + s, h)))
    operands.append(gate)
    return pl.pallas_call(
        functools.partial(_dil_kernel, nsb=nsb),
        grid=(batch, DIL_HEADS, nsb),
        in_specs=in_specs,
        out_specs=pl.BlockSpec((DIL_SB, LANES), lambda b, h, s: (b * nsb + s, h)),
        out_shape=jax.ShapeDtypeStruct((tokens, WC), BF16),
        scratch_shapes=[pltpu.VMEM((4, DIL_TQ, DIL_TK), F32),
                        pltpu.VMEM((3, DIL_SB, LANES), F32),
                        pltpu.VMEM((3, DIL_SB, LANES), F32)],
        compiler_params=_params(3),
    )(*operands)


def _rope_tables(seq, half):
    inv = ROPE_THETA ** (-jnp.arange(half, dtype=F32) / half)
    ang = jnp.arange(seq, dtype=F32)[:, None] * inv[None, :]
    cos, sin = jnp.cos(ang), jnp.sin(ang)
    pad = jnp.zeros((seq, LANES - 2 * half), F32)
    return jnp.concatenate([cos, cos, pad], 1), jnp.concatenate([-sin, sin, pad], 1)


def _prep_weights(ab_w_in, ab_w_q_up, ab_w_kv_up, ab_w_out, c_w_in, c_w_out):
    qa, ka, va, ga, q_lat, kv_lat, k_rope, gb = jnp.split(
        ab_w_in, [WA, 2 * WA, 3 * WA, 4 * WA, 4 * WA + MLA_Q_RANK,
                  4 * WA + MLA_Q_RANK + MLA_KV_RANK, 4 * WA + MLA_Q_RANK + MLA_KV_RANK + MLA_ROPE], axis=1)
    w_in0 = (jnp.concatenate(
        [qa * (HEAD_DIM ** -0.5 * LOG2E), ka, ga, q_lat, kv_lat, gb, k_rope,
         jnp.zeros((D_MODEL, IN0_PAD_WIDTH - IN0_WIDTH), F32)], axis=1).astype(BF16),
        va.T.astype(BF16))
    wq = (ab_w_q_up * ((MLA_NOPE + MLA_ROPE) ** -0.5 * LOG2E)).reshape(
        MLA_Q_RANK, MLA_HEADS, MLA_NOPE + MLA_ROPE)
    wq = jnp.pad(wq, ((0, 0), (0, 0), (0, MLA_QK_PAD - MLA_NOPE - MLA_ROPE)))
    w_q = wq.reshape(MLA_Q_RANK, MLA_HEADS * MLA_QK_PAD).astype(BF16)
    w_kv = ab_w_kv_up.astype(BF16)
    qc, kc, vc, gc = (c_w_in[:, i * WC:(i + 1) * WC] for i in range(4))
    w_qk1 = jnp.concatenate([qc * (HEAD_DIM ** -0.5 * LOG2E), kc], axis=1).astype(BF16)
    w_in1 = (w_qk1, vc.astype(BF16), gc.astype(BF16))
    return w_in0, w_q, w_kv, ab_w_out.astype(BF16), w_in1, c_w_out.astype(BF16)


def _trunk(x, w, tabs):
    batch, seq, _ = x.shape
    (w_in0, w_q, w_kv, w_out0, w_in1, w_out1, na_bias, q_g, kv_g,
     ln0_g, ln0_b, ln1_g, ln1_b) = w
    cos64, sin64, cos128, sin128 = tabs
    x2 = x.reshape(batch * seq, D_MODEL)
    proj0 = _proj_f32(x2, w_in0[0], tm=1024, tn=IN0_TN)
    vt_na = _proj_t(w_in0[1], x2, tm=1024)
    ya = _na_attention(proj0, vt_na, na_bias, batch, seq)
    q_all, k_all, vt_all = _mla_up(proj0, q_g, w_q, kv_g, w_kv, cos64, sin64, seq, tm=512)
    yb = _mla_attention(q_all, k_all, vt_all, proj0, batch, seq, tq=1024, tk=512)
    x1, x1b = _out_ln(ya, 0, yb, 0, w_out0, x2, ln0_g, ln0_b, tm=512)
    qk1 = _proj_heads(x1b, w_in1[0], cos128, sin128, seq, tm=1024, tn=1024, rope=True, dils=DIL_DILS)
    v1 = _proj_heads(x1b, w_in1[1], cos128, sin128, seq, tm=1024, tn=1024, rope=False, dils=DIL_DILS)
    gate1 = _proj_bf16(x1b, w_in1[2], tm=1024, tn=1024)
    yc = _dil_attention(qk1, v1, gate1, batch, seq)
    y, _ = _out_ln(yc, 0, yc, 1, w_out1, x1, ln1_g, ln1_b, tm=512)
    return y.reshape(batch, seq, D_MODEL)


def kernel(x_prompt, x_sample, ab_w_in, ab_rpb, ab_q_norm_g, ab_w_q_up, ab_kv_norm_g, ab_w_kv_up,
           ab_w_out, ab_ln_g, ab_ln_b, c_w_in, c_w_out, c_ln_g, c_ln_b):
    w_in0, w_q, w_kv, w_out0, w_in1, w_out1 = _prep_weights(
        ab_w_in, ab_w_q_up, ab_w_kv_up, ab_w_out, c_w_in, c_w_out)
    row = lambda a: a.reshape(1, -1).astype(F32)
    w = (w_in0, w_q, w_kv, w_out0, w_in1, w_out1, _na_bias(ab_rpb), row(ab_q_norm_g), row(ab_kv_norm_g),
         row(ab_ln_g), row(ab_ln_b), row(c_ln_g), row(c_ln_b))
    max_seq = max(x_prompt.shape[1], x_sample.shape[1])
    tabs = _rope_tables(max_seq, MLA_ROPE // 2) + _rope_tables(max_seq, HEAD_DIM // 2)
    return tuple(_trunk(x, w, tabs) for x in (x_prompt, x_sample))
```

```python
import functools
import math

import jax
import jax.numpy as jnp
from jax import lax
from jax.experimental import pallas as pl
from jax.experimental.pallas import tpu as pltpu

F32 = jnp.float32
BF16 = jnp.bfloat16

D_MODEL = 2048
DEPTH = 2
GRID_W = 64
HEAD_DIM = 128
NA_HEADS = 8
NA_WIN_ROWS = 8
NA_WIN_COLS = 16
MLA_HEADS = 8
MLA_Q_RANK = 512
MLA_KV_RANK = 512
MLA_NOPE = 128
MLA_ROPE = 64
MLA_V = 128
DIL_HEADS = 16
DIL_PAIRS = ((128, 1), (512, 4), (2048, 16))
DIL_HALF = 1024
WA = NA_HEADS * HEAD_DIM
WB = MLA_HEADS * MLA_V
WC = DIL_HEADS * HEAD_DIM
ROPE_THETA = 10000.0
ALPHA = (2 * DEPTH) ** 0.25
LN_EPS = 1e-5
RMS_EPS = 1e-6
NEG = -1e30
LOG2E = 1.4426950408889634

LANES = 128
MLA_QK_PAD = 2 * LANES
MXU_WIDTH = 256
IN0_WIDTH = 3 * WA + MLA_Q_RANK + MLA_KV_RANK + WB + MLA_ROPE
IN0_PAD_WIDTH = -(-IN0_WIDTH // MXU_WIDTH) * MXU_WIDTH
IN0_TN = IN0_PAD_WIDTH // 7
VMEM_LIMIT = 56 * 1024 * 1024

C0_Q, C0_K, C0_G = 0, 8, 16
C0_QLAT, C0_KVLAT = 6, 7
C0_GB = 32
C0_KROPE = 40

NA_QROWS = 8
NA_KROWS = 16
NA_TQ = NA_QROWS * GRID_W
NA_TK = NA_KROWS * GRID_W


def _params(n_axes):
    return pltpu.CompilerParams(dimension_semantics=("arbitrary",) * n_axes,
                                vmem_limit_bytes=VMEM_LIMIT)


def _proj_f32_kernel(x_ref, w_ref, o_ref, xb_ref):
    @pl.when(pl.program_id(1) == 0)
    def _():
        xb_ref[...] = x_ref[...].astype(BF16)

    o_ref[...] = jnp.dot(xb_ref[...], w_ref[...], preferred_element_type=F32).astype(o_ref.dtype)


def _proj_f32(x, w, tm, tn):
    m, k = x.shape
    n = w.shape[1]
    return pl.pallas_call(
        _proj_f32_kernel,
        grid=(m // tm, n // tn),
        in_specs=[pl.BlockSpec((tm, k), lambda i, j: (i, 0)),
                  pl.BlockSpec((k, tn), lambda i, j: (0, j))],
        out_specs=pl.BlockSpec((tm, tn), lambda i, j: (i, j)),
        out_shape=jax.ShapeDtypeStruct((m, n), BF16),
        scratch_shapes=[pltpu.VMEM((tm, k), BF16)],
        compiler_params=_params(2),
    )(x, w)


def _proj_t_kernel(w_ref, x_ref, o_ref):
    dn = (((1,), (1,)), ((), ()))
    o_ref[...] = lax.dot_general(w_ref[...], x_ref[...].astype(BF16), dn,
                                 preferred_element_type=F32).astype(o_ref.dtype)


def _proj_t(w_t, x, tm):
    n, k = w_t.shape
    m = x.shape[0]
    return pl.pallas_call(
        _proj_t_kernel,
        grid=(m // tm,),
        in_specs=[pl.BlockSpec((n, k), lambda i: (0, 0)),
                  pl.BlockSpec((tm, k), lambda i: (i, 0))],
        out_specs=pl.BlockSpec((n, tm), lambda i: (0, i)),
        out_shape=jax.ShapeDtypeStruct((n, m), BF16),
        compiler_params=_params(1),
    )(w_t, x)


def _proj_bf16_kernel(x_ref, w_ref, o_ref):
    o_ref[...] = jnp.dot(x_ref[...], w_ref[...], preferred_element_type=F32).astype(o_ref.dtype)


def _proj_bf16(x, w, tm, tn):
    m, k = x.shape
    n = w.shape[1]
    return pl.pallas_call(
        _proj_bf16_kernel,
        grid=(m // tm, n // tn),
        in_specs=[pl.BlockSpec((tm, k), lambda i, j: (i, 0)),
                  pl.BlockSpec((k, tn), lambda i, j: (0, j))],
        out_specs=pl.BlockSpec((tm, tn), lambda i, j: (i, j)),
        out_shape=jax.ShapeDtypeStruct((m, n), BF16),
        compiler_params=_params(2),
    )(x, w)


def _proj_heads_kernel(x_ref, w_ref, cos_ref, sin_ref, *rest, rope, dils):
    o_refs, scrs = rest[:len(dils)], rest[len(dils):]
    y = jnp.dot(x_ref[...], w_ref[...], preferred_element_type=F32)
    tm = y.shape[0]
    for h in range(y.shape[1] // LANES):
        yh = y[:, h * LANES:(h + 1) * LANES]
        if rope:
            yh = yh * cos_ref[...] + pltpu.roll(yh, LANES // 2, 1) * sin_ref[...]
        o_refs[0][h] = yh.astype(o_refs[0].dtype)
        scrs[0][h, 0] = yh
        for lvl in range(1, len(dils)):
            d_prev, d = dils[lvl - 1], dils[lvl]
            q = d // d_prev
            for r_prev in range(d_prev):
                for a in range(q):
                    part = scrs[lvl - 1][h, r_prev, pl.ds(a, tm // d, stride=q), :]
                    r = d_prev * a + r_prev
                    o_refs[lvl][h, :, r * LANES:(r + 1) * LANES] = part.astype(o_refs[lvl].dtype)
                    if lvl + 1 < len(dils):
                        scrs[lvl][h, r] = part


def _proj_heads(x, w, cos_t, sin_t, seq, tm, tn, rope, dils):
    m, k = x.shape
    n = w.shape[1]
    pos_blocks = seq // tm
    hb = tn // LANES
    return pl.pallas_call(
        functools.partial(_proj_heads_kernel, rope=rope, dils=dils),
        grid=(m // tm, n // tn),
        in_specs=[pl.BlockSpec((tm, k), lambda i, j: (i, 0)),
                  pl.BlockSpec((k, tn), lambda i, j: (0, j)),
                  pl.BlockSpec((tm, LANES), lambda i, j: (i % pos_blocks, 0)),
                  pl.BlockSpec((tm, LANES), lambda i, j: (i % pos_blocks, 0))],
        out_specs=[pl.BlockSpec((hb, tm // d, d * LANES), lambda i, j: (j, i, 0)) for d in dils],
        out_shape=[jax.ShapeDtypeStruct((n // LANES, m // d, d * LANES), BF16) for d in dils],
        scratch_shapes=[pltpu.VMEM((hb, d, tm // d, LANES), F32) for d in dils[:-1]],
        compiler_params=_params(2),
    )(x, w, cos_t, sin_t)


def _na_bias_kernel(rpb_ref, o_ref, t_ref):
    h = pl.program_id(0)
    n_dc = 2 * NA_WIN_COLS - 1
    kc = lax.broadcasted_iota(jnp.int32, (GRID_W, GRID_W), 0)
    c = lax.broadcasted_iota(jnp.int32, (GRID_W, GRID_W), 1)
    d = kc - c + (NA_WIN_COLS - 1)
    cs = jnp.clip(c - NA_WIN_COLS // 2, 0, GRID_W - NA_WIN_COLS)
    col_ok = (kc >= cs) & (kc < cs + NA_WIN_COLS)
    for dr in range(2 * NA_WIN_ROWS - 1):
        t = jnp.zeros((GRID_W, GRID_W), F32)
        for dd in range(n_dc):
            t = jnp.where(d == dd, rpb_ref[h, dr * n_dc + dd] * LOG2E, t)
        t_ref[dr] = jnp.where(col_ok, t, NEG)
    neg_blk = jnp.full((GRID_W, GRID_W), NEG, F32)
    for typ in range(3):
        for a in range(NA_QROWS):
            if typ == 0:
                lo, dr0 = max(a - 4, 0), 7 - a
            elif typ == 1:
                lo, dr0 = a, 3 - a
            else:
                lo, dr0 = 8 + min(a - 4, 0), -1 - a
            for kb in range(NA_KROWS):
                ok = lo <= kb < lo + NA_WIN_ROWS
                blk = t_ref[kb + dr0] if ok else neg_blk
                o_ref[0, typ, kb * GRID_W:(kb + 1) * GRID_W, a * GRID_W:(a + 1) * GRID_W] = blk


def _na_bias(rpb):
    rpb2 = rpb.reshape(NA_HEADS, -1)
    return pl.pallas_call(
        _na_bias_kernel,
        grid=(NA_HEADS,),
        in_specs=[pl.BlockSpec(memory_space=pltpu.SMEM)],
        out_specs=pl.BlockSpec((1, 3, NA_TK, NA_TQ), lambda h: (h, 0, 0, 0)),
        out_shape=jax.ShapeDtypeStruct((NA_HEADS, 3, NA_TK, NA_TQ), F32),
        scratch_shapes=[pltpu.VMEM((2 * NA_WIN_ROWS - 1, GRID_W, GRID_W), F32)],
        compiler_params=_params(1),
    )(rpb2)


def _na_kernel(q_ref, k0_ref, k1_ref, k2_ref, k3_ref, vt0_ref, vt1_ref, vt2_ref, vt3_ref,
               g_ref, b_ref, o_ref):
    q = q_ref[...]
    dn = (((1,), (1,)), ((), ()))
    k = jnp.concatenate([r[...] for r in (k0_ref, k1_ref, k2_ref, k3_ref)], axis=0)
    vt = jnp.concatenate([r[...] for r in (vt0_ref, vt1_ref, vt2_ref, vt3_ref)], axis=1)
    st = lax.dot_general(k, q, dn, preferred_element_type=F32) + b_ref[0, 0]
    m = jnp.max(st, axis=0, keepdims=True)
    p = jnp.exp2(st - m)
    l = jnp.sum(p, axis=0, keepdims=True)
    ot = jnp.dot(vt, p.astype(BF16), preferred_element_type=F32)
    g = g_ref[...].astype(F32)
    o_ref[...] = ((ot * (1.0 / l)).T * (g * jax.nn.sigmoid(g))).astype(o_ref.dtype)


def _na_attention(proj, vt_na, bias, batch, seq):
    rows = seq // GRID_W
    nj = rows // NA_QROWS
    kq = NA_TK // 4
    parts_per_seq = seq // kq

    def kbase(j):
        return jnp.clip(2 * j - 1, 0, rows // 4 - 4)

    def kv_spec(col0, part):
        return pl.BlockSpec((kq, LANES),
                            lambda h, j, b: (b * parts_per_seq + kbase(j) + part, col0 + h))

    def btype(j):
        return jnp.where(j == 0, 0, jnp.where(j == nj - 1, 2, 1))

    def vt_spec(part):
        return pl.BlockSpec((HEAD_DIM, kq), lambda h, j, b: (h, b * parts_per_seq + kbase(j) + part))

    tok_spec = lambda col0: pl.BlockSpec((NA_TQ, LANES), lambda h, j, b: (b * nj + j, col0 + h))
    in_specs = ([tok_spec(C0_Q)] + [kv_spec(C0_K, i) for i in range(4)]
                + [vt_spec(i) for i in range(4)] + [tok_spec(C0_G)]
                + [pl.BlockSpec((1, 1, NA_TK, NA_TQ), lambda h, j, b: (h, btype(j), 0, 0))])
    return pl.pallas_call(
        _na_kernel,
        grid=(NA_HEADS, nj, batch),
        in_specs=in_specs,
        out_specs=pl.BlockSpec((NA_TQ, LANES), lambda h, j, b: (b * nj + j, h)),
        out_shape=jax.ShapeDtypeStruct((batch * seq, WA), BF16),
        compiler_params=_params(3),
    )(*([proj] * 5), *([vt_na] * 4), proj, bias)


def _rms(lat_ref, g_ref):
    x = lat_ref[...].astype(F32)
    ms = jnp.mean(x * x, axis=1, keepdims=True)
    return (x * lax.rsqrt(ms + RMS_EPS) * g_ref[...]).astype(BF16)


def _rope64(t, cos_t, sin_t):
    lane = lax.broadcasted_iota(jnp.int32, t.shape, 1)
    half = MLA_ROPE // 2
    partner = jnp.where(lane < half, pltpu.roll(t, LANES - half, 1), pltpu.roll(t, half, 1))
    return t * cos_t + partner * sin_t


def _mla_q_up_kernel(lat_ref, g_ref, w_ref, cos_ref, sin_ref, o_ref):
    y = jnp.dot(_rms(lat_ref, g_ref), w_ref[...], preferred_element_type=F32)
    c = cos_ref[...]
    s = sin_ref[...]
    for h in range(MLA_HEADS):
        base = h * MLA_QK_PAD
        o_ref[:, base:base + LANES] = y[:, base:base + LANES].astype(o_ref.dtype)
        o_ref[:, base + LANES:base + 2 * LANES] = _rope64(
            y[:, base + LANES:base + 2 * LANES], c, s).astype(o_ref.dtype)


def _mla_kv_up_kernel(lat_ref, g_ref, w_ref, kr_ref, cos_ref, sin_ref, k_ref, vt_ref):
    y = jnp.dot(_rms(lat_ref, g_ref), w_ref[...], preferred_element_type=F32)
    k_pe = _rope64(kr_ref[...].astype(F32), cos_ref[...], sin_ref[...]).astype(k_ref.dtype)
    for h in range(MLA_HEADS):
        base = h * (MLA_NOPE + MLA_V)
        k_ref[:, h * MLA_QK_PAD:h * MLA_QK_PAD + LANES] = y[:, base:base + MLA_NOPE].astype(k_ref.dtype)
        k_ref[:, h * MLA_QK_PAD + LANES:(h + 1) * MLA_QK_PAD] = k_pe
        vt_ref[h * MLA_V:(h + 1) * MLA_V, :] = y[:, base + MLA_NOPE:base + MLA_NOPE + MLA_V].T.astype(vt_ref.dtype)


def _mla_up(proj, q_g, w_q, kv_g, w_kv, cos_t, sin_t, seq, tm):
    m = proj.shape[0]
    pos_blocks = seq // tm
    tab = pl.BlockSpec((tm, LANES), lambda i: (i % pos_blocks, 0))
    full = lambda a: pl.BlockSpec(a.shape, lambda i: (0,) * a.ndim)
    q_all = pl.pallas_call(
        _mla_q_up_kernel,
        grid=(m // tm,),
        in_specs=[pl.BlockSpec((tm, MLA_Q_RANK), lambda i: (i, C0_QLAT)), full(q_g), full(w_q), tab, tab],
        out_specs=pl.BlockSpec((tm, MLA_HEADS * MLA_QK_PAD), lambda i: (i, 0)),
        out_shape=jax.ShapeDtypeStruct((m, MLA_HEADS * MLA_QK_PAD), BF16),
        compiler_params=_params(1),
    )(proj, q_g, w_q, cos_t, sin_t)
    k_all, vt_all = pl.pallas_call(
        _mla_kv_up_kernel,
        grid=(m // tm,),
        in_specs=[pl.BlockSpec((tm, MLA_KV_RANK), lambda i: (i, C0_KVLAT)), full(kv_g), full(w_kv),
                  pl.BlockSpec((tm, LANES), lambda i: (i, C0_KROPE)), tab, tab],
        out_specs=[pl.BlockSpec((tm, MLA_HEADS * MLA_QK_PAD), lambda i: (i, 0)),
                   pl.BlockSpec((WB, tm), lambda i: (0, i))],
        out_shape=[jax.ShapeDtypeStruct((m, MLA_HEADS * MLA_QK_PAD), BF16),
                   jax.ShapeDtypeStruct((WB, m), BF16)],
        compiler_params=_params(1),
    )(proj, kv_g, w_kv, proj, cos_t, sin_t)
    return q_all, k_all, vt_all


MLA_UNROLL = 8


def _mla_attn_kernel(q_ref, k_ref, vt_ref, g_ref, o_ref, s0, s1, m_s, l_s, acc_s, *, tk):
    q = q_ref[...]
    tq = q.shape[0]
    n = k_ref.shape[0] // tk
    dn = (((1,), (1,)), ((), ()))

    def scores(c):
        start = pl.multiple_of(c * tk, tk)
        return lax.dot_general(k_ref[pl.ds(start, tk), :], q, dn, preferred_element_type=F32)

    def update(c, st):
        vt = vt_ref[:, pl.ds(pl.multiple_of(c * tk, tk), tk)]
        m = m_s[...]
        m_new = jnp.maximum(m, jnp.max(st, axis=0, keepdims=True))
        alpha = jnp.exp2(m - m_new)
        p = jnp.exp2(st - m_new)
        m_s[...] = m_new
        l_s[...] = alpha * l_s[...] + jnp.sum(p, axis=0, keepdims=True)
        acc_s[...] = alpha * acc_s[...] + jnp.dot(vt, p.astype(BF16), preferred_element_type=F32)

    m_s[...] = jnp.full(m_s.shape, NEG, F32)
    l_s[...] = jnp.zeros(l_s.shape, F32)
    acc_s[...] = jnp.zeros(acc_s.shape, F32)
    s0[...] = scores(0)

    bufs = (s0, s1)
    unroll = math.gcd(MLA_UNROLL, n)

    def body(i, carry):
        c0 = unroll * i
        for u in range(unroll):
            bufs[(u + 1) % 2][...] = scores(jnp.minimum(c0 + u + 1, n - 1))
            update(c0 + u, bufs[u % 2][...])
        return carry

    lax.fori_loop(0, n // unroll, body, 0)
    g = g_ref[...].astype(F32)
    o_ref[...] = ((acc_s[...] * (1.0 / l_s[...])).T * (g * jax.nn.sigmoid(g))).astype(o_ref.dtype)


def _mla_attention(q_all, k_all, vt_all, proj, batch, seq, tq, tk):
    nq = seq // tq
    n = seq // tk
    assert n % 2 == 0 and MLA_UNROLL % 2 == 0
    return pl.pallas_call(
        functools.partial(_mla_attn_kernel, tk=tk),
        grid=(batch, MLA_HEADS, nq),
        in_specs=[pl.BlockSpec((tq, MLA_QK_PAD), lambda b, h, i: (b * nq + i, h)),
                  pl.BlockSpec((seq, MLA_QK_PAD), lambda b, h, i: (b, h)),
                  pl.BlockSpec((MLA_V, seq), lambda b, h, i: (h, b)),
                  pl.BlockSpec((tq, LANES), lambda b, h, i: (b * nq + i, C0_GB + h))],
        out_specs=pl.BlockSpec((tq, MLA_V), lambda b, h, i: (b * nq + i, h)),
        out_shape=jax.ShapeDtypeStruct((batch * seq, WB), BF16),
        scratch_shapes=[pltpu.VMEM((tk, tq), F32), pltpu.VMEM((tk, tq), F32), pltpu.VMEM((1, tq), F32),
                        pltpu.VMEM((1, tq), F32), pltpu.VMEM((MLA_V, tq), F32)],
        compiler_params=_params(3),
    )(q_all, k_all, vt_all, proj)


def _out_ln_kernel(y1_ref, y2_ref, w_ref, x_ref, g_ref, b_ref, o_ref, ob_ref):
    half = y1_ref.shape[1]
    y = jnp.dot(y1_ref[...], w_ref[:half, :], preferred_element_type=F32)
    y = y + jnp.dot(y2_ref[...], w_ref[half:, :], preferred_element_type=F32)
    z = ALPHA * x_ref[...] + y
    mu = jnp.mean(z, axis=1, keepdims=True)
    zc = z - mu
    var = jnp.mean(zc * zc, axis=1, keepdims=True)
    out = zc * lax.rsqrt(var + LN_EPS) * g_ref[...] + b_ref[...]
    o_ref[...] = out
    ob_ref[...] = out.astype(ob_ref.dtype)


def _out_ln(y1, y1_col, y2, y2_col, w, x, g, b, tm):
    m, d = x.shape
    half = w.shape[0] // 2
    full = lambda a: pl.BlockSpec(a.shape, lambda i: (0,) * a.ndim)
    return pl.pallas_call(
        _out_ln_kernel,
        grid=(m // tm,),
        in_specs=[pl.BlockSpec((tm, half), lambda i: (i, y1_col)),
                  pl.BlockSpec((tm, half), lambda i: (i, y2_col)),
                  full(w), pl.BlockSpec((tm, d), lambda i: (i, 0)), full(g), full(b)],
        out_specs=[pl.BlockSpec((tm, d), lambda i: (i, 0)), pl.BlockSpec((tm, d), lambda i: (i, 0))],
        out_shape=[jax.ShapeDtypeStruct((m, d), F32), jax.ShapeDtypeStruct((m, d), BF16)],
        compiler_params=_params(1),
    )(y1, y2, w, x, g, b)


DIL_DILS = tuple(d for _, d in DIL_PAIRS)
DIL_SIDE = DIL_PAIRS[0][0] // 2 // DIL_PAIRS[0][1]
DIL_TQ = 128
DIL_TK = DIL_TQ + 2 * DIL_SIDE
DIL_SB = DIL_TQ * DIL_DILS[-1]
assert all(w // 2 // d == DIL_SIDE for w, d in DIL_PAIRS)


def _dil_kernel(*refs, nsb):
    q_refs = refs[0:3]
    k_refs = [refs[3 + 3 * p:6 + 3 * p] for p in range(3)]
    v_refs = [refs[12 + 3 * p:15 + 3 * p] for p in range(3)]
    g_ref, o_ref, bias_s, o_s, l_s = refs[21:]
    sb = pl.program_id(2)
    dn = (((1,), (1,)), ((), ()))

    qi = lax.broadcasted_iota(jnp.int32, (DIL_TQ, DIL_TK), 0)
    kj = lax.broadcasted_iota(jnp.int32, (DIL_TQ, DIL_TK), 1)
    base = jnp.where((kj >= qi) & (kj <= qi + 2 * DIL_SIDE), 0.0, NEG).astype(F32)
    lo = jnp.where(kj < DIL_SIDE, NEG, 0.0) * (sb == 0).astype(F32)
    hi = jnp.where(kj >= DIL_TK - DIL_SIDE, NEG, 0.0) * (sb == nsb - 1).astype(F32)
    bias_s[0] = base
    bias_s[1] = base + lo
    bias_s[2] = base + hi
    bias_s[3] = base + lo + hi

    def window(trio, lanes, c, n):
        main, prev, nxt = trio
        if n == 1:
            return jnp.concatenate([prev[:, lanes], main[:, lanes], nxt[:, lanes]], axis=0), 3
        if c == 0:
            return jnp.concatenate([prev[:, lanes], main[0:DIL_TK - DIL_SIDE, lanes]], axis=0), 1
        if c == n - 1:
            return jnp.concatenate([main[n * DIL_TQ - (DIL_TK - DIL_SIDE):n * DIL_TQ, lanes],
                                    nxt[:, lanes]], axis=0), 2
        return main[c * DIL_TQ - DIL_SIDE:c * DIL_TQ + DIL_TQ + DIL_SIDE, lanes], 0

    def tile(q, kw, vw, mask_id):
        s = lax.dot_general(q, kw, dn, preferred_element_type=F32) + bias_s[mask_id]
        m = jnp.max(s, axis=1, keepdims=True)
        p = jnp.exp2(s - m)
        l = jnp.sum(p, axis=1, keepdims=True)
        pv = jnp.dot(p.astype(BF16), vw, preferred_element_type=F32)
        return pv * (1.0 / l), jnp.broadcast_to(m + jnp.log2(l), (DIL_TQ, LANES))

    for pair in reversed(range(len(DIL_DILS))):
        dil = DIL_DILS[pair]
        n = DIL_SB // dil // DIL_TQ
        for r in range(dil):
            lanes = slice(r * LANES, (r + 1) * LANES)
            for c in range(n):
                kw, mask_id = window(k_refs[pair], lanes, c, n)
                vw, _ = window(v_refs[pair], lanes, c, n)
                q = q_refs[pair][c * DIL_TQ:(c + 1) * DIL_TQ, lanes]
                o, lse = tile(q, kw, vw, mask_id)
                start = (c * DIL_TQ) * dil + r
                if dil > 1:
                    rows = pl.ds(start, DIL_TQ, stride=dil)
                    o_s[pair - 1, rows, :] = o
                    l_s[pair - 1, rows, :] = lse
                    continue
                rows = pl.ds(start, DIL_TQ)
                lses = [lse] + [l_s[i, rows, :] for i in range(len(DIL_DILS) - 1)]
                outs = [o] + [o_s[i, rows, :] for i in range(len(DIL_DILS) - 1)]
                top = functools.reduce(jnp.maximum, lses)
                w = [jnp.exp2(x - top) for x in lses]
                num = sum(wi * oi for wi, oi in zip(w, outs))
                g = g_ref[rows, :].astype(F32)
                o_ref[rows, :] = (num / sum(w) * (g * jax.nn.sigmoid(g))).astype(o_ref.dtype)


def _dil_attention(qk_views, v_views, gate, batch, seq):
    tokens = gate.shape[0]
    nsb = seq // DIL_SB
    halo = DIL_SIDE
    specs_main, specs_halo = [], []
    for dil in DIL_DILS:
        rows, width = tokens // dil, dil * LANES
        main_rows = DIL_SB // dil
        per_main = main_rows // halo
        last = rows // halo - 1

        def main_spec(h0, main_rows=main_rows, width=width):
            return pl.BlockSpec((None, main_rows, width), lambda b, h, s: (h0 + h, b * nsb + s, 0))

        def prev_spec(h0, per_main=per_main, width=width):
            return pl.BlockSpec((None, halo, width),
                                lambda b, h, s: (h0 + h, jnp.maximum((b * nsb + s) * per_main - 1, 0), 0))

        def next_spec(h0, per_main=per_main, width=width, last=last):
            return pl.BlockSpec((None, halo, width),
                                lambda b, h, s: (h0 + h, jnp.minimum((b * nsb + s + 1) * per_main, last), 0))

        specs_main.append(main_spec)
        specs_halo.append((prev_spec, next_spec))
    in_specs = [specs_main[p](0) for p in range(3)]
    operands = list(qk_views)
    for h0, views in ((DIL_HEADS, qk_views), (0, v_views)):
        for p in range(3):
            in_specs += [specs_main[p](h0), specs_halo[p][0](h0), specs_halo[p][1](h0)]
            operands += [views[p]] * 3
    in_specs.append(pl.BlockSpec((DIL_SB, LANES), lambda b, h, s: (b * nsb + s, h)))
    operands.append(gate)
    return pl.pallas_call(
        functools.partial(_dil_kernel, nsb=nsb),
        grid=(batch, DIL_HEADS, nsb),
        in_specs=in_specs,
        out_specs=pl.BlockSpec((DIL_SB, LANES), lambda b, h, s: (b * nsb + s, h)),
        out_shape=jax.ShapeDtypeStruct((tokens, WC), BF16),
        scratch_shapes=[pltpu.VMEM((4, DIL_TQ, DIL_TK), F32),
                        pltpu.VMEM((len(DIL_DILS) - 1, DIL_SB, LANES), F32),
                        pltpu.VMEM((len(DIL_DILS) - 1, DIL_SB, LANES), F32)],
        compiler_params=_params(3),
    )(*operands)


def _rope_tables(seq, half):
    inv = ROPE_THETA ** (-jnp.arange(half, dtype=F32) / half)
    ang = jnp.arange(seq, dtype=F32)[:, None] * inv[None, :]
    cos, sin = jnp.cos(ang), jnp.sin(ang)
    pad = jnp.zeros((seq, LANES - 2 * half), F32)
    return jnp.concatenate([cos, cos, pad], 1), jnp.concatenate([-sin, sin, pad], 1)


def _prep_weights(ab_w_in, ab_w_q_up, ab_w_kv_up, ab_w_out, c_w_in, c_w_out):
    qa, ka, va, ga, q_lat, kv_lat, k_rope, gb = jnp.split(
        ab_w_in, [WA, 2 * WA, 3 * WA, 4 * WA, 4 * WA + MLA_Q_RANK,
                  4 * WA + MLA_Q_RANK + MLA_KV_RANK, 4 * WA + MLA_Q_RANK + MLA_KV_RANK + MLA_ROPE], axis=1)
    w_in0 = (jnp.concatenate(
        [qa * (HEAD_DIM ** -0.5 * LOG2E), ka, ga, q_lat, kv_lat, gb, k_rope,
         jnp.zeros((D_MODEL, IN0_PAD_WIDTH - IN0_WIDTH), F32)], axis=1).astype(BF16),
        va.T.astype(BF16))
    wq = (ab_w_q_up * ((MLA_NOPE + MLA_ROPE) ** -0.5 * LOG2E)).reshape(
        MLA_Q_RANK, MLA_HEADS, MLA_NOPE + MLA_ROPE)
    wq = jnp.pad(wq, ((0, 0), (0, 0), (0, MLA_QK_PAD - MLA_NOPE - MLA_ROPE)))
    w_q = wq.reshape(MLA_Q_RANK, MLA_HEADS * MLA_QK_PAD).astype(BF16)
    w_kv = ab_w_kv_up.astype(BF16)
    qc, kc, vc, gc = (c_w_in[:, i * WC:(i + 1) * WC] for i in range(4))
    w_qk1 = jnp.concatenate([qc * (HEAD_DIM ** -0.5 * LOG2E), kc], axis=1).astype(BF16)
    w_in1 = (w_qk1, vc.astype(BF16), gc.astype(BF16))
    return w_in0, w_q, w_kv, ab_w_out.astype(BF16), w_in1, c_w_out.astype(BF16)


def _trunk(x, w, tabs):
    batch, seq, _ = x.shape
    (w_in0, w_q, w_kv, w_out0, w_in1, w_out1, na_bias, q_g, kv_g,
     ln0_g, ln0_b, ln1_g, ln1_b) = w
    cos64, sin64, cos128, sin128 = tabs
    x2 = x.reshape(batch * seq, D_MODEL)
    proj0 = _proj_f32(x2, w_in0[0], tm=1024, tn=IN0_TN)
    vt_na = _proj_t(w_in0[1], x2, tm=1024)
    ya = _na_attention(proj0, vt_na, na_bias, batch, seq)
    q_all, k_all, vt_all = _mla_up(proj0, q_g, w_q, kv_g, w_kv, cos64, sin64, seq, tm=512)
    yb = _mla_attention(q_all, k_all, vt_all, proj0, batch, seq, tq=1024, tk=1024)
    x1, x1b = _out_ln(ya, 0, yb, 0, w_out0, x2, ln0_g, ln0_b, tm=512)
    qk1 = _proj_heads(x1b, w_in1[0], cos128, sin128, seq, tm=1024, tn=1024, rope=True, dils=DIL_DILS)
    v1 = _proj_heads(x1b, w_in1[1], cos128, sin128, seq, tm=1024, tn=1024, rope=False, dils=DIL_DILS)
    gate1 = _proj_bf16(x1b, w_in1[2], tm=1024, tn=1024)
    yc = _dil_attention(qk1, v1, gate1, batch, seq)
    y, _ = _out_ln(yc, 0, yc, 1, w_out1, x1, ln1_g, ln1_b, tm=512)
    return y.reshape(batch, seq, D_MODEL)


def kernel(x_prompt, x_sample, ab_w_in, ab_rpb, ab_q_norm_g, ab_w_q_up, ab_kv_norm_g, ab_w_kv_up,
           ab_w_out, ab_ln_g, ab_ln_b, c_w_in, c_w_out, c_ln_g, c_ln_b):
    w_in0, w_q, w_kv, w_out0, w_in1, w_out1 = _prep_weights(
        ab_w_in, ab_w_q_up, ab_w_kv_up, ab_w_out, c_w_in, c_w_out)
    row = lambda a: a.reshape(1, -1).astype(F32)
    w = (w_in0, w_q, w_kv, w_out0, w_in1, w_out1, _na_bias(ab_rpb), row(ab_q_norm_g), row(ab_kv_norm_g),
         row(ab_ln_g), row(ab_ln_b), row(c_ln_g), row(c_ln_b))
    max_seq = max(x_prompt.shape[1], x_sample.shape[1])
    tabs = _rope_tables(max_seq, MLA_ROPE // 2) + _rope_tables(max_seq, HEAD_DIM // 2)
    return tuple(_trunk(x, w, tabs) for x in (x_prompt, x_sample))
```

```python
import functools
import math

import jax
import jax.numpy as jnp
from jax import lax
from jax.experimental import pallas as pl
from jax.experimental.pallas import tpu as pltpu

F32 = jnp.float32
BF16 = jnp.bfloat16

D_MODEL = 2048
DEPTH = 2
GRID_W = 64
HEAD_DIM = 128
NA_HEADS = 8
NA_WIN_ROWS = 8
NA_WIN_COLS = 16
MLA_HEADS = 8
MLA_Q_RANK = 512
MLA_KV_RANK = 512
MLA_NOPE = 128
MLA_ROPE = 64
MLA_V = 128
DIL_HEADS = 16
DIL_PAIRS = ((128, 1), (512, 4), (2048, 16))
DIL_HALF = 1024
WA = NA_HEADS * HEAD_DIM
WB = MLA_HEADS * MLA_V
WC = DIL_HEADS * HEAD_DIM
ROPE_THETA = 10000.0
ALPHA = (2 * DEPTH) ** 0.25
LN_EPS = 1e-5
RMS_EPS = 1e-6
NEG = -1e30
LOG2E = 1.4426950408889634

LANES = 128
MLA_QK_PAD = 2 * LANES
MXU_WIDTH = 256
IN0_WIDTH = 3 * WA + MLA_Q_RANK + MLA_KV_RANK + WB + MLA_ROPE
IN0_PAD_WIDTH = -(-IN0_WIDTH // MXU_WIDTH) * MXU_WIDTH
IN0_TN = IN0_PAD_WIDTH // 7
VMEM_LIMIT = 56 * 1024 * 1024

C0_Q, C0_K, C0_G = 0, 8, 16
C0_QLAT, C0_KVLAT = 6, 7
C0_GB = 32
C0_KROPE = 40

NA_QROWS = 8
NA_KROWS = 16
NA_TQ = NA_QROWS * GRID_W
NA_TK = NA_KROWS * GRID_W
NA_GROUP = 4


def _params(n_axes):
    return pltpu.CompilerParams(dimension_semantics=("arbitrary",) * n_axes,
                                vmem_limit_bytes=VMEM_LIMIT)


def _proj_f32_kernel(x_ref, w_ref, o_ref, xb_ref):
    @pl.when(pl.program_id(1) == 0)
    def _():
        xb_ref[...] = x_ref[...].astype(BF16)

    o_ref[...] = jnp.dot(xb_ref[...], w_ref[...], preferred_element_type=F32).astype(o_ref.dtype)


def _proj_f32(x, w, tm, tn):
    m, k = x.shape
    n = w.shape[1]
    return pl.pallas_call(
        _proj_f32_kernel,
        grid=(m // tm, n // tn),
        in_specs=[pl.BlockSpec((tm, k), lambda i, j: (i, 0)),
                  pl.BlockSpec((k, tn), lambda i, j: (0, j))],
        out_specs=pl.BlockSpec((tm, tn), lambda i, j: (i, j)),
        out_shape=jax.ShapeDtypeStruct((m, n), BF16),
        scratch_shapes=[pltpu.VMEM((tm, k), BF16)],
        compiler_params=_params(2),
    )(x, w)


def _proj_t_kernel(w_ref, x_ref, o_ref):
    dn = (((1,), (1,)), ((), ()))
    o_ref[...] = lax.dot_general(w_ref[...], x_ref[...].astype(BF16), dn,
                                 preferred_element_type=F32).astype(o_ref.dtype)


def _proj_t(w_t, x, batch, seq, tm):
    n, k = w_t.shape
    per_seq = seq // tm
    return pl.pallas_call(
        _proj_t_kernel,
        grid=(batch * per_seq,),
        in_specs=[pl.BlockSpec((n, k), lambda i: (0, 0)),
                  pl.BlockSpec((tm, k), lambda i: (i, 0))],
        out_specs=pl.BlockSpec((None, n, tm), lambda i: (i // per_seq, 0, i % per_seq)),
        out_shape=jax.ShapeDtypeStruct((batch, n, seq), BF16),
        compiler_params=_params(1),
    )(w_t, x)


def _proj_bf16_kernel(x_ref, w_ref, o_ref):
    o_ref[...] = jnp.dot(x_ref[...], w_ref[...], preferred_element_type=F32).astype(o_ref.dtype)


def _proj_bf16(x, w, tm, tn):
    m, k = x.shape
    n = w.shape[1]
    return pl.pallas_call(
        _proj_bf16_kernel,
        grid=(m // tm, n // tn),
        in_specs=[pl.BlockSpec((tm, k), lambda i, j: (i, 0)),
                  pl.BlockSpec((k, tn), lambda i, j: (0, j))],
        out_specs=pl.BlockSpec((tm, tn), lambda i, j: (i, j)),
        out_shape=jax.ShapeDtypeStruct((m, n), BF16),
        compiler_params=_params(2),
    )(x, w)


PROJ_SPLIT = 2


def _proj_heads_kernel(x_ref, w_ref, cos_ref, sin_ref, *rest, rope, dils):
    o_refs, scrs = rest[:len(dils)], rest[len(dils):]
    tg = x_ref.shape[0] // PROJ_SPLIT

    def project(g):
        return jnp.dot(x_ref[g * tg:(g + 1) * tg, :], w_ref[...], preferred_element_type=F32)

    def emit(g, y):
        for h in range(y.shape[1] // LANES):
            yh = y[:, h * LANES:(h + 1) * LANES]
            if rope:
                rows = slice(g * tg, (g + 1) * tg)
                yh = yh * cos_ref[rows, :] + pltpu.roll(yh, LANES // 2, 1) * sin_ref[rows, :]
            o_refs[0][h, g * tg:(g + 1) * tg, :] = yh.astype(o_refs[0].dtype)
            scrs[0][h, 0, g * tg:(g + 1) * tg, :] = yh
            for lvl in range(1, len(dils)):
                d_prev, d = dils[lvl - 1], dils[lvl]
                q, n = d // d_prev, tg // d
                for r_prev in range(d_prev):
                    for a in range(q):
                        part = scrs[lvl - 1][h, r_prev, pl.ds(g * tg // d_prev + a, n, stride=q), :]
                        r = d_prev * a + r_prev
                        o_refs[lvl][h, g * n:(g + 1) * n, r * LANES:(r + 1) * LANES] = (
                            part.astype(o_refs[lvl].dtype))
                        if lvl + 1 < len(dils):
                            scrs[lvl][h, r, g * n:(g + 1) * n, :] = part

    y_prev = project(0)
    for g in range(1, PROJ_SPLIT):
        y_next = project(g)
        emit(g - 1, y_prev)
        y_prev = y_next
    emit(PROJ_SPLIT - 1, y_prev)


def _proj_heads(x, w, cos_t, sin_t, seq, tm, tn, rope, dils):
    m, k = x.shape
    n = w.shape[1]
    pos_blocks = seq // tm
    hb = tn // LANES
    return pl.pallas_call(
        functools.partial(_proj_heads_kernel, rope=rope, dils=dils),
        grid=(m // tm, n // tn),
        in_specs=[pl.BlockSpec((tm, k), lambda i, j: (i, 0)),
                  pl.BlockSpec((k, tn), lambda i, j: (0, j)),
                  pl.BlockSpec((tm, LANES), lambda i, j: (i % pos_blocks, 0)),
                  pl.BlockSpec((tm, LANES), lambda i, j: (i % pos_blocks, 0))],
        out_specs=[pl.BlockSpec((hb, tm // d, d * LANES), lambda i, j: (j, i, 0)) for d in dils],
        out_shape=[jax.ShapeDtypeStruct((n // LANES, m // d, d * LANES), BF16) for d in dils],
        scratch_shapes=[pltpu.VMEM((hb, d, tm // d, LANES), F32) for d in dils[:-1]],
        compiler_params=_params(2),
    )(x, w, cos_t, sin_t)


def _na_bias_kernel(rpb_ref, o_ref, t_ref):
    h = pl.program_id(0)
    n_dc = 2 * NA_WIN_COLS - 1
    kc = lax.broadcasted_iota(jnp.int32, (GRID_W, GRID_W), 0)
    c = lax.broadcasted_iota(jnp.int32, (GRID_W, GRID_W), 1)
    d = kc - c + (NA_WIN_COLS - 1)
    cs = jnp.clip(c - NA_WIN_COLS // 2, 0, GRID_W - NA_WIN_COLS)
    col_ok = (kc >= cs) & (kc < cs + NA_WIN_COLS)
    for dr in range(2 * NA_WIN_ROWS - 1):
        t = jnp.zeros((GRID_W, GRID_W), F32)
        for dd in range(n_dc):
            t = jnp.where(d == dd, rpb_ref[h, dr * n_dc + dd] * LOG2E, t)
        t_ref[dr] = jnp.where(col_ok, t, NEG)
    neg_blk = jnp.full((GRID_W, GRID_W), NEG, F32)
    for typ in range(3):
        for a in range(NA_QROWS):
            if typ == 0:
                lo, dr0 = max(a - 4, 0), 7 - a
            elif typ == 1:
                lo, dr0 = a, 3 - a
            else:
                lo, dr0 = 8 + min(a - 4, 0), -1 - a
            for kb in range(NA_KROWS):
                ok = lo <= kb < lo + NA_WIN_ROWS
                blk = t_ref[kb + dr0] if ok else neg_blk
                o_ref[0, typ, kb * GRID_W:(kb + 1) * GRID_W, a * GRID_W:(a + 1) * GRID_W] = blk


def _na_bias(rpb):
    rpb2 = rpb.reshape(NA_HEADS, -1)
    return pl.pallas_call(
        _na_bias_kernel,
        grid=(NA_HEADS,),
        in_specs=[pl.BlockSpec(memory_space=pltpu.SMEM)],
        out_specs=pl.BlockSpec((1, 3, NA_TK, NA_TQ), lambda h: (h, 0, 0, 0)),
        out_shape=jax.ShapeDtypeStruct((NA_HEADS, 3, NA_TK, NA_TQ), F32),
        scratch_shapes=[pltpu.VMEM((2 * NA_WIN_ROWS - 1, GRID_W, GRID_W), F32)],
        compiler_params=_params(1),
    )(rpb2)


def _na_kernel(q_ref, k0_ref, k1_ref, k2_ref, k3_ref, vt0_ref, vt1_ref, vt2_ref, vt3_ref,
               g_ref, b_ref, o_ref):
    dn = (((1,), (1,)), ((), ()))
    bias = b_ref[0, 0]
    sts = []
    for e in range(q_ref.shape[0]):
        k = jnp.concatenate([r[e] for r in (k0_ref, k1_ref, k2_ref, k3_ref)], axis=0)
        sts.append(lax.dot_general(k, q_ref[e], dn, preferred_element_type=F32) + bias)
    for e, st in enumerate(sts):
        vt = jnp.concatenate([r[e] for r in (vt0_ref, vt1_ref, vt2_ref, vt3_ref)], axis=1)
        m = jnp.max(st, axis=0, keepdims=True)
        p = jnp.exp2(st - m)
        l = jnp.sum(p, axis=0, keepdims=True)
        ot = jnp.dot(vt, p.astype(BF16), preferred_element_type=F32)
        g = g_ref[e].astype(F32)
        o_ref[e] = ((ot * (1.0 / l)).T * (g * jax.nn.sigmoid(g))).astype(o_ref.dtype)


def _na_attention(proj, vt_na, bias, batch, seq):
    rows = seq // GRID_W
    nj = rows // NA_QROWS
    kq = NA_TK // 4
    proj3 = proj.reshape(batch, seq, proj.shape[1])
    group = math.gcd(batch, NA_GROUP)

    def kbase(j):
        return jnp.clip(2 * j - 1, 0, rows // 4 - 4)

    def k_spec(part):
        return pl.BlockSpec((group, kq, LANES), lambda h, j, b: (b, kbase(j) + part, C0_K + h))

    def vt_spec(part):
        return pl.BlockSpec((group, HEAD_DIM, kq), lambda h, j, b: (b, h, kbase(j) + part))

    def btype(j):
        return jnp.where(j == 0, 0, jnp.where(j == nj - 1, 2, 1))

    tok_spec = lambda col0: pl.BlockSpec((group, NA_TQ, LANES), lambda h, j, b: (b, j, col0 + h))
    in_specs = ([tok_spec(C0_Q)] + [k_spec(i) for i in range(4)] + [vt_spec(i) for i in range(4)]
                + [tok_spec(C0_G)]
                + [pl.BlockSpec((1, 1, NA_TK, NA_TQ), lambda h, j, b: (h, btype(j), 0, 0))])
    out = pl.pallas_call(
        _na_kernel,
        grid=(NA_HEADS, nj, batch // group),
        in_specs=in_specs,
        out_specs=pl.BlockSpec((group, NA_TQ, LANES), lambda h, j, b: (b, j, h)),
        out_shape=jax.ShapeDtypeStruct((batch, seq, WA), BF16),
        compiler_params=_params(3),
    )(*([proj3] * 5), *([vt_na] * 4), proj3, bias)
    return out.reshape(batch * seq, WA)


def _rms(lat_ref, g_ref):
    x = lat_ref[...].astype(F32)
    ms = jnp.mean(x * x, axis=1, keepdims=True)
    return (x * lax.rsqrt(ms + RMS_EPS) * g_ref[...]).astype(BF16)


def _rope64(t, cos_t, sin_t):
    lane = lax.broadcasted_iota(jnp.int32, t.shape, 1)
    half = MLA_ROPE // 2
    partner = jnp.where(lane < half, pltpu.roll(t, LANES - half, 1), pltpu.roll(t, half, 1))
    return t * cos_t + partner * sin_t


def _mla_q_up_kernel(lat_ref, g_ref, w_ref, cos_ref, sin_ref, o_ref):
    y = jnp.dot(_rms(lat_ref, g_ref), w_ref[...], preferred_element_type=F32)
    c = cos_ref[...]
    s = sin_ref[...]
    for h in range(MLA_HEADS):
        base = h * MLA_QK_PAD
        o_ref[:, base:base + LANES] = y[:, base:base + LANES].astype(o_ref.dtype)
        o_ref[:, base + LANES:base + 2 * LANES] = _rope64(
            y[:, base + LANES:base + 2 * LANES], c, s).astype(o_ref.dtype)


def _mla_kv_up_kernel(lat_ref, g_ref, w_ref, kr_ref, cos_ref, sin_ref, k_ref, vt_ref):
    y = jnp.dot(_rms(lat_ref, g_ref), w_ref[...], preferred_element_type=F32)
    k_pe = _rope64(kr_ref[...].astype(F32), cos_ref[...], sin_ref[...]).astype(k_ref.dtype)
    for h in range(MLA_HEADS):
        base = h * (MLA_NOPE + MLA_V)
        k_ref[:, h * MLA_QK_PAD:h * MLA_QK_PAD + LANES] = y[:, base:base + MLA_NOPE].astype(k_ref.dtype)
        k_ref[:, h * MLA_QK_PAD + LANES:(h + 1) * MLA_QK_PAD] = k_pe
        vt_ref[h * MLA_V:(h + 1) * MLA_V, :] = y[:, base + MLA_NOPE:base + MLA_NOPE + MLA_V].T.astype(vt_ref.dtype)


def _mla_up(proj, q_g, w_q, kv_g, w_kv, cos_t, sin_t, seq, tm):
    m = proj.shape[0]
    pos_blocks = seq // tm
    tab = pl.BlockSpec((tm, LANES), lambda i: (i % pos_blocks, 0))
    full = lambda a: pl.BlockSpec(a.shape, lambda i: (0,) * a.ndim)
    q_all = pl.pallas_call(
        _mla_q_up_kernel,
        grid=(m // tm,),
        in_specs=[pl.BlockSpec((tm, MLA_Q_RANK), lambda i: (i, C0_QLAT)), full(q_g), full(w_q), tab, tab],
        out_specs=pl.BlockSpec((tm, MLA_HEADS * MLA_QK_PAD), lambda i: (i, 0)),
        out_shape=jax.ShapeDtypeStruct((m, MLA_HEADS * MLA_QK_PAD), BF16),
        compiler_params=_params(1),
    )(proj, q_g, w_q, cos_t, sin_t)
    k_all, vt_all = pl.pallas_call(
        _mla_kv_up_kernel,
        grid=(m // tm,),
        in_specs=[pl.BlockSpec((tm, MLA_KV_RANK), lambda i: (i, C0_KVLAT)), full(kv_g), full(w_kv),
                  pl.BlockSpec((tm, LANES), lambda i: (i, C0_KROPE)), tab, tab],
        out_specs=[pl.BlockSpec((tm, MLA_HEADS * MLA_QK_PAD), lambda i: (i, 0)),
                   pl.BlockSpec((WB, tm), lambda i: (0, i))],
        out_shape=[jax.ShapeDtypeStruct((m, MLA_HEADS * MLA_QK_PAD), BF16),
                   jax.ShapeDtypeStruct((WB, m), BF16)],
        compiler_params=_params(1),
    )(proj, kv_g, w_kv, proj, cos_t, sin_t)
    return q_all, k_all, vt_all


MLA_UNROLL = 8


def _mla_attn_kernel(q_ref, k_ref, vt_ref, g_ref, o_ref, s0, s1, m_s, l_s, acc_s, *, tk):
    q = q_ref[...]
    tq = q.shape[0]
    n = k_ref.shape[0] // tk
    dn = (((1,), (1,)), ((), ()))

    def scores(c):
        start = pl.multiple_of(c * tk, tk)
        return lax.dot_general(k_ref[pl.ds(start, tk), :], q, dn, preferred_element_type=F32)

    def update(c, st):
        vt = vt_ref[:, pl.ds(pl.multiple_of(c * tk, tk), tk)]
        m = m_s[...]
        m_new = jnp.maximum(m, jnp.max(st, axis=0, keepdims=True))
        alpha = jnp.exp2(m - m_new)
        p = jnp.exp2(st - m_new)
        m_s[...] = m_new
        l_s[...] = alpha * l_s[...] + jnp.sum(p, axis=0, keepdims=True)
        acc_s[...] = alpha * acc_s[...] + jnp.dot(vt, p.astype(BF16), preferred_element_type=F32)

    m_s[...] = jnp.full(m_s.shape, NEG, F32)
    l_s[...] = jnp.zeros(l_s.shape, F32)
    acc_s[...] = jnp.zeros(acc_s.shape, F32)
    s0[...] = scores(0)

    bufs = (s0, s1)
    unroll = math.gcd(MLA_UNROLL, n)

    def body(i, carry):
        c0 = unroll * i
        for u in range(unroll):
            bufs[(u + 1) % 2][...] = scores(jnp.minimum(c0 + u + 1, n - 1))
            update(c0 + u, bufs[u % 2][...])
        return carry

    lax.fori_loop(0, n // unroll, body, 0)
    g = g_ref[...].astype(F32)
    o_ref[...] = ((acc_s[...] * (1.0 / l_s[...])).T * (g * jax.nn.sigmoid(g))).astype(o_ref.dtype)


def _mla_attention(q_all, k_all, vt_all, proj, batch, seq, tq, tk):
    nq = seq // tq
    n = seq // tk
    assert n % 2 == 0 and MLA_UNROLL % 2 == 0
    return pl.pallas_call(
        functools.partial(_mla_attn_kernel, tk=tk),
        grid=(batch, MLA_HEADS, nq),
        in_specs=[pl.BlockSpec((tq, MLA_QK_PAD), lambda b, h, i: (b * nq + i, h)),
                  pl.BlockSpec((seq, MLA_QK_PAD), lambda b, h, i: (b, h)),
                  pl.BlockSpec((MLA_V, seq), lambda b, h, i: (h, b)),
                  pl.BlockSpec((tq, LANES), lambda b, h, i: (b * nq + i, C0_GB + h))],
        out_specs=pl.BlockSpec((tq, MLA_V), lambda b, h, i: (b * nq + i, h)),
        out_shape=jax.ShapeDtypeStruct((batch * seq, WB), BF16),
        scratch_shapes=[pltpu.VMEM((tk, tq), F32), pltpu.VMEM((tk, tq), F32), pltpu.VMEM((1, tq), F32),
                        pltpu.VMEM((1, tq), F32), pltpu.VMEM((MLA_V, tq), F32)],
        compiler_params=_params(3),
    )(q_all, k_all, vt_all, proj)


OUT_LN_SPLIT = 2


def _out_ln_kernel(y1_ref, y2_ref, w_ref, x_ref, g_ref, b_ref, o_ref, ob_ref):
    half = y1_ref.shape[1]
    tm = y1_ref.shape[0]
    sub = tm // OUT_LN_SPLIT

    def project(rows):
        y = jnp.dot(y1_ref[rows, :], w_ref[:half, :], preferred_element_type=F32)
        return y + jnp.dot(y2_ref[rows, :], w_ref[half:, :], preferred_element_type=F32)

    def norm(rows, y):
        z = ALPHA * x_ref[rows, :] + y
        mu = jnp.mean(z, axis=1, keepdims=True)
        zc = z - mu
        var = jnp.mean(zc * zc, axis=1, keepdims=True)
        out = zc * lax.rsqrt(var + LN_EPS) * g_ref[...] + b_ref[...]
        o_ref[rows, :] = out
        ob_ref[rows, :] = out.astype(ob_ref.dtype)

    rows = [slice(i * sub, (i + 1) * sub) for i in range(OUT_LN_SPLIT)]
    y_prev = project(rows[0])
    for i in range(1, OUT_LN_SPLIT):
        y_next = project(rows[i])
        norm(rows[i - 1], y_prev)
        y_prev = y_next
    norm(rows[-1], y_prev)


def _out_ln(y1, y1_col, y2, y2_col, w, x, g, b, tm):
    m, d = x.shape
    half = w.shape[0] // 2
    full = lambda a: pl.BlockSpec(a.shape, lambda i: (0,) * a.ndim)
    return pl.pallas_call(
        _out_ln_kernel,
        grid=(m // tm,),
        in_specs=[pl.BlockSpec((tm, half), lambda i: (i, y1_col)),
                  pl.BlockSpec((tm, half), lambda i: (i, y2_col)),
                  full(w), pl.BlockSpec((tm, d), lambda i: (i, 0)), full(g), full(b)],
        out_specs=[pl.BlockSpec((tm, d), lambda i: (i, 0)), pl.BlockSpec((tm, d), lambda i: (i, 0))],
        out_shape=[jax.ShapeDtypeStruct((m, d), F32), jax.ShapeDtypeStruct((m, d), BF16)],
        compiler_params=_params(1),
    )(y1, y2, w, x, g, b)


DIL_DILS = tuple(d for _, d in DIL_PAIRS)
DIL_SIDE = DIL_PAIRS[0][0] // 2 // DIL_PAIRS[0][1]
DIL_TQ = 128
DIL_TK = DIL_TQ + 2 * DIL_SIDE
DIL_SB = DIL_TQ * DIL_DILS[-1]
assert all(w // 2 // d == DIL_SIDE for w, d in DIL_PAIRS)


def _dil_kernel(*refs, nsb):
    q_refs = refs[0:3]
    k_refs = [refs[3 + 3 * p:6 + 3 * p] for p in range(3)]
    v_refs = [refs[12 + 3 * p:15 + 3 * p] for p in range(3)]
    g_ref, o_ref, bias_s, o_s, l_s = refs[21:]
    sb = pl.program_id(2)
    dn = (((1,), (1,)), ((), ()))

    qi = lax.broadcasted_iota(jnp.int32, (DIL_TQ, DIL_TK), 0)
    kj = lax.broadcasted_iota(jnp.int32, (DIL_TQ, DIL_TK), 1)
    base = jnp.where((kj >= qi) & (kj <= qi + 2 * DIL_SIDE), 0.0, NEG).astype(F32)
    lo = jnp.where(kj < DIL_SIDE, NEG, 0.0) * (sb == 0).astype(F32)
    hi = jnp.where(kj >= DIL_TK - DIL_SIDE, NEG, 0.0) * (sb == nsb - 1).astype(F32)
    bias_s[0] = base
    bias_s[1] = base + lo
    bias_s[2] = base + hi
    bias_s[3] = base + lo + hi

    def window(trio, lanes, c, n):
        main, prev, nxt = trio
        if n == 1:
            return jnp.concatenate([prev[:, lanes], main[:, lanes], nxt[:, lanes]], axis=0), 3
        if c == 0:
            return jnp.concatenate([prev[:, lanes], main[0:DIL_TK - DIL_SIDE, lanes]], axis=0), 1
        if c == n - 1:
            return jnp.concatenate([main[n * DIL_TQ - (DIL_TK - DIL_SIDE):n * DIL_TQ, lanes],
                                    nxt[:, lanes]], axis=0), 2
        return main[c * DIL_TQ - DIL_SIDE:c * DIL_TQ + DIL_TQ + DIL_SIDE, lanes], 0

    def scores(t):
        pair, r, c, n = t
        lanes = slice(r * LANES, (r + 1) * LANES)
        kw, mask_id = window(k_refs[pair], lanes, c, n)
        q = q_refs[pair][c * DIL_TQ:(c + 1) * DIL_TQ, lanes]
        return lax.dot_general(q, kw, dn, preferred_element_type=F32), mask_id

    def finish(t, s, mask_id):
        pair, r, c, n = t
        dil = DIL_DILS[pair]
        vw, _ = window(v_refs[pair], slice(r * LANES, (r + 1) * LANES), c, n)
        s = s + bias_s[mask_id]
        m = jnp.max(s, axis=1, keepdims=True)
        p = jnp.exp2(s - m)
        l = jnp.sum(p, axis=1, keepdims=True)
        o = jnp.dot(p.astype(BF16), vw, preferred_element_type=F32) * (1.0 / l)
        lse = jnp.broadcast_to(m + jnp.log2(l), (DIL_TQ, LANES))
        start = (c * DIL_TQ) * dil + r
        if dil > 1:
            rows = pl.ds(start, DIL_TQ, stride=dil)
            o_s[pair - 1, rows, :] = o
            l_s[pair - 1, rows, :] = lse
            return
        rows = pl.ds(start, DIL_TQ)
        lses = [lse] + [l_s[i, rows, :] for i in range(len(DIL_DILS) - 1)]
        outs = [o] + [o_s[i, rows, :] for i in range(len(DIL_DILS) - 1)]
        top = functools.reduce(jnp.maximum, lses)
        w = [jnp.exp2(x - top) for x in lses]
        num = sum(wi * oi for wi, oi in zip(w, outs))
        g = g_ref[rows, :].astype(F32)
        o_ref[rows, :] = (num / sum(w) * (g * jax.nn.sigmoid(g))).astype(o_ref.dtype)

    for pair in reversed(range(len(DIL_DILS))):
        n = DIL_SB // DIL_DILS[pair] // DIL_TQ
        for r in range(DIL_DILS[pair]):
            for c in range(n):
                t = (pair, r, c, n)
                finish(t, *scores(t))


def _dil_attention(qk_views, v_views, gate, batch, seq):
    tokens = gate.shape[0]
    nsb = seq // DIL_SB
    halo = DIL_SIDE
    specs_main, specs_halo = [], []
    for dil in DIL_DILS:
        rows, width = tokens // dil, dil * LANES
        main_rows = DIL_SB // dil
        per_main = main_rows // halo
        last = rows // halo - 1

        def main_spec(h0, main_rows=main_rows, width=width):
            return pl.BlockSpec((None, main_rows, width), lambda b, h, s: (h0 + h, b * nsb + s, 0))

        def prev_spec(h0, per_main=per_main, width=width):
            return pl.BlockSpec((None, halo, width),
                                lambda b, h, s: (h0 + h, jnp.maximum((b * nsb + s) * per_main - 1, 0), 0))

        def next_spec(h0, per_main=per_main, width=width, last=last):
            return pl.BlockSpec((None, halo, width),
                                lambda b, h, s: (h0 + h, jnp.minimum((b * nsb + s + 1) * per_main, last), 0))

        specs_main.append(main_spec)
        specs_halo.append((prev_spec, next_spec))
    in_specs = [specs_main[p](0) for p in range(3)]
    operands = list(qk_views)
    for h0, views in ((DIL_HEADS, qk_views), (0, v_views)):
        for p in range(3):
            in_specs += [specs_main[p](h0), specs_halo[p][0](h0), specs_halo[p][1](h0)]
            operands += [views[p]] * 3
    in_specs.append(pl.BlockSpec((DIL_SB, LANES), lambda b, h, s: (b * nsb + s, h)))
    operands.append(gate)
    return pl.pallas_call(
        functools.partial(_dil_kernel, nsb=nsb),
        grid=(batch, DIL_HEADS, nsb),
        in_specs=in_specs,
        out_specs=pl.BlockSpec((DIL_SB, LANES), lambda b, h, s: (b * nsb + s, h)),
        out_shape=jax.ShapeDtypeStruct((tokens, WC), BF16),
        scratch_shapes=[pltpu.VMEM((4, DIL_TQ, DIL_TK), F32),
                        pltpu.VMEM((len(DIL_DILS) - 1, DIL_SB, LANES), F32),
                        pltpu.VMEM((len(DIL_DILS) - 1, DIL_SB, LANES), F32)],
        compiler_params=_params(3),
    )(*operands)


def _rope_tables(seq, half):
    inv = ROPE_THETA ** (-jnp.arange(half, dtype=F32) / half)
    ang = jnp.arange(seq, dtype=F32)[:, None] * inv[None, :]
    cos, sin = jnp.cos(ang), jnp.sin(ang)
    pad = jnp.zeros((seq, LANES - 2 * half), F32)
    return jnp.concatenate([cos, cos, pad], 1), jnp.concatenate([-sin, sin, pad], 1)


def _prep_weights(ab_w_in, ab_w_q_up, ab_w_kv_up, ab_w_out, c_w_in, c_w_out):
    qa, ka, va, ga, q_lat, kv_lat, k_rope, gb = jnp.split(
        ab_w_in, [WA, 2 * WA, 3 * WA, 4 * WA, 4 * WA + MLA_Q_RANK,
                  4 * WA + MLA_Q_RANK + MLA_KV_RANK, 4 * WA + MLA_Q_RANK + MLA_KV_RANK + MLA_ROPE], axis=1)
    w_in0 = (jnp.concatenate(
        [qa * (HEAD_DIM ** -0.5 * LOG2E), ka, ga, q_lat, kv_lat, gb, k_rope,
         jnp.zeros((D_MODEL, IN0_PAD_WIDTH - IN0_WIDTH), F32)], axis=1).astype(BF16),
        va.T.astype(BF16))
    wq = (ab_w_q_up * ((MLA_NOPE + MLA_ROPE) ** -0.5 * LOG2E)).reshape(
        MLA_Q_RANK, MLA_HEADS, MLA_NOPE + MLA_ROPE)
    wq = jnp.pad(wq, ((0, 0), (0, 0), (0, MLA_QK_PAD - MLA_NOPE - MLA_ROPE)))
    w_q = wq.reshape(MLA_Q_RANK, MLA_HEADS * MLA_QK_PAD).astype(BF16)
    w_kv = ab_w_kv_up.astype(BF16)
    qc, kc, vc, gc = (c_w_in[:, i * WC:(i + 1) * WC] for i in range(4))
    w_qk1 = jnp.concatenate([qc * (HEAD_DIM ** -0.5 * LOG2E), kc], axis=1).astype(BF16)
    w_in1 = (w_qk1, vc.astype(BF16), gc.astype(BF16))
    return w_in0, w_q, w_kv, ab_w_out.astype(BF16), w_in1, c_w_out.astype(BF16)


def _trunk(x, w, tabs):
    batch, seq, _ = x.shape
    (w_in0, w_q, w_kv, w_out0, w_in1, w_out1, na_bias, q_g, kv_g,
     ln0_g, ln0_b, ln1_g, ln1_b) = w
    cos64, sin64, cos128, sin128 = tabs
    x2 = x.reshape(batch * seq, D_MODEL)
    proj0 = _proj_f32(x2, w_in0[0], tm=1024, tn=IN0_TN)
    vt_na = _proj_t(w_in0[1], x2, batch, seq, tm=1024)
    ya = _na_attention(proj0, vt_na, na_bias, batch, seq)
    q_all, k_all, vt_all = _mla_up(proj0, q_g, w_q, kv_g, w_kv, cos64, sin64, seq, tm=512)
    yb = _mla_attention(q_all, k_all, vt_all, proj0, batch, seq, tq=1024, tk=1024)
    x1, x1b = _out_ln(ya, 0, yb, 0, w_out0, x2, ln0_g, ln0_b, tm=512)
    qk1 = _proj_heads(x1b, w_in1[0], cos128, sin128, seq, tm=1024, tn=1024, rope=True, dils=DIL_DILS)
    v1 = _proj_heads(x1b, w_in1[1], cos128, sin128, seq, tm=1024, tn=1024, rope=False, dils=DIL_DILS)
    gate1 = _proj_bf16(x1b, w_in1[2], tm=1024, tn=1024)
    yc = _dil_attention(qk1, v1, gate1, batch, seq)
    y, _ = _out_ln(yc, 0, yc, 1, w_out1, x1, ln1_g, ln1_b, tm=512)
    return y.reshape(batch, seq, D_MODEL)


def kernel(x_prompt, x_sample, ab_w_in, ab_rpb, ab_q_norm_g, ab_w_q_up, ab_kv_norm_g, ab_w_kv_up,
           ab_w_out, ab_ln_g, ab_ln_b, c_w_in, c_w_out, c_ln_g, c_ln_b):
    w_in0, w_q, w_kv, w_out0, w_in1, w_out1 = _prep_weights(
        ab_w_in, ab_w_q_up, ab_w_kv_up, ab_w_out, c_w_in, c_w_out)
    row = lambda a: a.reshape(1, -1).astype(F32)
    w = (w_in0, w_q, w_kv, w_out0, w_in1, w_out1, _na_bias(ab_rpb), row(ab_q_norm_g), row(ab_kv_norm_g),
         row(ab_ln_g), row(ab_ln_b), row(c_ln_g), row(c_ln_b))
    max_seq = max(x_prompt.shape[1], x_sample.shape[1])
    tabs = _rope_tables(max_seq, MLA_ROPE // 2) + _rope_tables(max_seq, HEAD_DIM // 2)
    return tuple(_trunk(x, w, tabs) for x in (x_prompt, x_sample))
```

```python
import functools
import math

import jax
import jax.numpy as jnp
from jax import lax
from jax.experimental import pallas as pl
from jax.experimental.pallas import tpu as pltpu

F32 = jnp.float32
BF16 = jnp.bfloat16

D_MODEL = 2048
DEPTH = 2
GRID_W = 64
HEAD_DIM = 128
NA_HEADS = 8
NA_WIN_ROWS = 8
NA_WIN_COLS = 16
MLA_HEADS = 8
MLA_Q_RANK = 512
MLA_KV_RANK = 512
MLA_NOPE = 128
MLA_ROPE = 64
MLA_V = 128
DIL_HEADS = 16
DIL_PAIRS = ((128, 1), (512, 4), (2048, 16))
DIL_HALF = 1024
WA = NA_HEADS * HEAD_DIM
WB = MLA_HEADS * MLA_V
WC = DIL_HEADS * HEAD_DIM
ROPE_THETA = 10000.0
ALPHA = (2 * DEPTH) ** 0.25
LN_EPS = 1e-5
RMS_EPS = 1e-6
NEG = -1e30
LOG2E = 1.4426950408889634

LANES = 128
MLA_QK_PAD = 2 * LANES
MXU_WIDTH = 256
VMEM_LIMIT = 56 * 1024 * 1024

IN0_TN = WA
IN0_MAIN_BLOCKS = (0, 1, 3, 4)
IN0_TAIL_WIDTH = -(-(WB + MLA_ROPE) // MXU_WIDTH) * MXU_WIDTH
assert MLA_Q_RANK + MLA_KV_RANK == IN0_TN

C0_Q, C0_K, C0_G = 0, 8, 16
C0_QLAT, C0_KVLAT = 6, 7
C0_GB = 0
C0_KROPE = WB // LANES

NA_QROWS = 8
NA_KROWS = 16
NA_TQ = NA_QROWS * GRID_W
NA_TK = NA_KROWS * GRID_W
NA_GROUP = 4


def _params(n_axes):
    return pltpu.CompilerParams(dimension_semantics=("arbitrary",) * n_axes,
                                vmem_limit_bytes=VMEM_LIMIT)


def _proj_f32_kernel(x_ref, w_ref, o_ref, xb_ref):
    @pl.when(pl.program_id(1) == 0)
    def _():
        xb_ref[...] = x_ref[...].astype(BF16)

    o_ref[...] = jnp.dot(xb_ref[...], w_ref[...], preferred_element_type=F32).astype(o_ref.dtype)


def _proj_f32(x, w, tm, tn, w_blocks):
    m, k = x.shape
    n_blocks = len(w_blocks)
    lo, skip = w_blocks[0], [b for b in range(w_blocks[0], w_blocks[-1]) if b not in w_blocks]
    assert len(skip) <= 1 and list(w_blocks) == sorted(w_blocks)

    def w_index(i, j):
        jj = j + lo
        return (0, jj + (jj >= skip[0]) if skip else jj)

    return pl.pallas_call(
        _proj_f32_kernel,
        grid=(m // tm, n_blocks),
        in_specs=[pl.BlockSpec((tm, k), lambda i, j: (i, 0)),
                  pl.BlockSpec((k, tn), w_index)],
        out_specs=[pl.BlockSpec((tm, tn), lambda i, j: (i, j)),
                   pl.BlockSpec((tm, k), lambda i, j: (i, 0))],
        out_shape=[jax.ShapeDtypeStruct((m, n_blocks * tn), BF16), jax.ShapeDtypeStruct((m, k), BF16)],
        compiler_params=_params(2),
    )(x, w)


def _proj_t_kernel(w_ref, x_ref, o_ref):
    dn = (((1,), (1,)), ((), ()))
    o_ref[...] = lax.dot_general(w_ref[...], x_ref[...], dn,
                                 preferred_element_type=F32).astype(o_ref.dtype)


def _proj_t(w_t, x, batch, seq, tm):
    n, k = w_t.shape
    per_seq = seq // tm
    return pl.pallas_call(
        _proj_t_kernel,
        grid=(batch * per_seq,),
        in_specs=[pl.BlockSpec((n, k), lambda i: (0, 0)),
                  pl.BlockSpec((tm, k), lambda i: (i, 0))],
        out_specs=pl.BlockSpec((None, n, tm), lambda i: (i // per_seq, 0, i % per_seq)),
        out_shape=jax.ShapeDtypeStruct((batch, n, seq), BF16),
        compiler_params=_params(1),
    )(w_t, x)


def _proj_bf16_kernel(x_ref, w_ref, o_ref):
    o_ref[...] = jnp.dot(x_ref[...], w_ref[...], preferred_element_type=F32).astype(o_ref.dtype)


def _proj_bf16(x, w, tm, tn, col0, n):
    m, k = x.shape
    blk0 = col0 // tn
    return pl.pallas_call(
        _proj_bf16_kernel,
        grid=(m // tm, n // tn),
        in_specs=[pl.BlockSpec((tm, k), lambda i, j: (i, 0)),
                  pl.BlockSpec((k, tn), lambda i, j: (0, blk0 + j))],
        out_specs=pl.BlockSpec((tm, tn), lambda i, j: (i, j)),
        out_shape=jax.ShapeDtypeStruct((m, n), BF16),
        compiler_params=_params(2),
    )(x, w)


PROJ_SPLIT = 2


def _proj_heads_kernel(x_ref, w_ref, cos_ref, sin_ref, *rest, rope, dils):
    o_refs, scrs = rest[:len(dils)], rest[len(dils):]
    tg = x_ref.shape[0] // PROJ_SPLIT

    def project(g):
        return jnp.dot(x_ref[g * tg:(g + 1) * tg, :], w_ref[...], preferred_element_type=F32)

    def emit(g, y):
        for h in range(y.shape[1] // LANES):
            yh = y[:, h * LANES:(h + 1) * LANES]
            if rope:
                rows = slice(g * tg, (g + 1) * tg)
                yh = yh * cos_ref[rows, :] + pltpu.roll(yh, LANES // 2, 1) * sin_ref[rows, :]
            o_refs[0][h, g * tg:(g + 1) * tg, :] = yh.astype(o_refs[0].dtype)
            scrs[0][h, 0, g * tg:(g + 1) * tg, :] = yh
            for lvl in range(1, len(dils)):
                d_prev, d = dils[lvl - 1], dils[lvl]
                q, n = d // d_prev, tg // d
                for r_prev in range(d_prev):
                    for a in range(q):
                        part = scrs[lvl - 1][h, r_prev, pl.ds(g * tg // d_prev + a, n, stride=q), :]
                        r = d_prev * a + r_prev
                        o_refs[lvl][h, g * n:(g + 1) * n, r * LANES:(r + 1) * LANES] = (
                            part.astype(o_refs[lvl].dtype))
                        if lvl + 1 < len(dils):
                            scrs[lvl][h, r, g * n:(g + 1) * n, :] = part

    y_prev = project(0)
    for g in range(1, PROJ_SPLIT):
        y_next = project(g)
        emit(g - 1, y_prev)
        y_prev = y_next
    emit(PROJ_SPLIT - 1, y_prev)


def _proj_heads(x, w, col0, n, cos_t, sin_t, seq, tm, tn, rope, dils):
    m, k = x.shape
    blk0 = col0 // tn
    pos_blocks = seq // tm
    hb = tn // LANES
    return pl.pallas_call(
        functools.partial(_proj_heads_kernel, rope=rope, dils=dils),
        grid=(m // tm, n // tn),
        in_specs=[pl.BlockSpec((tm, k), lambda i, j: (i, 0)),
                  pl.BlockSpec((k, tn), lambda i, j: (0, blk0 + j)),
                  pl.BlockSpec((tm, LANES), lambda i, j: (i % pos_blocks, 0)),
                  pl.BlockSpec((tm, LANES), lambda i, j: (i % pos_blocks, 0))],
        out_specs=[pl.BlockSpec((hb, tm // d, d * LANES), lambda i, j: (j, i, 0)) for d in dils],
        out_shape=[jax.ShapeDtypeStruct((n // LANES, m // d, d * LANES), BF16) for d in dils],
        scratch_shapes=[pltpu.VMEM((hb, d, tm // d, LANES), F32) for d in dils[:-1]],
        compiler_params=_params(2),
    )(x, w, cos_t, sin_t)


def _na_bias_kernel(rpb_ref, o_ref, t_ref):
    h = pl.program_id(0)
    n_dc = 2 * NA_WIN_COLS - 1
    kc = lax.broadcasted_iota(jnp.int32, (GRID_W, GRID_W), 0)
    c = lax.broadcasted_iota(jnp.int32, (GRID_W, GRID_W), 1)
    d = kc - c + (NA_WIN_COLS - 1)
    cs = jnp.clip(c - NA_WIN_COLS // 2, 0, GRID_W - NA_WIN_COLS)
    col_ok = (kc >= cs) & (kc < cs + NA_WIN_COLS)
    for dr in range(2 * NA_WIN_ROWS - 1):
        t = jnp.zeros((GRID_W, GRID_W), F32)
        for dd in range(n_dc):
            t = jnp.where(d == dd, rpb_ref[h, dr * n_dc + dd] * LOG2E, t)
        t_ref[dr] = jnp.where(col_ok, t, NEG)
    neg_blk = jnp.full((GRID_W, GRID_W), NEG, F32)
    for typ in range(3):
        for a in range(NA_QROWS):
            if typ == 0:
                lo, dr0 = max(a - 4, 0), 7 - a
            elif typ == 1:
                lo, dr0 = a, 3 - a
            else:
                lo, dr0 = 8 + min(a - 4, 0), -1 - a
            for kb in range(NA_KROWS):
                ok = lo <= kb < lo + NA_WIN_ROWS
                blk = t_ref[kb + dr0] if ok else neg_blk
                o_ref[0, typ, kb * GRID_W:(kb + 1) * GRID_W, a * GRID_W:(a + 1) * GRID_W] = blk


def _na_bias(rpb):
    rpb2 = rpb.reshape(NA_HEADS, -1)
    return pl.pallas_call(
        _na_bias_kernel,
        grid=(NA_HEADS,),
        in_specs=[pl.BlockSpec(memory_space=pltpu.SMEM)],
        out_specs=pl.BlockSpec((1, 3, NA_TK, NA_TQ), lambda h: (h, 0, 0, 0)),
        out_shape=jax.ShapeDtypeStruct((NA_HEADS, 3, NA_TK, NA_TQ), F32),
        scratch_shapes=[pltpu.VMEM((2 * NA_WIN_ROWS - 1, GRID_W, GRID_W), F32)],
        compiler_params=_params(1),
    )(rpb2)


def _na_kernel(q_ref, k0_ref, k1_ref, k2_ref, k3_ref, vt0_ref, vt1_ref, vt2_ref, vt3_ref,
               g_ref, b_ref, o_ref):
    dn = (((1,), (1,)), ((), ()))
    bias = b_ref[0, 0]
    sts = []
    for e in range(q_ref.shape[0]):
        k = jnp.concatenate([r[e] for r in (k0_ref, k1_ref, k2_ref, k3_ref)], axis=0)
        sts.append(lax.dot_general(k, q_ref[e], dn, preferred_element_type=F32) + bias)
    for e, st in enumerate(sts):
        vt = jnp.concatenate([r[e] for r in (vt0_ref, vt1_ref, vt2_ref, vt3_ref)], axis=1)
        m = jnp.max(st, axis=0, keepdims=True)
        p = jnp.exp2(st - m)
        l = jnp.sum(p, axis=0, keepdims=True)
        ot = jnp.dot(vt, p.astype(BF16), preferred_element_type=F32)
        g = g_ref[e].astype(F32)
        o_ref[e] = ((ot * (1.0 / l)).T * (g * jax.nn.sigmoid(g))).astype(o_ref.dtype)


def _na_attention(proj, vt_na, bias, batch, seq):
    rows = seq // GRID_W
    nj = rows // NA_QROWS
    kq = NA_TK // 4
    proj3 = proj.reshape(batch, seq, proj.shape[1])
    group = math.gcd(batch, NA_GROUP)

    def kbase(j):
        return jnp.clip(2 * j - 1, 0, rows // 4 - 4)

    def k_spec(part):
        return pl.BlockSpec((group, kq, LANES), lambda h, j, b: (b, kbase(j) + part, C0_K + h))

    def vt_spec(part):
        return pl.BlockSpec((group, HEAD_DIM, kq), lambda h, j, b: (b, h, kbase(j) + part))

    def btype(j):
        return jnp.where(j == 0, 0, jnp.where(j == nj - 1, 2, 1))

    tok_spec = lambda col0: pl.BlockSpec((group, NA_TQ, LANES), lambda h, j, b: (b, j, col0 + h))
    in_specs = ([tok_spec(C0_Q)] + [k_spec(i) for i in range(4)] + [vt_spec(i) for i in range(4)]
                + [tok_spec(C0_G)]
                + [pl.BlockSpec((1, 1, NA_TK, NA_TQ), lambda h, j, b: (h, btype(j), 0, 0))])
    out = pl.pallas_call(
        _na_kernel,
        grid=(NA_HEADS, nj, batch // group),
        in_specs=in_specs,
        out_specs=pl.BlockSpec((group, NA_TQ, LANES), lambda h, j, b: (b, j, h)),
        out_shape=jax.ShapeDtypeStruct((batch, seq, WA), BF16),
        compiler_params=_params(3),
    )(*([proj3] * 5), *([vt_na] * 4), proj3, bias)
    return out.reshape(batch * seq, WA)


def _rms(lat_ref, g_ref):
    x = lat_ref[...].astype(F32)
    ms = jnp.mean(x * x, axis=1, keepdims=True)
    return (x * lax.rsqrt(ms + RMS_EPS) * g_ref[...]).astype(BF16)


def _rope64(t, cos_t, sin_t):
    lane = lax.broadcasted_iota(jnp.int32, t.shape, 1)
    half = MLA_ROPE // 2
    partner = jnp.where(lane < half, pltpu.roll(t, LANES - half, 1), pltpu.roll(t, half, 1))
    return t * cos_t + partner * sin_t


def _mla_q_up_kernel(lat_ref, g_ref, w_ref, cos_ref, sin_ref, o_ref):
    y = jnp.dot(_rms(lat_ref, g_ref), w_ref[...], preferred_element_type=F32)
    c = cos_ref[...]
    s = sin_ref[...]
    for h in range(MLA_HEADS):
        base = h * MLA_QK_PAD
        o_ref[:, base:base + LANES] = y[:, base:base + LANES].astype(o_ref.dtype)
        o_ref[:, base + LANES:base + 2 * LANES] = _rope64(
            y[:, base + LANES:base + 2 * LANES], c, s).astype(o_ref.dtype)


def _mla_kv_up_kernel(lat_ref, g_ref, w_ref, kr_ref, cos_ref, sin_ref, k_ref, vt_ref):
    y = jnp.dot(_rms(lat_ref, g_ref), w_ref[...], preferred_element_type=F32)
    k_pe = _rope64(kr_ref[...].astype(F32), cos_ref[...], sin_ref[...]).astype(k_ref.dtype)
    for h in range(MLA_HEADS):
        base = h * (MLA_NOPE + MLA_V)
        k_ref[:, h * MLA_QK_PAD:h * MLA_QK_PAD + LANES] = y[:, base:base + MLA_NOPE].astype(k_ref.dtype)
        k_ref[:, h * MLA_QK_PAD + LANES:(h + 1) * MLA_QK_PAD] = k_pe
        vt_ref[h * MLA_V:(h + 1) * MLA_V, :] = y[:, base + MLA_NOPE:base + MLA_NOPE + MLA_V].T.astype(vt_ref.dtype)


def _mla_up(proj, tail, q_g, w_q, kv_g, w_kv, cos_t, sin_t, seq, tm):
    m = proj.shape[0]
    pos_blocks = seq // tm
    tab = pl.BlockSpec((tm, LANES), lambda i: (i % pos_blocks, 0))
    full = lambda a: pl.BlockSpec(a.shape, lambda i: (0,) * a.ndim)
    q_all = pl.pallas_call(
        _mla_q_up_kernel,
        grid=(m // tm,),
        in_specs=[pl.BlockSpec((tm, MLA_Q_RANK), lambda i: (i, C0_QLAT)), full(q_g), full(w_q), tab, tab],
        out_specs=pl.BlockSpec((tm, MLA_HEADS * MLA_QK_PAD), lambda i: (i, 0)),
        out_shape=jax.ShapeDtypeStruct((m, MLA_HEADS * MLA_QK_PAD), BF16),
        compiler_params=_params(1),
    )(proj, q_g, w_q, cos_t, sin_t)
    k_all, vt_all = pl.pallas_call(
        _mla_kv_up_kernel,
        grid=(m // tm,),
        in_specs=[pl.BlockSpec((tm, MLA_KV_RANK), lambda i: (i, C0_KVLAT)), full(kv_g), full(w_kv),
                  pl.BlockSpec((tm, LANES), lambda i: (i, C0_KROPE)), tab, tab],
        out_specs=[pl.BlockSpec((tm, MLA_HEADS * MLA_QK_PAD), lambda i: (i, 0)),
                   pl.BlockSpec((WB, tm), lambda i: (0, i))],
        out_shape=[jax.ShapeDtypeStruct((m, MLA_HEADS * MLA_QK_PAD), BF16),
                   jax.ShapeDtypeStruct((WB, m), BF16)],
        compiler_params=_params(1),
    )(proj, kv_g, w_kv, tail, cos_t, sin_t)
    return q_all, k_all, vt_all


MLA_UNROLL = 8


def _mla_attn_kernel(q_ref, k_ref, vt_ref, g_ref, o_ref, s0, s1, m_s, l_s, acc_s, *, tk):
    q = q_ref[...]
    tq = q.shape[0]
    n = k_ref.shape[0] // tk
    dn = (((1,), (1,)), ((), ()))

    def scores(c):
        start = pl.multiple_of(c * tk, tk)
        return lax.dot_general(k_ref[pl.ds(start, tk), :], q, dn, preferred_element_type=F32)

    def update(c, st):
        vt = vt_ref[:, pl.ds(pl.multiple_of(c * tk, tk), tk)]
        m = m_s[...]
        m_new = jnp.maximum(m, jnp.max(st, axis=0, keepdims=True))
        alpha = jnp.exp2(m - m_new)
        p = jnp.exp2(st - m_new)
        m_s[...] = m_new
        l_s[...] = alpha * l_s[...] + jnp.sum(p, axis=0, keepdims=True)
        acc_s[...] = alpha * acc_s[...] + jnp.dot(vt, p.astype(BF16), preferred_element_type=F32)

    m_s[...] = jnp.full(m_s.shape, NEG, F32)
    l_s[...] = jnp.zeros(l_s.shape, F32)
    acc_s[...] = jnp.zeros(acc_s.shape, F32)
    s0[...] = scores(0)

    bufs = (s0, s1)
    unroll = math.gcd(MLA_UNROLL, n)

    def body(i, carry):
        c0 = unroll * i
        for u in range(unroll):
            bufs[(u + 1) % 2][...] = scores(jnp.minimum(c0 + u + 1, n - 1))
            update(c0 + u, bufs[u % 2][...])
        return carry

    lax.fori_loop(0, n // unroll, body, 0)
    g = g_ref[...].astype(F32)
    o_ref[...] = ((acc_s[...] * (1.0 / l_s[...])).T * (g * jax.nn.sigmoid(g))).astype(o_ref.dtype)


def _mla_attention(q_all, k_all, vt_all, proj, batch, seq, tq, tk):
    nq = seq // tq
    n = seq // tk
    assert n % 2 == 0 and MLA_UNROLL % 2 == 0
    return pl.pallas_call(
        functools.partial(_mla_attn_kernel, tk=tk),
        grid=(batch, MLA_HEADS, nq),
        in_specs=[pl.BlockSpec((tq, MLA_QK_PAD), lambda b, h, i: (b * nq + i, h)),
                  pl.BlockSpec((seq, MLA_QK_PAD), lambda b, h, i: (b, h)),
                  pl.BlockSpec((MLA_V, seq), lambda b, h, i: (h, b)),
                  pl.BlockSpec((tq, LANES), lambda b, h, i: (b * nq + i, C0_GB + h))],
        out_specs=pl.BlockSpec((tq, MLA_V), lambda b, h, i: (b * nq + i, h)),
        out_shape=jax.ShapeDtypeStruct((batch * seq, WB), BF16),
        scratch_shapes=[pltpu.VMEM((tk, tq), F32), pltpu.VMEM((tk, tq), F32), pltpu.VMEM((1, tq), F32),
                        pltpu.VMEM((1, tq), F32), pltpu.VMEM((MLA_V, tq), F32)],
        compiler_params=_params(3),
    )(q_all, k_all, vt_all, proj)


OUT_LN_SPLIT = 2


def _out_ln_kernel(y1_ref, y2_ref, w_ref, x_ref, g_ref, b_ref, o_ref, ob_ref):
    half = y1_ref.shape[1]
    tm = y1_ref.shape[0]
    sub = tm // OUT_LN_SPLIT

    def project(rows):
        y = jnp.dot(y1_ref[rows, :], w_ref[:half, :], preferred_element_type=F32)
        return y + jnp.dot(y2_ref[rows, :], w_ref[half:, :], preferred_element_type=F32)

    def norm(rows, y):
        z = ALPHA * x_ref[rows, :] + y
        mu = jnp.mean(z, axis=1, keepdims=True)
        zc = z - mu
        var = jnp.mean(zc * zc, axis=1, keepdims=True)
        out = zc * lax.rsqrt(var + LN_EPS) * g_ref[...] + b_ref[...]
        o_ref[rows, :] = out
        ob_ref[rows, :] = out.astype(ob_ref.dtype)

    rows = [slice(i * sub, (i + 1) * sub) for i in range(OUT_LN_SPLIT)]
    y_prev = project(rows[0])
    for i in range(1, OUT_LN_SPLIT):
        y_next = project(rows[i])
        norm(rows[i - 1], y_prev)
        y_prev = y_next
    norm(rows[-1], y_prev)


def _out_ln(y1, y1_col, y2, y2_col, w, x, g, b, tm):
    m, d = x.shape
    half = w.shape[0] // 2
    full = lambda a: pl.BlockSpec(a.shape, lambda i: (0,) * a.ndim)
    return pl.pallas_call(
        _out_ln_kernel,
        grid=(m // tm,),
        in_specs=[pl.BlockSpec((tm, half), lambda i: (i, y1_col)),
                  pl.BlockSpec((tm, half), lambda i: (i, y2_col)),
                  full(w), pl.BlockSpec((tm, d), lambda i: (i, 0)), full(g), full(b)],
        out_specs=[pl.BlockSpec((tm, d), lambda i: (i, 0)), pl.BlockSpec((tm, d), lambda i: (i, 0))],
        out_shape=[jax.ShapeDtypeStruct((m, d), F32), jax.ShapeDtypeStruct((m, d), BF16)],
        compiler_params=_params(1),
    )(y1, y2, w, x, g, b)


DIL_DILS = tuple(d for _, d in DIL_PAIRS)
DIL_SIDE = DIL_PAIRS[0][0] // 2 // DIL_PAIRS[0][1]
DIL_TQ = 128
DIL_TK = DIL_TQ + 2 * DIL_SIDE
DIL_SB = DIL_TQ * DIL_DILS[-1]
assert all(w // 2 // d == DIL_SIDE for w, d in DIL_PAIRS)


def _dil_kernel(*refs, nsb):
    q_refs = refs[0:3]
    k_refs = [refs[3 + 3 * p:6 + 3 * p] for p in range(3)]
    v_refs = [refs[12 + 3 * p:15 + 3 * p] for p in range(3)]
    g_ref, o_ref, bias_s, o_s, l_s = refs[21:]
    sb = pl.program_id(2)
    dn = (((1,), (1,)), ((), ()))

    qi = lax.broadcasted_iota(jnp.int32, (DIL_TQ, DIL_TK), 0)
    kj = lax.broadcasted_iota(jnp.int32, (DIL_TQ, DIL_TK), 1)
    base = jnp.where((kj >= qi) & (kj <= qi + 2 * DIL_SIDE), 0.0, NEG).astype(F32)
    lo = jnp.where(kj < DIL_SIDE, NEG, 0.0) * (sb == 0).astype(F32)
    hi = jnp.where(kj >= DIL_TK - DIL_SIDE, NEG, 0.0) * (sb == nsb - 1).astype(F32)
    bias_s[0] = base
    bias_s[1] = base + lo
    bias_s[2] = base + hi
    bias_s[3] = base + lo + hi

    def window(trio, lanes, c, n):
        main, prev, nxt = trio
        if n == 1:
            return jnp.concatenate([prev[:, lanes], main[:, lanes], nxt[:, lanes]], axis=0), 3
        if c == 0:
            return jnp.concatenate([prev[:, lanes], main[0:DIL_TK - DIL_SIDE, lanes]], axis=0), 1
        if c == n - 1:
            return jnp.concatenate([main[n * DIL_TQ - (DIL_TK - DIL_SIDE):n * DIL_TQ, lanes],
                                    nxt[:, lanes]], axis=0), 2
        return main[c * DIL_TQ - DIL_SIDE:c * DIL_TQ + DIL_TQ + DIL_SIDE, lanes], 0

    def scores(t):
        pair, r, c, n = t
        lanes = slice(r * LANES, (r + 1) * LANES)
        kw, mask_id = window(k_refs[pair], lanes, c, n)
        q = q_refs[pair][c * DIL_TQ:(c + 1) * DIL_TQ, lanes]
        return lax.dot_general(q, kw, dn, preferred_element_type=F32), mask_id

    def finish(t, s, mask_id):
        pair, r, c, n = t
        dil = DIL_DILS[pair]
        vw, _ = window(v_refs[pair], slice(r * LANES, (r + 1) * LANES), c, n)
        s = s + bias_s[mask_id]
        m = jnp.max(s, axis=1, keepdims=True)
        p = jnp.exp2(s - m)
        l = jnp.sum(p, axis=1, keepdims=True)
        o = jnp.dot(p.astype(BF16), vw, preferred_element_type=F32) * (1.0 / l)
        lse = jnp.broadcast_to(m + jnp.log2(l), (DIL_TQ, LANES))
        start = (c * DIL_TQ) * dil + r
        if dil > 1:
            rows = pl.ds(start, DIL_TQ, stride=dil)
            o_s[pair - 1, rows, :] = o
            l_s[pair - 1, rows, :] = lse
            return
        rows = pl.ds(start, DIL_TQ)
        lses = [lse] + [l_s[i, rows, :] for i in range(len(DIL_DILS) - 1)]
        outs = [o] + [o_s[i, rows, :] for i in range(len(DIL_DILS) - 1)]
        top = functools.reduce(jnp.maximum, lses)
        w = [jnp.exp2(x - top) for x in lses]
        num = sum(wi * oi for wi, oi in zip(w, outs))
        g = g_ref[rows, :].astype(F32)
        o_ref[rows, :] = (num / sum(w) * (g * jax.nn.sigmoid(g))).astype(o_ref.dtype)

    for pair in reversed(range(len(DIL_DILS))):
        n = DIL_SB // DIL_DILS[pair] // DIL_TQ
        for r in range(DIL_DILS[pair]):
            for c in range(n):
                t = (pair, r, c, n)
                finish(t, *scores(t))


def _dil_attention(qk_views, v_views, gate, batch, seq):
    tokens = gate.shape[0]
    nsb = seq // DIL_SB
    halo = DIL_SIDE
    specs_main, specs_halo = [], []
    for dil in DIL_DILS:
        rows, width = tokens // dil, dil * LANES
        main_rows = DIL_SB // dil
        per_main = main_rows // halo
        last = rows // halo - 1

        def main_spec(h0, main_rows=main_rows, width=width):
            return pl.BlockSpec((None, main_rows, width), lambda b, h, s: (h0 + h, b * nsb + s, 0))

        def prev_spec(h0, per_main=per_main, width=width):
            return pl.BlockSpec((None, halo, width),
                                lambda b, h, s: (h0 + h, jnp.maximum((b * nsb + s) * per_main - 1, 0), 0))

        def next_spec(h0, per_main=per_main, width=width, last=last):
            return pl.BlockSpec((None, halo, width),
                                lambda b, h, s: (h0 + h, jnp.minimum((b * nsb + s + 1) * per_main, last), 0))

        specs_main.append(main_spec)
        specs_halo.append((prev_spec, next_spec))
    in_specs = [specs_main[p](0) for p in range(3)]
    operands = list(qk_views)
    for h0, views in ((DIL_HEADS, qk_views), (0, v_views)):
        for p in range(3):
            in_specs += [specs_main[p](h0), specs_halo[p][0](h0), specs_halo[p][1](h0)]
            operands += [views[p]] * 3
    in_specs.append(pl.BlockSpec((DIL_SB, LANES), lambda b, h, s: (b * nsb + s, h)))
    operands.append(gate)
    return pl.pallas_call(
        functools.partial(_dil_kernel, nsb=nsb),
        grid=(batch, DIL_HEADS, nsb),
        in_specs=in_specs,
        out_specs=pl.BlockSpec((DIL_SB, LANES), lambda b, h, s: (b * nsb + s, h)),
        out_shape=jax.ShapeDtypeStruct((tokens, WC), BF16),
        scratch_shapes=[pltpu.VMEM((4, DIL_TQ, DIL_TK), F32),
                        pltpu.VMEM((len(DIL_DILS) - 1, DIL_SB, LANES), F32),
                        pltpu.VMEM((len(DIL_DILS) - 1, DIL_SB, LANES), F32)],
        compiler_params=_params(3),
    )(*operands)


def _rope_tables(seq, half):
    inv = ROPE_THETA ** (-jnp.arange(half, dtype=F32) / half)
    ang = jnp.arange(seq, dtype=F32)[:, None] * inv[None, :]
    cos, sin = jnp.cos(ang), jnp.sin(ang)
    pad = jnp.zeros((seq, LANES - 2 * half), F32)
    return jnp.concatenate([cos, cos, pad], 1), jnp.concatenate([-sin, sin, pad], 1)


def _prep_weights(ab_w_in, ab_w_q_up, ab_w_kv_up, ab_w_out, c_w_in, c_w_out):
    def cast_scaling_q(w, q_cols, scale):
        col = lax.broadcasted_iota(jnp.int32, (1, w.shape[1]), 1)
        return (w * jnp.where(col < q_cols, scale, 1.0)).astype(BF16)

    w0 = cast_scaling_q(ab_w_in, WA, HEAD_DIM ** -0.5 * LOG2E)
    tail0 = 4 * WA + MLA_Q_RANK + MLA_KV_RANK
    w_tail = jnp.concatenate(
        [w0[:, tail0 + MLA_ROPE:], w0[:, tail0:tail0 + MLA_ROPE],
         jnp.zeros((D_MODEL, IN0_TAIL_WIDTH - WB - MLA_ROPE), BF16)], axis=1)
    w_in0 = (w0, w_tail, w0[:, 2 * WA:3 * WA].T)
    wq = (ab_w_q_up * ((MLA_NOPE + MLA_ROPE) ** -0.5 * LOG2E)).reshape(
        MLA_Q_RANK, MLA_HEADS, MLA_NOPE + MLA_ROPE)
    wq = jnp.pad(wq, ((0, 0), (0, 0), (0, MLA_QK_PAD - MLA_NOPE - MLA_ROPE)))
    w_q = wq.reshape(MLA_Q_RANK, MLA_HEADS * MLA_QK_PAD).astype(BF16)
    w_kv = ab_w_kv_up.astype(BF16)
    w_in1 = cast_scaling_q(c_w_in, WC, HEAD_DIM ** -0.5 * LOG2E)
    return w_in0, w_q, w_kv, ab_w_out.astype(BF16), w_in1, c_w_out.astype(BF16)


def _trunk(x, w, tabs):
    batch, seq, _ = x.shape
    (w_in0, w_q, w_kv, w_out0, w_in1, w_out1, na_bias, q_g, kv_g,
     ln0_g, ln0_b, ln1_g, ln1_b) = w
    cos64, sin64, cos128, sin128 = tabs
    x2 = x.reshape(batch * seq, D_MODEL)
    w0, w_tail, w_va_t = w_in0
    proj0, xb = _proj_f32(x2, w0, tm=1024, tn=IN0_TN, w_blocks=IN0_MAIN_BLOCKS)
    tail0 = _proj_bf16(xb, w_tail, tm=1024, tn=IN0_TAIL_WIDTH, col0=0, n=IN0_TAIL_WIDTH)
    vt_na = _proj_t(w_va_t, xb, batch, seq, tm=1024)
    ya = _na_attention(proj0, vt_na, na_bias, batch, seq)
    q_all, k_all, vt_all = _mla_up(proj0, tail0, q_g, w_q, kv_g, w_kv, cos64, sin64, seq, tm=512)
    yb = _mla_attention(q_all, k_all, vt_all, tail0, batch, seq, tq=1024, tk=1024)
    x1, x1b = _out_ln(ya, 0, yb, 0, w_out0, x2, ln0_g, ln0_b, tm=512)
    proj1 = functools.partial(_proj_heads, x1b, w_in1, cos_t=cos128, sin_t=sin128, seq=seq,
                              tm=1024, tn=1024, dils=DIL_DILS)
    qk1 = proj1(col0=0, n=2 * WC, rope=True)
    v1 = proj1(col0=2 * WC, n=WC, rope=False)
    gate1 = _proj_bf16(x1b, w_in1, tm=1024, tn=1024, col0=3 * WC, n=WC)
    yc = _dil_attention(qk1, v1, gate1, batch, seq)
    y, _ = _out_ln(yc, 0, yc, 1, w_out1, x1, ln1_g, ln1_b, tm=512)
    return y.reshape(batch, seq, D_MODEL)


def kernel(x_prompt, x_sample, ab_w_in, ab_rpb, ab_q_norm_g, ab_w_q_up, ab_kv_norm_g, ab_w_kv_up,
           ab_w_out, ab_ln_g, ab_ln_b, c_w_in, c_w_out, c_ln_g, c_ln_b):
    w_in0, w_q, w_kv, w_out0, w_in1, w_out1 = _prep_weights(
        ab_w_in, ab_w_q_up, ab_w_kv_up, ab_w_out, c_w_in, c_w_out)
    row = lambda a: a.reshape(1, -1).astype(F32)
    w = (w_in0, w_q, w_kv, w_out0, w_in1, w_out1, _na_bias(ab_rpb), row(ab_q_norm_g), row(ab_kv_norm_g),
         row(ab_ln_g), row(ab_ln_b), row(c_ln_g), row(c_ln_b))
    max_seq = max(x_prompt.shape[1], x_sample.shape[1])
    tabs = _rope_tables(max_seq, MLA_ROPE // 2) + _rope_tables(max_seq, HEAD_DIM // 2)
    return tuple(_trunk(x, w, tabs) for x in (x_prompt, x_sample))
```

```python
import functools
import math

import jax
import jax.numpy as jnp
from jax import lax
from jax.experimental import pallas as pl
from jax.experimental.pallas import tpu as pltpu

F32 = jnp.float32
BF16 = jnp.bfloat16

D_MODEL = 2048
DEPTH = 2
GRID_W = 64
HEAD_DIM = 128
NA_HEADS = 8
NA_WIN_ROWS = 8
NA_WIN_COLS = 16
MLA_HEADS = 8
MLA_Q_RANK = 512
MLA_KV_RANK = 512
MLA_NOPE = 128
MLA_ROPE = 64
MLA_V = 128
DIL_HEADS = 16
DIL_PAIRS = ((128, 1), (512, 4), (2048, 16))
DIL_HALF = 1024
WA = NA_HEADS * HEAD_DIM
WB = MLA_HEADS * MLA_V
WC = DIL_HEADS * HEAD_DIM
ROPE_THETA = 10000.0
ALPHA = (2 * DEPTH) ** 0.25
LN_EPS = 1e-5
RMS_EPS = 1e-6
NEG = -1e30
LOG2E = 1.4426950408889634

LANES = 128
MLA_QK_PAD = 2 * LANES
MXU_WIDTH = 256
VMEM_LIMIT = 56 * 1024 * 1024

IN0_TN = WA
IN0_MAIN_BLOCKS = (0, 1, 3, 4)
IN0_TAIL_WIDTH = -(-(WB + MLA_ROPE) // MXU_WIDTH) * MXU_WIDTH
assert MLA_Q_RANK + MLA_KV_RANK == IN0_TN

C0_Q, C0_K, C0_G = 0, 8, 16
C0_QLAT, C0_KVLAT = 6, 7
C0_GB = 0
C0_KROPE = WB // LANES

NA_QROWS = 8
NA_KROWS = 16
NA_TQ = NA_QROWS * GRID_W
NA_TK = NA_KROWS * GRID_W
NA_GROUP = 4


def _params(n_axes):
    return pltpu.CompilerParams(dimension_semantics=("arbitrary",) * n_axes,
                                vmem_limit_bytes=VMEM_LIMIT)


def _proj_f32_kernel(x_ref, w_ref, o_ref, xb_ref):
    @pl.when(pl.program_id(1) == 0)
    def _():
        xb_ref[...] = x_ref[...].astype(BF16)

    o_ref[...] = jnp.dot(xb_ref[...], w_ref[...], preferred_element_type=F32).astype(o_ref.dtype)


def _proj_f32(x, w, tm, tn, w_blocks):
    m, k = x.shape
    n_blocks = len(w_blocks)
    lo, skip = w_blocks[0], [b for b in range(w_blocks[0], w_blocks[-1]) if b not in w_blocks]
    assert len(skip) <= 1 and list(w_blocks) == sorted(w_blocks)

    def w_index(i, j):
        jj = j + lo
        return (0, jj + (jj >= skip[0]) if skip else jj)

    return pl.pallas_call(
        _proj_f32_kernel,
        grid=(m // tm, n_blocks),
        in_specs=[pl.BlockSpec((tm, k), lambda i, j: (i, 0)),
                  pl.BlockSpec((k, tn), w_index)],
        out_specs=[pl.BlockSpec((tm, tn), lambda i, j: (i, j)),
                   pl.BlockSpec((tm, k), lambda i, j: (i, 0))],
        out_shape=[jax.ShapeDtypeStruct((m, n_blocks * tn), BF16), jax.ShapeDtypeStruct((m, k), BF16)],
        compiler_params=_params(2),
    )(x, w)


def _proj_t_kernel(w_ref, x_ref, o_ref):
    dn = (((1,), (1,)), ((), ()))
    o_ref[...] = lax.dot_general(w_ref[...], x_ref[...], dn,
                                 preferred_element_type=F32).astype(o_ref.dtype)


def _proj_t(w_t, x, batch, seq, tm):
    n, k = w_t.shape
    per_seq = seq // tm
    return pl.pallas_call(
        _proj_t_kernel,
        grid=(batch * per_seq,),
        in_specs=[pl.BlockSpec((n, k), lambda i: (0, 0)),
                  pl.BlockSpec((tm, k), lambda i: (i, 0))],
        out_specs=pl.BlockSpec((None, n, tm), lambda i: (i // per_seq, 0, i % per_seq)),
        out_shape=jax.ShapeDtypeStruct((batch, n, seq), BF16),
        compiler_params=_params(1),
    )(w_t, x)


def _proj_bf16_kernel(x_ref, w_ref, o_ref):
    o_ref[...] = jnp.dot(x_ref[...], w_ref[...], preferred_element_type=F32).astype(o_ref.dtype)


def _proj_bf16(x, w, tm, tn, col0, n):
    m, k = x.shape
    blk0 = col0 // tn
    return pl.pallas_call(
        _proj_bf16_kernel,
        grid=(m // tm, n // tn),
        in_specs=[pl.BlockSpec((tm, k), lambda i, j: (i, 0)),
                  pl.BlockSpec((k, tn), lambda i, j: (0, blk0 + j))],
        out_specs=pl.BlockSpec((tm, tn), lambda i, j: (i, j)),
        out_shape=jax.ShapeDtypeStruct((m, n), BF16),
        compiler_params=_params(2),
    )(x, w)


PROJ_SPLIT = 4


def _proj_heads_kernel(x_ref, w_ref, cos_ref, sin_ref, *rest, rope, dils):
    o_refs, scrs = rest[:len(dils)], rest[len(dils):]
    tg = x_ref.shape[0] // PROJ_SPLIT

    def project(g):
        return jnp.dot(x_ref[g * tg:(g + 1) * tg, :], w_ref[...], preferred_element_type=F32)

    def emit(g, y):
        for h in range(y.shape[1] // LANES):
            yh = y[:, h * LANES:(h + 1) * LANES]
            if rope:
                rows = slice(g * tg, (g + 1) * tg)
                yh = yh * cos_ref[rows, :] + pltpu.roll(yh, LANES // 2, 1) * sin_ref[rows, :]
            o_refs[0][h, g * tg:(g + 1) * tg, :] = yh.astype(o_refs[0].dtype)
            scrs[0][h, 0, g * tg:(g + 1) * tg, :] = yh
            for lvl in range(1, len(dils)):
                d_prev, d = dils[lvl - 1], dils[lvl]
                q, n = d // d_prev, tg // d
                for r_prev in range(d_prev):
                    for a in range(q):
                        part = scrs[lvl - 1][h, r_prev, pl.ds(g * tg // d_prev + a, n, stride=q), :]
                        r = d_prev * a + r_prev
                        o_refs[lvl][h, g * n:(g + 1) * n, r * LANES:(r + 1) * LANES] = (
                            part.astype(o_refs[lvl].dtype))
                        if lvl + 1 < len(dils):
                            scrs[lvl][h, r, g * n:(g + 1) * n, :] = part

    y_prev = project(0)
    for g in range(1, PROJ_SPLIT):
        y_next = project(g)
        emit(g - 1, y_prev)
        y_prev = y_next
    emit(PROJ_SPLIT - 1, y_prev)


def _proj_heads(x, w, col0, n, cos_t, sin_t, seq, tm, tn, rope, dils):
    m, k = x.shape
    blk0 = col0 // tn
    pos_blocks = seq // tm
    hb = tn // LANES
    return pl.pallas_call(
        functools.partial(_proj_heads_kernel, rope=rope, dils=dils),
        grid=(m // tm, n // tn),
        in_specs=[pl.BlockSpec((tm, k), lambda i, j: (i, 0)),
                  pl.BlockSpec((k, tn), lambda i, j: (0, blk0 + j)),
                  pl.BlockSpec((tm, LANES), lambda i, j: (i % pos_blocks, 0)),
                  pl.BlockSpec((tm, LANES), lambda i, j: (i % pos_blocks, 0))],
        out_specs=[pl.BlockSpec((hb, tm // d, d * LANES), lambda i, j: (j, i, 0)) for d in dils],
        out_shape=[jax.ShapeDtypeStruct((n // LANES, m // d, d * LANES), BF16) for d in dils],
        scratch_shapes=[pltpu.VMEM((hb, d, tm // d, LANES), F32) for d in dils[:-1]],
        compiler_params=_params(2),
    )(x, w, cos_t, sin_t)


def _na_bias_kernel(rpb_ref, o_ref, t_ref):
    h = pl.program_id(0)
    n_dc = 2 * NA_WIN_COLS - 1
    kc = lax.broadcasted_iota(jnp.int32, (GRID_W, GRID_W), 0)
    c = lax.broadcasted_iota(jnp.int32, (GRID_W, GRID_W), 1)
    d = kc - c + (NA_WIN_COLS - 1)
    cs = jnp.clip(c - NA_WIN_COLS // 2, 0, GRID_W - NA_WIN_COLS)
    col_ok = (kc >= cs) & (kc < cs + NA_WIN_COLS)
    for dr in range(2 * NA_WIN_ROWS - 1):
        t = jnp.zeros((GRID_W, GRID_W), F32)
        for dd in range(n_dc):
            t = jnp.where(d == dd, rpb_ref[h, dr * n_dc + dd] * LOG2E, t)
        t_ref[dr] = jnp.where(col_ok, t, NEG)
    neg_blk = jnp.full((GRID_W, GRID_W), NEG, F32)
    for typ in range(3):
        for a in range(NA_QROWS):
            if typ == 0:
                lo, dr0 = max(a - 4, 0), 7 - a
            elif typ == 1:
                lo, dr0 = a, 3 - a
            else:
                lo, dr0 = 8 + min(a - 4, 0), -1 - a
            for kb in range(NA_KROWS):
                ok = lo <= kb < lo + NA_WIN_ROWS
                blk = t_ref[kb + dr0] if ok else neg_blk
                o_ref[0, typ, kb * GRID_W:(kb + 1) * GRID_W, a * GRID_W:(a + 1) * GRID_W] = blk


def _na_bias(rpb):
    rpb2 = rpb.reshape(NA_HEADS, -1)
    return pl.pallas_call(
        _na_bias_kernel,
        grid=(NA_HEADS,),
        in_specs=[pl.BlockSpec(memory_space=pltpu.SMEM)],
        out_specs=pl.BlockSpec((1, 3, NA_TK, NA_TQ), lambda h: (h, 0, 0, 0)),
        out_shape=jax.ShapeDtypeStruct((NA_HEADS, 3, NA_TK, NA_TQ), F32),
        scratch_shapes=[pltpu.VMEM((2 * NA_WIN_ROWS - 1, GRID_W, GRID_W), F32)],
        compiler_params=_params(1),
    )(rpb2)


def _na_kernel(q_ref, k0_ref, k1_ref, k2_ref, k3_ref, vt0_ref, vt1_ref, vt2_ref, vt3_ref,
               g_ref, b_ref, o_ref):
    dn = (((1,), (1,)), ((), ()))
    bias = b_ref[0, 0]
    sts = []
    for e in range(q_ref.shape[0]):
        k = jnp.concatenate([r[e] for r in (k0_ref, k1_ref, k2_ref, k3_ref)], axis=0)
        sts.append(lax.dot_general(k, q_ref[e], dn, preferred_element_type=F32) + bias)
    for e, st in enumerate(sts):
        vt = jnp.concatenate([r[e] for r in (vt0_ref, vt1_ref, vt2_ref, vt3_ref)], axis=1)
        m = jnp.max(st, axis=0, keepdims=True)
        p = jnp.exp2(st - m)
        l = jnp.sum(p, axis=0, keepdims=True)
        ot = jnp.dot(vt, p.astype(BF16), preferred_element_type=F32)
        g = g_ref[e].astype(F32)
        o_ref[e] = ((ot * (1.0 / l)).T * (g * jax.nn.sigmoid(g))).astype(o_ref.dtype)


def _na_attention(proj, vt_na, bias, batch, seq):
    rows = seq // GRID_W
    nj = rows // NA_QROWS
    kq = NA_TK // 4
    proj3 = proj.reshape(batch, seq, proj.shape[1])
    group = math.gcd(batch, NA_GROUP)

    def kbase(j):
        return jnp.clip(2 * j - 1, 0, rows // 4 - 4)

    def k_spec(part):
        return pl.BlockSpec((group, kq, LANES), lambda h, j, b: (b, kbase(j) + part, C0_K + h))

    def vt_spec(part):
        return pl.BlockSpec((group, HEAD_DIM, kq), lambda h, j, b: (b, h, kbase(j) + part))

    def btype(j):
        return jnp.where(j == 0, 0, jnp.where(j == nj - 1, 2, 1))

    tok_spec = lambda col0: pl.BlockSpec((group, NA_TQ, LANES), lambda h, j, b: (b, j, col0 + h))
    in_specs = ([tok_spec(C0_Q)] + [k_spec(i) for i in range(4)] + [vt_spec(i) for i in range(4)]
                + [tok_spec(C0_G)]
                + [pl.BlockSpec((1, 1, NA_TK, NA_TQ), lambda h, j, b: (h, btype(j), 0, 0))])
    out = pl.pallas_call(
        _na_kernel,
        grid=(NA_HEADS, nj, batch // group),
        in_specs=in_specs,
        out_specs=pl.BlockSpec((group, NA_TQ, LANES), lambda h, j, b: (b, j, h)),
        out_shape=jax.ShapeDtypeStruct((batch, seq, WA), BF16),
        compiler_params=_params(3),
    )(*([proj3] * 5), *([vt_na] * 4), proj3, bias)
    return out.reshape(batch * seq, WA)


def _rms(lat_ref, g_ref):
    x = lat_ref[...].astype(F32)
    ms = jnp.mean(x * x, axis=1, keepdims=True)
    return (x * lax.rsqrt(ms + RMS_EPS) * g_ref[...]).astype(BF16)


def _rope64(t, cos_t, sin_t):
    lane = lax.broadcasted_iota(jnp.int32, t.shape, 1)
    half = MLA_ROPE // 2
    partner = jnp.where(lane < half, pltpu.roll(t, LANES - half, 1), pltpu.roll(t, half, 1))
    return t * cos_t + partner * sin_t


def _mla_q_up_kernel(lat_ref, g_ref, w_ref, cos_ref, sin_ref, o_ref):
    y = jnp.dot(_rms(lat_ref, g_ref), w_ref[...], preferred_element_type=F32)
    c = cos_ref[...]
    s = sin_ref[...]
    for h in range(MLA_HEADS):
        base = h * MLA_QK_PAD
        o_ref[:, base:base + LANES] = y[:, base:base + LANES].astype(o_ref.dtype)
        o_ref[:, base + LANES:base + 2 * LANES] = _rope64(
            y[:, base + LANES:base + 2 * LANES], c, s).astype(o_ref.dtype)


def _mla_kv_up_kernel(lat_ref, g_ref, w_ref, kr_ref, cos_ref, sin_ref, k_ref, vt_ref):
    y = jnp.dot(_rms(lat_ref, g_ref), w_ref[...], preferred_element_type=F32)
    k_pe = _rope64(kr_ref[...].astype(F32), cos_ref[...], sin_ref[...]).astype(k_ref.dtype)
    for h in range(MLA_HEADS):
        base = h * (MLA_NOPE + MLA_V)
        k_ref[:, h * MLA_QK_PAD:h * MLA_QK_PAD + LANES] = y[:, base:base + MLA_NOPE].astype(k_ref.dtype)
        k_ref[:, h * MLA_QK_PAD + LANES:(h + 1) * MLA_QK_PAD] = k_pe
        vt_ref[h * MLA_V:(h + 1) * MLA_V, :] = y[:, base + MLA_NOPE:base + MLA_NOPE + MLA_V].T.astype(vt_ref.dtype)


def _mla_up(proj, tail, q_g, w_q, kv_g, w_kv, cos_t, sin_t, seq, tm):
    m = proj.shape[0]
    pos_blocks = seq // tm
    tab = pl.BlockSpec((tm, LANES), lambda i: (i % pos_blocks, 0))
    full = lambda a: pl.BlockSpec(a.shape, lambda i: (0,) * a.ndim)
    q_all = pl.pallas_call(
        _mla_q_up_kernel,
        grid=(m // tm,),
        in_specs=[pl.BlockSpec((tm, MLA_Q_RANK), lambda i: (i, C0_QLAT)), full(q_g), full(w_q), tab, tab],
        out_specs=pl.BlockSpec((tm, MLA_HEADS * MLA_QK_PAD), lambda i: (i, 0)),
        out_shape=jax.ShapeDtypeStruct((m, MLA_HEADS * MLA_QK_PAD), BF16),
        compiler_params=_params(1),
    )(proj, q_g, w_q, cos_t, sin_t)
    k_all, vt_all = pl.pallas_call(
        _mla_kv_up_kernel,
        grid=(m // tm,),
        in_specs=[pl.BlockSpec((tm, MLA_KV_RANK), lambda i: (i, C0_KVLAT)), full(kv_g), full(w_kv),
                  pl.BlockSpec((tm, LANES), lambda i: (i, C0_KROPE)), tab, tab],
        out_specs=[pl.BlockSpec((tm, MLA_HEADS * MLA_QK_PAD), lambda i: (i, 0)),
                   pl.BlockSpec((WB, tm), lambda i: (0, i))],
        out_shape=[jax.ShapeDtypeStruct((m, MLA_HEADS * MLA_QK_PAD), BF16),
                   jax.ShapeDtypeStruct((WB, m), BF16)],
        compiler_params=_params(1),
    )(proj, kv_g, w_kv, tail, cos_t, sin_t)
    return q_all, k_all, vt_all


MLA_UNROLL = 8


def _mla_attn_kernel(q_ref, k_ref, vt_ref, g_ref, o_ref, s0, s1, m_s, l_s, acc_s, *, tk):
    q = q_ref[...]
    tq = q.shape[0]
    n = k_ref.shape[0] // tk
    dn = (((1,), (1,)), ((), ()))

    def scores(c):
        start = pl.multiple_of(c * tk, tk)
        return lax.dot_general(k_ref[pl.ds(start, tk), :], q, dn, preferred_element_type=F32)

    def update(c, st):
        vt = vt_ref[:, pl.ds(pl.multiple_of(c * tk, tk), tk)]
        m = m_s[...]
        m_new = jnp.maximum(m, jnp.max(st, axis=0, keepdims=True))
        alpha = jnp.exp2(m - m_new)
        p = jnp.exp2(st - m_new)
        m_s[...] = m_new
        l_s[...] = alpha * l_s[...] + jnp.sum(p, axis=0, keepdims=True)
        acc_s[...] = alpha * acc_s[...] + jnp.dot(vt, p.astype(BF16), preferred_element_type=F32)

    m_s[...] = jnp.full(m_s.shape, NEG, F32)
    l_s[...] = jnp.zeros(l_s.shape, F32)
    acc_s[...] = jnp.zeros(acc_s.shape, F32)
    s0[...] = scores(0)

    bufs = (s0, s1)
    unroll = math.gcd(MLA_UNROLL, n)

    def body(i, carry):
        c0 = unroll * i
        for u in range(unroll):
            bufs[(u + 1) % 2][...] = scores(jnp.minimum(c0 + u + 1, n - 1))
            update(c0 + u, bufs[u % 2][...])
        return carry

    lax.fori_loop(0, n // unroll, body, 0)
    g = g_ref[...].astype(F32)
    o_ref[...] = ((acc_s[...] * (1.0 / l_s[...])).T * (g * jax.nn.sigmoid(g))).astype(o_ref.dtype)


def _mla_attention(q_all, k_all, vt_all, proj, batch, seq, tq, tk):
    nq = seq // tq
    n = seq // tk
    assert n % 2 == 0 and MLA_UNROLL % 2 == 0
    return pl.pallas_call(
        functools.partial(_mla_attn_kernel, tk=tk),
        grid=(batch, MLA_HEADS, nq),
        in_specs=[pl.BlockSpec((tq, MLA_QK_PAD), lambda b, h, i: (b * nq + i, h)),
                  pl.BlockSpec((seq, MLA_QK_PAD), lambda b, h, i: (b, h)),
                  pl.BlockSpec((MLA_V, seq), lambda b, h, i: (h, b)),
                  pl.BlockSpec((tq, LANES), lambda b, h, i: (b * nq + i, C0_GB + h))],
        out_specs=pl.BlockSpec((tq, MLA_V), lambda b, h, i: (b * nq + i, h)),
        out_shape=jax.ShapeDtypeStruct((batch * seq, WB), BF16),
        scratch_shapes=[pltpu.VMEM((tk, tq), F32), pltpu.VMEM((tk, tq), F32), pltpu.VMEM((1, tq), F32),
                        pltpu.VMEM((1, tq), F32), pltpu.VMEM((MLA_V, tq), F32)],
        compiler_params=_params(3),
    )(q_all, k_all, vt_all, proj)


OUT_LN_SPLIT = 2


def _out_ln_kernel(y1_ref, y2_ref, w_ref, x_ref, g_ref, b_ref, o_ref, ob_ref):
    half = y1_ref.shape[1]
    tm = y1_ref.shape[0]
    sub = tm // OUT_LN_SPLIT

    def project(rows):
        y = jnp.dot(y1_ref[rows, :], w_ref[:half, :], preferred_element_type=F32)
        return y + jnp.dot(y2_ref[rows, :], w_ref[half:, :], preferred_element_type=F32)

    def norm(rows, y):
        z = ALPHA * x_ref[rows, :] + y
        mu = jnp.mean(z, axis=1, keepdims=True)
        zc = z - mu
        var = jnp.mean(zc * zc, axis=1, keepdims=True)
        out = zc * lax.rsqrt(var + LN_EPS) * g_ref[...] + b_ref[...]
        o_ref[rows, :] = out
        ob_ref[rows, :] = out.astype(ob_ref.dtype)

    rows = [slice(i * sub, (i + 1) * sub) for i in range(OUT_LN_SPLIT)]
    y_prev = project(rows[0])
    for i in range(1, OUT_LN_SPLIT):
        y_next = project(rows[i])
        norm(rows[i - 1], y_prev)
        y_prev = y_next
    norm(rows[-1], y_prev)


def _out_ln(y1, y1_col, y2, y2_col, w, x, g, b, tm):
    m, d = x.shape
    half = w.shape[0] // 2
    full = lambda a: pl.BlockSpec(a.shape, lambda i: (0,) * a.ndim)
    return pl.pallas_call(
        _out_ln_kernel,
        grid=(m // tm,),
        in_specs=[pl.BlockSpec((tm, half), lambda i: (i, y1_col)),
                  pl.BlockSpec((tm, half), lambda i: (i, y2_col)),
                  full(w), pl.BlockSpec((tm, d), lambda i: (i, 0)), full(g), full(b)],
        out_specs=[pl.BlockSpec((tm, d), lambda i: (i, 0)), pl.BlockSpec((tm, d), lambda i: (i, 0))],
        out_shape=[jax.ShapeDtypeStruct((m, d), F32), jax.ShapeDtypeStruct((m, d), BF16)],
        compiler_params=_params(1),
    )(y1, y2, w, x, g, b)


DIL_DILS = tuple(d for _, d in DIL_PAIRS)
DIL_SIDE = DIL_PAIRS[0][0] // 2 // DIL_PAIRS[0][1]
DIL_TQ = 128
DIL_TK = DIL_TQ + 2 * DIL_SIDE
DIL_SB = DIL_TQ * DIL_DILS[-1]
assert all(w // 2 // d == DIL_SIDE for w, d in DIL_PAIRS)


def _dil_kernel(*refs, nsb):
    q_refs = refs[0:3]
    k_refs = [refs[3 + 3 * p:6 + 3 * p] for p in range(3)]
    v_refs = [refs[12 + 3 * p:15 + 3 * p] for p in range(3)]
    g_ref, o_ref, bias_s, o_s, m_s, l_s = refs[21:]
    sb = pl.program_id(2)
    dn = (((1,), (1,)), ((), ()))

    qi = lax.broadcasted_iota(jnp.int32, (DIL_TQ, DIL_TK), 0)
    kj = lax.broadcasted_iota(jnp.int32, (DIL_TQ, DIL_TK), 1)
    base = jnp.where((kj >= qi) & (kj <= qi + 2 * DIL_SIDE), 0.0, NEG).astype(F32)
    lo = jnp.where(kj < DIL_SIDE, NEG, 0.0) * (sb == 0).astype(F32)
    hi = jnp.where(kj >= DIL_TK - DIL_SIDE, NEG, 0.0) * (sb == nsb - 1).astype(F32)
    bias_s[0] = base
    bias_s[1] = base + lo
    bias_s[2] = base + hi
    bias_s[3] = base + lo + hi

    def window(trio, lanes, c, n):
        main, prev, nxt = trio
        if n == 1:
            return jnp.concatenate([prev[:, lanes], main[:, lanes], nxt[:, lanes]], axis=0), 3
        if c == 0:
            return jnp.concatenate([prev[:, lanes], main[0:DIL_TK - DIL_SIDE, lanes]], axis=0), 1
        if c == n - 1:
            return jnp.concatenate([main[n * DIL_TQ - (DIL_TK - DIL_SIDE):n * DIL_TQ, lanes],
                                    nxt[:, lanes]], axis=0), 2
        return main[c * DIL_TQ - DIL_SIDE:c * DIL_TQ + DIL_TQ + DIL_SIDE, lanes], 0

    def scores(t):
        pair, r, c, n = t
        lanes = slice(r * LANES, (r + 1) * LANES)
        kw, mask_id = window(k_refs[pair], lanes, c, n)
        q = q_refs[pair][c * DIL_TQ:(c + 1) * DIL_TQ, lanes]
        return lax.dot_general(q, kw, dn, preferred_element_type=F32), mask_id

    def finish(t, s, mask_id):
        pair, r, c, n = t
        dil = DIL_DILS[pair]
        vw, _ = window(v_refs[pair], slice(r * LANES, (r + 1) * LANES), c, n)
        s = s + bias_s[mask_id]
        m = jnp.max(s, axis=1, keepdims=True)
        p = jnp.exp2(s - m)
        l = jnp.sum(p, axis=1, keepdims=True)
        num = jnp.dot(p.astype(BF16), vw, preferred_element_type=F32)
        m = jnp.broadcast_to(m, (DIL_TQ, LANES))
        l = jnp.broadcast_to(l, (DIL_TQ, LANES))
        start = (c * DIL_TQ) * dil + r
        if dil > 1:
            rows = pl.ds(start, DIL_TQ, stride=dil)
            o_s[pair - 1, rows, :] = num
            m_s[pair - 1, rows, :] = m
            l_s[pair - 1, rows, :] = l
            return
        rows = pl.ds(start, DIL_TQ)
        others = range(len(DIL_DILS) - 1)
        ms = [m] + [m_s[i, rows, :] for i in others]
        dens = [l] + [l_s[i, rows, :] for i in others]
        nums = [num] + [o_s[i, rows, :] for i in others]
        top = functools.reduce(jnp.maximum, ms)
        w = [jnp.exp2(x - top) for x in ms]
        num_tot = sum(wi * ni for wi, ni in zip(w, nums))
        den_tot = sum(wi * di for wi, di in zip(w, dens))
        g = g_ref[rows, :].astype(F32)
        o_ref[rows, :] = (num_tot / den_tot * (g * jax.nn.sigmoid(g))).astype(o_ref.dtype)

    for pair in reversed(range(len(DIL_DILS))):
        n = DIL_SB // DIL_DILS[pair] // DIL_TQ
        for r in range(DIL_DILS[pair]):
            for c in range(n):
                t = (pair, r, c, n)
                finish(t, *scores(t))


def _dil_attention(qk_views, v_views, gate, batch, seq):
    tokens = gate.shape[0]
    nsb = seq // DIL_SB
    halo = DIL_SIDE
    specs_main, specs_halo = [], []
    for dil in DIL_DILS:
        rows, width = tokens // dil, dil * LANES
        main_rows = DIL_SB // dil
        per_main = main_rows // halo
        last = rows // halo - 1

        def main_spec(h0, main_rows=main_rows, width=width):
            return pl.BlockSpec((None, main_rows, width), lambda b, h, s: (h0 + h, b * nsb + s, 0))

        def prev_spec(h0, per_main=per_main, width=width):
            return pl.BlockSpec((None, halo, width),
                                lambda b, h, s: (h0 + h, jnp.maximum((b * nsb + s) * per_main - 1, 0), 0))

        def next_spec(h0, per_main=per_main, width=width, last=last):
            return pl.BlockSpec((None, halo, width),
                                lambda b, h, s: (h0 + h, jnp.minimum((b * nsb + s + 1) * per_main, last), 0))

        specs_main.append(main_spec)
        specs_halo.append((prev_spec, next_spec))
    in_specs = [specs_main[p](0) for p in range(3)]
    operands = list(qk_views)
    for h0, views in ((DIL_HEADS, qk_views), (0, v_views)):
        for p in range(3):
            in_specs += [specs_main[p](h0), specs_halo[p][0](h0), specs_halo[p][1](h0)]
            operands += [views[p]] * 3
    in_specs.append(pl.BlockSpec((DIL_SB, LANES), lambda b, h, s: (b * nsb + s, h)))
    operands.append(gate)
    return pl.pallas_call(
        functools.partial(_dil_kernel, nsb=nsb),
        grid=(batch, DIL_HEADS, nsb),
        in_specs=in_specs,
        out_specs=pl.BlockSpec((DIL_SB, LANES), lambda b, h, s: (b * nsb + s, h)),
        out_shape=jax.ShapeDtypeStruct((tokens, WC), BF16),
        scratch_shapes=[pltpu.VMEM((4, DIL_TQ, DIL_TK), F32),
                        ] + [pltpu.VMEM((len(DIL_DILS) - 1, DIL_SB, LANES), F32)] * 3,
        compiler_params=_params(3),
    )(*operands)


def _rope_tables(seq, half):
    inv = ROPE_THETA ** (-jnp.arange(half, dtype=F32) / half)
    ang = jnp.arange(seq, dtype=F32)[:, None] * inv[None, :]
    cos, sin = jnp.cos(ang), jnp.sin(ang)
    pad = jnp.zeros((seq, LANES - 2 * half), F32)
    return jnp.concatenate([cos, cos, pad], 1), jnp.concatenate([-sin, sin, pad], 1)


def _prep_weights(ab_w_in, ab_w_q_up, ab_w_kv_up, ab_w_out, c_w_in, c_w_out):
    def cast_scaling_q(w, q_cols, scale):
        col = lax.broadcasted_iota(jnp.int32, (1, w.shape[1]), 1)
        return (w * jnp.where(col < q_cols, scale, 1.0)).astype(BF16)

    w0 = cast_scaling_q(ab_w_in, WA, HEAD_DIM ** -0.5 * LOG2E)
    tail0 = 4 * WA + MLA_Q_RANK + MLA_KV_RANK
    w_tail = jnp.concatenate(
        [w0[:, tail0 + MLA_ROPE:], w0[:, tail0:tail0 + MLA_ROPE],
         jnp.zeros((D_MODEL, IN0_TAIL_WIDTH - WB - MLA_ROPE), BF16)], axis=1)
    w_in0 = (w0, w_tail, w0[:, 2 * WA:3 * WA].T)
    wq = (ab_w_q_up * ((MLA_NOPE + MLA_ROPE) ** -0.5 * LOG2E)).reshape(
        MLA_Q_RANK, MLA_HEADS, MLA_NOPE + MLA_ROPE)
    wq = jnp.pad(wq, ((0, 0), (0, 0), (0, MLA_QK_PAD - MLA_NOPE - MLA_ROPE)))
    w_q = wq.reshape(MLA_Q_RANK, MLA_HEADS * MLA_QK_PAD).astype(BF16)
    w_kv = ab_w_kv_up.astype(BF16)
    w_in1 = cast_scaling_q(c_w_in, WC, HEAD_DIM ** -0.5 * LOG2E)
    return w_in0, w_q, w_kv, ab_w_out.astype(BF16), w_in1, c_w_out.astype(BF16)


def _trunk(x, w, tabs):
    batch, seq, _ = x.shape
    (w_in0, w_q, w_kv, w_out0, w_in1, w_out1, na_bias, q_g, kv_g,
     ln0_g, ln0_b, ln1_g, ln1_b) = w
    cos64, sin64, cos128, sin128 = tabs
    x2 = x.reshape(batch * seq, D_MODEL)
    w0, w_tail, w_va_t = w_in0
    proj0, xb = _proj_f32(x2, w0, tm=1024, tn=IN0_TN, w_blocks=IN0_MAIN_BLOCKS)
    tail0 = _proj_bf16(xb, w_tail, tm=1024, tn=IN0_TAIL_WIDTH, col0=0, n=IN0_TAIL_WIDTH)
    vt_na = _proj_t(w_va_t, xb, batch, seq, tm=1024)
    ya = _na_attention(proj0, vt_na, na_bias, batch, seq)
    q_all, k_all, vt_all = _mla_up(proj0, tail0, q_g, w_q, kv_g, w_kv, cos64, sin64, seq, tm=512)
    yb = _mla_attention(q_all, k_all, vt_all, tail0, batch, seq, tq=1024, tk=1024)
    x1, x1b = _out_ln(ya, 0, yb, 0, w_out0, x2, ln0_g, ln0_b, tm=512)
    proj1 = functools.partial(_proj_heads, x1b, w_in1, cos_t=cos128, sin_t=sin128, seq=seq,
                              tm=1024, tn=1024, dils=DIL_DILS)
    qk1 = proj1(col0=0, n=2 * WC, rope=True)
    v1 = proj1(col0=2 * WC, n=WC, rope=False)
    gate1 = _proj_bf16(x1b, w_in1, tm=1024, tn=1024, col0=3 * WC, n=WC)
    yc = _dil_attention(qk1, v1, gate1, batch, seq)
    y, _ = _out_ln(yc, 0, yc, 1, w_out1, x1, ln1_g, ln1_b, tm=512)
    return y.reshape(batch, seq, D_MODEL)


def kernel(x_prompt, x_sample, ab_w_in, ab_rpb, ab_q_norm_g, ab_w_q_up, ab_kv_norm_g, ab_w_kv_up,
           ab_w_out, ab_ln_g, ab_ln_b, c_w_in, c_w_out, c_ln_g, c_ln_b):
    w_in0, w_q, w_kv, w_out0, w_in1, w_out1 = _prep_weights(
        ab_w_in, ab_w_q_up, ab_w_kv_up, ab_w_out, c_w_in, c_w_out)
    row = lambda a: a.reshape(1, -1).astype(F32)
    w = (w_in0, w_q, w_kv, w_out0, w_in1, w_out1, _na_bias(ab_rpb), row(ab_q_norm_g), row(ab_kv_norm_g),
         row(ab_ln_g), row(ab_ln_b), row(c_ln_g), row(c_ln_b))
    max_seq = max(x_prompt.shape[1], x_sample.shape[1])
    tabs = _rope_tables(max_seq, MLA_ROPE // 2) + _rope_tables(max_seq, HEAD_DIM // 2)
    return tuple(_trunk(x, w, tabs) for x in (x_prompt, x_sample))
```

```python
import functools
import math

import jax
import jax.numpy as jnp
from jax import lax
from jax.experimental import pallas as pl
from jax.experimental.pallas import tpu as pltpu

F32 = jnp.float32
BF16 = jnp.bfloat16

D_MODEL = 2048
DEPTH = 2
GRID_W = 64
HEAD_DIM = 128
NA_HEADS = 8
NA_WIN_ROWS = 8
NA_WIN_COLS = 16
MLA_HEADS = 8
MLA_Q_RANK = 512
MLA_KV_RANK = 512
MLA_NOPE = 128
MLA_ROPE = 64
MLA_V = 128
DIL_HEADS = 16
DIL_PAIRS = ((128, 1), (512, 4), (2048, 16))
DIL_HALF = 1024
WA = NA_HEADS * HEAD_DIM
WB = MLA_HEADS * MLA_V
WC = DIL_HEADS * HEAD_DIM
ROPE_THETA = 10000.0
ALPHA = (2 * DEPTH) ** 0.25
LN_EPS = 1e-5
RMS_EPS = 1e-6
NEG = -1e30
LOG2E = 1.4426950408889634

LANES = 128
MLA_QK_PAD = 2 * LANES
MXU_WIDTH = 256
VMEM_LIMIT = 56 * 1024 * 1024

IN0_TN = WA
IN0_MAIN_BLOCKS = (0, 1, 3, 4)
IN0_TAIL_WIDTH = -(-(WB + MLA_ROPE) // MXU_WIDTH) * MXU_WIDTH
assert MLA_Q_RANK + MLA_KV_RANK == IN0_TN

C0_Q, C0_K, C0_G = 0, 8, 16
C0_QLAT, C0_KVLAT = 6, 7
C0_GB = 0
C0_KROPE = WB // LANES

NA_QROWS = 4
NA_KROWS = NA_QROWS + NA_WIN_ROWS
NA_TQ = NA_QROWS * GRID_W
NA_TK = NA_KROWS * GRID_W
NA_KPARTS = NA_KROWS // NA_QROWS
NA_SUB = 4
NA_GROUP = 4
assert (NA_WIN_ROWS // 2) % NA_QROWS == 0 and NA_KROWS % NA_QROWS == 0


def _params(n_axes):
    return pltpu.CompilerParams(dimension_semantics=("arbitrary",) * n_axes,
                                vmem_limit_bytes=VMEM_LIMIT)


def _proj_f32_kernel(x_ref, w_ref, o_ref, xb_ref):
    @pl.when(pl.program_id(1) == 0)
    def _():
        xb_ref[...] = x_ref[...].astype(BF16)

    o_ref[...] = jnp.dot(xb_ref[...], w_ref[...], preferred_element_type=F32).astype(o_ref.dtype)


def _proj_f32(x, w, tm, tn, w_blocks):
    m, k = x.shape
    n_blocks = len(w_blocks)
    lo, skip = w_blocks[0], [b for b in range(w_blocks[0], w_blocks[-1]) if b not in w_blocks]
    assert len(skip) <= 1 and list(w_blocks) == sorted(w_blocks)

    def w_index(i, j):
        jj = j + lo
        return (0, jj + (jj >= skip[0]) if skip else jj)

    return pl.pallas_call(
        _proj_f32_kernel,
        grid=(m // tm, n_blocks),
        in_specs=[pl.BlockSpec((tm, k), lambda i, j: (i, 0)),
                  pl.BlockSpec((k, tn), w_index)],
        out_specs=[pl.BlockSpec((tm, tn), lambda i, j: (i, j)),
                   pl.BlockSpec((tm, k), lambda i, j: (i, 0))],
        out_shape=[jax.ShapeDtypeStruct((m, n_blocks * tn), BF16), jax.ShapeDtypeStruct((m, k), BF16)],
        compiler_params=_params(2),
    )(x, w)


def _proj_t_kernel(w_ref, x_ref, o_ref):
    dn = (((1,), (1,)), ((), ()))
    o_ref[...] = lax.dot_general(w_ref[...], x_ref[...], dn,
                                 preferred_element_type=F32).astype(o_ref.dtype)


def _proj_t(w_t, x, batch, seq, tm):
    n, k = w_t.shape
    per_seq = seq // tm
    return pl.pallas_call(
        _proj_t_kernel,
        grid=(batch * per_seq,),
        in_specs=[pl.BlockSpec((n, k), lambda i: (0, 0)),
                  pl.BlockSpec((tm, k), lambda i: (i, 0))],
        out_specs=pl.BlockSpec((None, n, tm), lambda i: (i // per_seq, 0, i % per_seq)),
        out_shape=jax.ShapeDtypeStruct((batch, n, seq), BF16),
        compiler_params=_params(1),
    )(w_t, x)


def _proj_bf16_kernel(x_ref, w_ref, o_ref):
    o_ref[...] = jnp.dot(x_ref[...], w_ref[...], preferred_element_type=F32).astype(o_ref.dtype)


def _proj_bf16(x, w, tm, tn, col0, n):
    m, k = x.shape
    blk0 = col0 // tn
    return pl.pallas_call(
        _proj_bf16_kernel,
        grid=(m // tm, n // tn),
        in_specs=[pl.BlockSpec((tm, k), lambda i, j: (i, 0)),
                  pl.BlockSpec((k, tn), lambda i, j: (0, blk0 + j))],
        out_specs=pl.BlockSpec((tm, tn), lambda i, j: (i, j)),
        out_shape=jax.ShapeDtypeStruct((m, n), BF16),
        compiler_params=_params(2),
    )(x, w)


PROJ_SPLIT = 2


def _proj_heads_kernel(x_ref, w_ref, cos_ref, sin_ref, *rest, rope, dils):
    o_refs, scrs = rest[:len(dils)], rest[len(dils):]
    tg = x_ref.shape[0] // PROJ_SPLIT

    def project(g):
        return jnp.dot(x_ref[g * tg:(g + 1) * tg, :], w_ref[...], preferred_element_type=F32)

    def emit(g, y):
        for h in range(y.shape[1] // LANES):
            yh = y[:, h * LANES:(h + 1) * LANES]
            if rope:
                rows = slice(g * tg, (g + 1) * tg)
                yh = yh * cos_ref[rows, :] + pltpu.roll(yh, LANES // 2, 1) * sin_ref[rows, :]
            o_refs[0][h, g * tg:(g + 1) * tg, :] = yh.astype(o_refs[0].dtype)
            scrs[0][h, 0, g * tg:(g + 1) * tg, :] = yh
            for lvl in range(1, len(dils)):
                d_prev, d = dils[lvl - 1], dils[lvl]
                q, n = d // d_prev, tg // d
                for r_prev in range(d_prev):
                    for a in range(q):
                        part = scrs[lvl - 1][h, r_prev, pl.ds(g * tg // d_prev + a, n, stride=q), :]
                        r = d_prev * a + r_prev
                        o_refs[lvl][h, g * n:(g + 1) * n, r * LANES:(r + 1) * LANES] = (
                            part.astype(o_refs[lvl].dtype))
                        if lvl + 1 < len(dils):
                            scrs[lvl][h, r, g * n:(g + 1) * n, :] = part

    y_prev = project(0)
    for g in range(1, PROJ_SPLIT):
        y_next = project(g)
        emit(g - 1, y_prev)
        y_prev = y_next
    emit(PROJ_SPLIT - 1, y_prev)


def _proj_heads(x, w, col0, n, cos_t, sin_t, seq, tm, tn, rope, dils):
    m, k = x.shape
    blk0 = col0 // tn
    pos_blocks = seq // tm
    hb = tn // LANES
    return pl.pallas_call(
        functools.partial(_proj_heads_kernel, rope=rope, dils=dils),
        grid=(m // tm, n // tn),
        in_specs=[pl.BlockSpec((tm, k), lambda i, j: (i, 0)),
                  pl.BlockSpec((k, tn), lambda i, j: (0, blk0 + j)),
                  pl.BlockSpec((tm, LANES), lambda i, j: (i % pos_blocks, 0)),
                  pl.BlockSpec((tm, LANES), lambda i, j: (i % pos_blocks, 0))],
        out_specs=[pl.BlockSpec((hb, tm // d, d * LANES), lambda i, j: (j, i, 0)) for d in dils],
        out_shape=[jax.ShapeDtypeStruct((n // LANES, m // d, d * LANES), BF16) for d in dils],
        scratch_shapes=[pltpu.VMEM((hb, d, tm // d, LANES), F32) for d in dils[:-1]],
        compiler_params=_params(2),
    )(x, w, cos_t, sin_t)


def _na_bias_kernel(rpb_ref, o_ref, t_ref):
    h = pl.program_id(0)
    n_dc = 2 * NA_WIN_COLS - 1
    kc = lax.broadcasted_iota(jnp.int32, (GRID_W, GRID_W), 0)
    c = lax.broadcasted_iota(jnp.int32, (GRID_W, GRID_W), 1)
    d = kc - c + (NA_WIN_COLS - 1)
    cs = jnp.clip(c - NA_WIN_COLS // 2, 0, GRID_W - NA_WIN_COLS)
    col_ok = (kc >= cs) & (kc < cs + NA_WIN_COLS)
    for dr in range(2 * NA_WIN_ROWS - 1):
        t = jnp.zeros((GRID_W, GRID_W), F32)
        for dd in range(n_dc):
            t = jnp.where(d == dd, rpb_ref[h, dr * n_dc + dd] * LOG2E, t)
        t_ref[dr] = jnp.where(col_ok, t, NEG)
    neg_blk = jnp.full((GRID_W, GRID_W), NEG, F32)
    for typ in range(3):
        for a in range(NA_QROWS):
            half = NA_WIN_ROWS // 2
            if typ == 0:
                lo, dr0 = max(a - half, 0), NA_WIN_ROWS - 1 - a
            elif typ == 1:
                lo, dr0 = a, NA_WIN_ROWS - 1 - half - a
            else:
                lo, dr0 = NA_QROWS + min(a - NA_QROWS + half, 0), -1 - a
            for kb in range(NA_KROWS):
                ok = lo <= kb < lo + NA_WIN_ROWS
                blk = t_ref[kb + dr0] if ok else neg_blk
                o_ref[0, typ, kb * GRID_W:(kb + 1) * GRID_W, a * GRID_W:(a + 1) * GRID_W] = blk


def _na_bias(rpb):
    rpb2 = rpb.reshape(NA_HEADS, -1)
    return pl.pallas_call(
        _na_bias_kernel,
        grid=(NA_HEADS,),
        in_specs=[pl.BlockSpec(memory_space=pltpu.SMEM)],
        out_specs=pl.BlockSpec((1, 3, NA_TK, NA_TQ), lambda h: (h, 0, 0, 0)),
        out_shape=jax.ShapeDtypeStruct((NA_HEADS, 3, NA_TK, NA_TQ), F32),
        scratch_shapes=[pltpu.VMEM((2 * NA_WIN_ROWS - 1, GRID_W, GRID_W), F32)],
        compiler_params=_params(1),
    )(rpb2)


def _na_kernel(q_ref, *refs):
    k_refs = [refs[NA_KPARTS * s:NA_KPARTS * (s + 1)] for s in range(NA_SUB)]
    vt_refs = [refs[NA_KPARTS * (NA_SUB + s):NA_KPARTS * (NA_SUB + s + 1)] for s in range(NA_SUB)]
    g_ref = refs[2 * NA_KPARTS * NA_SUB]
    b_refs = refs[2 * NA_KPARTS * NA_SUB + 1:2 * NA_KPARTS * NA_SUB + 1 + NA_SUB]
    o_ref = refs[-1]
    dn = (((1,), (1,)), ((), ()))
    blocks = [(e, s) for e in range(q_ref.shape[0]) for s in range(NA_SUB)]
    sts = []
    for e, s in blocks:
        k = jnp.concatenate([r[e] for r in k_refs[s]], axis=0)
        q = q_ref[e, s * NA_TQ:(s + 1) * NA_TQ, :]
        sts.append(lax.dot_general(k, q, dn, preferred_element_type=F32) + b_refs[s][0, 0])
    for (e, s), st in zip(blocks, sts):
        vt = jnp.concatenate([r[e] for r in vt_refs[s]], axis=1)
        m = jnp.max(st, axis=0, keepdims=True)
        p = jnp.exp2(st - m)
        l = jnp.sum(p, axis=0, keepdims=True)
        ot = jnp.dot(vt, p.astype(BF16), preferred_element_type=F32)
        g = g_ref[e, s * NA_TQ:(s + 1) * NA_TQ, :].astype(F32)
        o_ref[e, s * NA_TQ:(s + 1) * NA_TQ, :] = (
            (ot * (1.0 / l)).T * (g * jax.nn.sigmoid(g))).astype(o_ref.dtype)


def _na_attention(proj, vt_na, bias, batch, seq):
    rows = seq // GRID_W
    n_blocks = rows // NA_QROWS
    kq = NA_TK // NA_KPARTS
    proj3 = proj.reshape(batch, seq, proj.shape[1])
    group = math.gcd(batch, NA_GROUP)

    def kbase(jj):
        return jnp.clip(jj - (NA_WIN_ROWS // 2) // NA_QROWS, 0, n_blocks - NA_KPARTS)

    def btype(jj):
        return jnp.where(jj == 0, 0, jnp.where(jj == n_blocks - 1, 2, 1))

    def k_spec(s, part):
        return pl.BlockSpec((group, kq, LANES),
                            lambda h, j, b: (b, kbase(NA_SUB * j + s) + part, C0_K + h))

    def vt_spec(s, part):
        return pl.BlockSpec((group, HEAD_DIM, kq), lambda h, j, b: (b, h, kbase(NA_SUB * j + s) + part))

    def b_spec(s):
        return pl.BlockSpec((1, 1, NA_TK, NA_TQ), lambda h, j, b: (h, btype(NA_SUB * j + s), 0, 0))

    tok_spec = lambda col0: pl.BlockSpec((group, NA_SUB * NA_TQ, LANES), lambda h, j, b: (b, j, col0 + h))
    parts = [(s, i) for s in range(NA_SUB) for i in range(NA_KPARTS)]
    in_specs = ([tok_spec(C0_Q)] + [k_spec(s, i) for s, i in parts] + [vt_spec(s, i) for s, i in parts]
                + [tok_spec(C0_G)] + [b_spec(s) for s in range(NA_SUB)])
    out = pl.pallas_call(
        _na_kernel,
        grid=(NA_HEADS, n_blocks // NA_SUB, batch // group),
        in_specs=in_specs,
        out_specs=pl.BlockSpec((group, NA_SUB * NA_TQ, LANES), lambda h, j, b: (b, j, h)),
        out_shape=jax.ShapeDtypeStruct((batch, seq, WA), BF16),
        compiler_params=_params(3),
    )(proj3, *([proj3] * len(parts)), *([vt_na] * len(parts)), proj3, *([bias] * NA_SUB))
    return out.reshape(batch * seq, WA)


def _rms(lat_ref, g_ref):
    x = lat_ref[...].astype(F32)
    ms = jnp.mean(x * x, axis=1, keepdims=True)
    return (x * lax.rsqrt(ms + RMS_EPS) * g_ref[...]).astype(BF16)


def _rope64(t, cos_t, sin_t):
    lane = lax.broadcasted_iota(jnp.int32, t.shape, 1)
    half = MLA_ROPE // 2
    partner = jnp.where(lane < half, pltpu.roll(t, LANES - half, 1), pltpu.roll(t, half, 1))
    return t * cos_t + partner * sin_t


def _mla_q_up_kernel(lat_ref, g_ref, w_ref, cos_ref, sin_ref, o_ref):
    y = jnp.dot(_rms(lat_ref, g_ref), w_ref[...], preferred_element_type=F32)
    c = cos_ref[...]
    s = sin_ref[...]
    for h in range(MLA_HEADS):
        base = h * MLA_QK_PAD
        o_ref[:, base:base + LANES] = y[:, base:base + LANES].astype(o_ref.dtype)
        o_ref[:, base + LANES:base + 2 * LANES] = _rope64(
            y[:, base + LANES:base + 2 * LANES], c, s).astype(o_ref.dtype)


def _mla_kv_up_kernel(lat_ref, g_ref, w_ref, kr_ref, cos_ref, sin_ref, k_ref, vt_ref):
    y = jnp.dot(_rms(lat_ref, g_ref), w_ref[...], preferred_element_type=F32)
    k_pe = _rope64(kr_ref[...].astype(F32), cos_ref[...], sin_ref[...]).astype(k_ref.dtype)
    for h in range(MLA_HEADS):
        base = h * (MLA_NOPE + MLA_V)
        k_ref[:, h * MLA_QK_PAD:h * MLA_QK_PAD + LANES] = y[:, base:base + MLA_NOPE].astype(k_ref.dtype)
        k_ref[:, h * MLA_QK_PAD + LANES:(h + 1) * MLA_QK_PAD] = k_pe
        vt_ref[h * MLA_V:(h + 1) * MLA_V, :] = y[:, base + MLA_NOPE:base + MLA_NOPE + MLA_V].T.astype(vt_ref.dtype)


def _mla_up(proj, tail, q_g, w_q, kv_g, w_kv, cos_t, sin_t, seq, tm):
    m = proj.shape[0]
    pos_blocks = seq // tm
    tab = pl.BlockSpec((tm, LANES), lambda i: (i % pos_blocks, 0))
    full = lambda a: pl.BlockSpec(a.shape, lambda i: (0,) * a.ndim)
    q_all = pl.pallas_call(
        _mla_q_up_kernel,
        grid=(m // tm,),
        in_specs=[pl.BlockSpec((tm, MLA_Q_RANK), lambda i: (i, C0_QLAT)), full(q_g), full(w_q), tab, tab],
        out_specs=pl.BlockSpec((tm, MLA_HEADS * MLA_QK_PAD), lambda i: (i, 0)),
        out_shape=jax.ShapeDtypeStruct((m, MLA_HEADS * MLA_QK_PAD), BF16),
        compiler_params=_params(1),
    )(proj, q_g, w_q, cos_t, sin_t)
    k_all, vt_all = pl.pallas_call(
        _mla_kv_up_kernel,
        grid=(m // tm,),
        in_specs=[pl.BlockSpec((tm, MLA_KV_RANK), lambda i: (i, C0_KVLAT)), full(kv_g), full(w_kv),
                  pl.BlockSpec((tm, LANES), lambda i: (i, C0_KROPE)), tab, tab],
        out_specs=[pl.BlockSpec((tm, MLA_HEADS * MLA_QK_PAD), lambda i: (i, 0)),
                   pl.BlockSpec((WB, tm), lambda i: (0, i))],
        out_shape=[jax.ShapeDtypeStruct((m, MLA_HEADS * MLA_QK_PAD), BF16),
                   jax.ShapeDtypeStruct((WB, m), BF16)],
        compiler_params=_params(1),
    )(proj, kv_g, w_kv, tail, cos_t, sin_t)
    return q_all, k_all, vt_all


MLA_UNROLL = 8


def _mla_attn_kernel(q_ref, k_ref, vt_ref, g_ref, o_ref, s0, s1, m_s, l_s, acc_s, *, tk):
    q = q_ref[...]
    tq = q.shape[0]
    n = k_ref.shape[0] // tk
    dn = (((1,), (1,)), ((), ()))

    def scores(c):
        start = pl.multiple_of(c * tk, tk)
        return lax.dot_general(k_ref[pl.ds(start, tk), :], q, dn, preferred_element_type=F32)

    def update(c, st):
        vt = vt_ref[:, pl.ds(pl.multiple_of(c * tk, tk), tk)]
        m = m_s[...]
        m_new = jnp.maximum(m, jnp.max(st, axis=0, keepdims=True))
        alpha = jnp.exp2(m - m_new)
        p = jnp.exp2(st - m_new)
        m_s[...] = m_new
        l_s[...] = alpha * l_s[...] + jnp.sum(p, axis=0, keepdims=True)
        acc_s[...] = alpha * acc_s[...] + jnp.dot(vt, p.astype(BF16), preferred_element_type=F32)

    m_s[...] = jnp.full(m_s.shape, NEG, F32)
    l_s[...] = jnp.zeros(l_s.shape, F32)
    acc_s[...] = jnp.zeros(acc_s.shape, F32)
    s0[...] = scores(0)

    bufs = (s0, s1)
    unroll = math.gcd(MLA_UNROLL, n)

    def body(i, carry):
        c0 = unroll * i
        for u in range(unroll):
            bufs[(u + 1) % 2][...] = scores(jnp.minimum(c0 + u + 1, n - 1))
            update(c0 + u, bufs[u % 2][...])
        return carry

    lax.fori_loop(0, n // unroll, body, 0)
    g = g_ref[...].astype(F32)
    o_ref[...] = ((acc_s[...] * (1.0 / l_s[...])).T * (g * jax.nn.sigmoid(g))).astype(o_ref.dtype)


def _mla_attention(q_all, k_all, vt_all, proj, batch, seq, tq, tk):
    nq = seq // tq
    n = seq // tk
    assert n % 2 == 0 and MLA_UNROLL % 2 == 0
    return pl.pallas_call(
        functools.partial(_mla_attn_kernel, tk=tk),
        grid=(batch, MLA_HEADS, nq),
        in_specs=[pl.BlockSpec((tq, MLA_QK_PAD), lambda b, h, i: (b * nq + i, h)),
                  pl.BlockSpec((seq, MLA_QK_PAD), lambda b, h, i: (b, h)),
                  pl.BlockSpec((MLA_V, seq), lambda b, h, i: (h, b)),
                  pl.BlockSpec((tq, LANES), lambda b, h, i: (b * nq + i, C0_GB + h))],
        out_specs=pl.BlockSpec((tq, MLA_V), lambda b, h, i: (b * nq + i, h)),
        out_shape=jax.ShapeDtypeStruct((batch * seq, WB), BF16),
        scratch_shapes=[pltpu.VMEM((tk, tq), F32), pltpu.VMEM((tk, tq), F32), pltpu.VMEM((1, tq), F32),
                        pltpu.VMEM((1, tq), F32), pltpu.VMEM((MLA_V, tq), F32)],
        compiler_params=_params(3),
    )(q_all, k_all, vt_all, proj)


OUT_LN_SPLIT = 2


def _out_ln_kernel(y1_ref, y2_ref, w_ref, x_ref, g_ref, b_ref, o_ref, ob_ref):
    half = y1_ref.shape[1]
    tm = y1_ref.shape[0]
    sub = tm // OUT_LN_SPLIT

    def project(rows):
        y = jnp.dot(y1_ref[rows, :], w_ref[:half, :], preferred_element_type=F32)
        return y + jnp.dot(y2_ref[rows, :], w_ref[half:, :], preferred_element_type=F32)

    def norm(rows, y):
        z = ALPHA * x_ref[rows, :] + y
        mu = jnp.mean(z, axis=1, keepdims=True)
        zc = z - mu
        var = jnp.mean(zc * zc, axis=1, keepdims=True)
        out = zc * lax.rsqrt(var + LN_EPS) * g_ref[...] + b_ref[...]
        o_ref[rows, :] = out
        ob_ref[rows, :] = out.astype(ob_ref.dtype)

    rows = [slice(i * sub, (i + 1) * sub) for i in range(OUT_LN_SPLIT)]
    y_prev = project(rows[0])
    for i in range(1, OUT_LN_SPLIT):
        y_next = project(rows[i])
        norm(rows[i - 1], y_prev)
        y_prev = y_next
    norm(rows[-1], y_prev)


def _out_ln(y1, y1_col, y2, y2_col, w, x, g, b, tm):
    m, d = x.shape
    half = w.shape[0] // 2
    full = lambda a: pl.BlockSpec(a.shape, lambda i: (0,) * a.ndim)
    return pl.pallas_call(
        _out_ln_kernel,
        grid=(m // tm,),
        in_specs=[pl.BlockSpec((tm, half), lambda i: (i, y1_col)),
                  pl.BlockSpec((tm, half), lambda i: (i, y2_col)),
                  full(w), pl.BlockSpec((tm, d), lambda i: (i, 0)), full(g), full(b)],
        out_specs=[pl.BlockSpec((tm, d), lambda i: (i, 0)), pl.BlockSpec((tm, d), lambda i: (i, 0))],
        out_shape=[jax.ShapeDtypeStruct((m, d), F32), jax.ShapeDtypeStruct((m, d), BF16)],
        compiler_params=_params(1),
    )(y1, y2, w, x, g, b)


DIL_DILS = tuple(d for _, d in DIL_PAIRS)
DIL_SIDE = DIL_PAIRS[0][0] // 2 // DIL_PAIRS[0][1]
DIL_TQ = 128
DIL_TK = DIL_TQ + 2 * DIL_SIDE
DIL_SB = DIL_TQ * DIL_DILS[-1]
assert all(w // 2 // d == DIL_SIDE for w, d in DIL_PAIRS)


def _dil_kernel(*refs, nsb):
    q_refs = refs[0:3]
    k_refs = [refs[3 + 3 * p:6 + 3 * p] for p in range(3)]
    v_refs = [refs[12 + 3 * p:15 + 3 * p] for p in range(3)]
    g_ref, o_ref, bias_s, o_s, m_s, l_s = refs[21:]
    sb = pl.program_id(2)
    dn = (((1,), (1,)), ((), ()))

    qi = lax.broadcasted_iota(jnp.int32, (DIL_TQ, DIL_TK), 0)
    kj = lax.broadcasted_iota(jnp.int32, (DIL_TQ, DIL_TK), 1)
    base = jnp.where((kj >= qi) & (kj <= qi + 2 * DIL_SIDE), 0.0, NEG).astype(F32)
    lo = jnp.where(kj < DIL_SIDE, NEG, 0.0) * (sb == 0).astype(F32)
    hi = jnp.where(kj >= DIL_TK - DIL_SIDE, NEG, 0.0) * (sb == nsb - 1).astype(F32)
    bias_s[0] = base
    bias_s[1] = base + lo
    bias_s[2] = base + hi
    bias_s[3] = base + lo + hi

    def window(trio, lanes, c, n):
        main, prev, nxt = trio
        if n == 1:
            return jnp.concatenate([prev[:, lanes], main[:, lanes], nxt[:, lanes]], axis=0), 3
        if c == 0:
            return jnp.concatenate([prev[:, lanes], main[0:DIL_TK - DIL_SIDE, lanes]], axis=0), 1
        if c == n - 1:
            return jnp.concatenate([main[n * DIL_TQ - (DIL_TK - DIL_SIDE):n * DIL_TQ, lanes],
                                    nxt[:, lanes]], axis=0), 2
        return main[c * DIL_TQ - DIL_SIDE:c * DIL_TQ + DIL_TQ + DIL_SIDE, lanes], 0

    def scores(t):
        pair, r, c, n = t
        lanes = slice(r * LANES, (r + 1) * LANES)
        kw, mask_id = window(k_refs[pair], lanes, c, n)
        q = q_refs[pair][c * DIL_TQ:(c + 1) * DIL_TQ, lanes]
        return lax.dot_general(q, kw, dn, preferred_element_type=F32), mask_id

    def finish(t, s, mask_id):
        pair, r, c, n = t
        dil = DIL_DILS[pair]
        vw, _ = window(v_refs[pair], slice(r * LANES, (r + 1) * LANES), c, n)
        s = s + bias_s[mask_id]
        m = jnp.max(s, axis=1, keepdims=True)
        p = jnp.exp2(s - m)
        l = jnp.sum(p, axis=1, keepdims=True)
        num = jnp.dot(p.astype(BF16), vw, preferred_element_type=F32)
        m = jnp.broadcast_to(m, (DIL_TQ, LANES))
        l = jnp.broadcast_to(l, (DIL_TQ, LANES))
        start = (c * DIL_TQ) * dil + r
        if dil > 1:
            rows = pl.ds(start, DIL_TQ, stride=dil)
            o_s[pair - 1, rows, :] = num
            m_s[pair - 1, rows, :] = m
            l_s[pair - 1, rows, :] = l
            return
        rows = pl.ds(start, DIL_TQ)
        others = range(len(DIL_DILS) - 1)
        ms = [m] + [m_s[i, rows, :] for i in others]
        dens = [l] + [l_s[i, rows, :] for i in others]
        nums = [num] + [o_s[i, rows, :] for i in others]
        top = functools.reduce(jnp.maximum, ms)
        w = [jnp.exp2(x - top) for x in ms]
        num_tot = sum(wi * ni for wi, ni in zip(w, nums))
        den_tot = sum(wi * di for wi, di in zip(w, dens))
        g = g_ref[rows, :].astype(F32)
        o_ref[rows, :] = (num_tot / den_tot * (g * jax.nn.sigmoid(g))).astype(o_ref.dtype)

    for pair in reversed(range(len(DIL_DILS))):
        n = DIL_SB // DIL_DILS[pair] // DIL_TQ
        for r in range(DIL_DILS[pair]):
            for c in range(n):
                t = (pair, r, c, n)
                finish(t, *scores(t))


def _dil_attention(qk_views, v_views, gate, batch, seq):
    tokens = gate.shape[0]
    nsb = seq // DIL_SB
    halo = DIL_SIDE
    specs_main, specs_halo = [], []
    for dil in DIL_DILS:
        rows, width = tokens // dil, dil * LANES
        main_rows = DIL_SB // dil
        per_main = main_rows // halo
        last = rows // halo - 1

        def main_spec(h0, main_rows=main_rows, width=width):
            return pl.BlockSpec((None, main_rows, width), lambda b, h, s: (h0 + h, b * nsb + s, 0))

        def prev_spec(h0, per_main=per_main, width=width):
            return pl.BlockSpec((None, halo, width),
                                lambda b, h, s: (h0 + h, jnp.maximum((b * nsb + s) * per_main - 1, 0), 0))

        def next_spec(h0, per_main=per_main, width=width, last=last):
            return pl.BlockSpec((None, halo, width),
                                lambda b, h, s: (h0 + h, jnp.minimum((b * nsb + s + 1) * per_main, last), 0))

        specs_main.append(main_spec)
        specs_halo.append((prev_spec, next_spec))
    in_specs = [specs_main[p](0) for p in range(3)]
    operands = list(qk_views)
    for h0, views in ((DIL_HEADS, qk_views), (0, v_views)):
        for p in range(3):
            in_specs += [specs_main[p](h0), specs_halo[p][0](h0), specs_halo[p][1](h0)]
            operands += [views[p]] * 3
    in_specs.append(pl.BlockSpec((DIL_SB, LANES), lambda b, h, s: (b * nsb + s, h)))
    operands.append(gate)
    return pl.pallas_call(
        functools.partial(_dil_kernel, nsb=nsb),
        grid=(batch, DIL_HEADS, nsb),
        in_specs=in_specs,
        out_specs=pl.BlockSpec((DIL_SB, LANES), lambda b, h, s: (b * nsb + s, h)),
        out_shape=jax.ShapeDtypeStruct((tokens, WC), BF16),
        scratch_shapes=[pltpu.VMEM((4, DIL_TQ, DIL_TK), F32),
                        ] + [pltpu.VMEM((len(DIL_DILS) - 1, DIL_SB, LANES), F32)] * 3,
        compiler_params=_params(3),
    )(*operands)


def _rope_tables(seq, half):
    inv = ROPE_THETA ** (-jnp.arange(half, dtype=F32) / half)
    ang = jnp.arange(seq, dtype=F32)[:, None] * inv[None, :]
    cos, sin = jnp.cos(ang), jnp.sin(ang)
    pad = jnp.zeros((seq, LANES - 2 * half), F32)
    return jnp.concatenate([cos, cos, pad], 1), jnp.concatenate([-sin, sin, pad], 1)


def _prep_weights(ab_w_in, ab_w_q_up, ab_w_kv_up, ab_w_out, c_w_in, c_w_out):
    def cast_scaling_q(w, q_cols, scale):
        col = lax.broadcasted_iota(jnp.int32, (1, w.shape[1]), 1)
        return (w * jnp.where(col < q_cols, scale, 1.0)).astype(BF16)

    w0 = cast_scaling_q(ab_w_in, WA, HEAD_DIM ** -0.5 * LOG2E)
    tail0 = 4 * WA + MLA_Q_RANK + MLA_KV_RANK
    w_tail = jnp.concatenate(
        [w0[:, tail0 + MLA_ROPE:], w0[:, tail0:tail0 + MLA_ROPE],
         jnp.zeros((D_MODEL, IN0_TAIL_WIDTH - WB - MLA_ROPE), BF16)], axis=1)
    w_in0 = (w0, w_tail, w0[:, 2 * WA:3 * WA].T)
    wq = (ab_w_q_up * ((MLA_NOPE + MLA_ROPE) ** -0.5 * LOG2E)).reshape(
        MLA_Q_RANK, MLA_HEADS, MLA_NOPE + MLA_ROPE)
    wq = jnp.pad(wq, ((0, 0), (0, 0), (0, MLA_QK_PAD - MLA_NOPE - MLA_ROPE)))
    w_q = wq.reshape(MLA_Q_RANK, MLA_HEADS * MLA_QK_PAD).astype(BF16)
    w_kv = ab_w_kv_up.astype(BF16)
    w_in1 = cast_scaling_q(c_w_in, WC, HEAD_DIM ** -0.5 * LOG2E)
    return w_in0, w_q, w_kv, ab_w_out.astype(BF16), w_in1, c_w_out.astype(BF16)


def _trunk(x, w, tabs):
    batch, seq, _ = x.shape
    (w_in0, w_q, w_kv, w_out0, w_in1, w_out1, na_bias, q_g, kv_g,
     ln0_g, ln0_b, ln1_g, ln1_b) = w
    cos64, sin64, cos128, sin128 = tabs
    x2 = x.reshape(batch * seq, D_MODEL)
    w0, w_tail, w_va_t = w_in0
    proj0, xb = _proj_f32(x2, w0, tm=1024, tn=IN0_TN, w_blocks=IN0_MAIN_BLOCKS)
    tail0 = _proj_bf16(xb, w_tail, tm=1024, tn=IN0_TAIL_WIDTH, col0=0, n=IN0_TAIL_WIDTH)
    vt_na = _proj_t(w_va_t, xb, batch, seq, tm=1024)
    ya = _na_attention(proj0, vt_na, na_bias, batch, seq)
    q_all, k_all, vt_all = _mla_up(proj0, tail0, q_g, w_q, kv_g, w_kv, cos64, sin64, seq, tm=512)
    yb = _mla_attention(q_all, k_all, vt_all, tail0, batch, seq, tq=1024, tk=1024)
    x1, x1b = _out_ln(ya, 0, yb, 0, w_out0, x2, ln0_g, ln0_b, tm=512)
    proj1 = functools.partial(_proj_heads, x1b, w_in1, cos_t=cos128, sin_t=sin128, seq=seq,
                              tm=1024, tn=1024, dils=DIL_DILS)
    qk1 = proj1(col0=0, n=2 * WC, rope=True)
    v1 = proj1(col0=2 * WC, n=WC, rope=False)
    gate1 = _proj_bf16(x1b, w_in1, tm=1024, tn=1024, col0=3 * WC, n=WC)
    yc = _dil_attention(qk1, v1, gate1, batch, seq)
    y, _ = _out_ln(yc, 0, yc, 1, w_out1, x1, ln1_g, ln1_b, tm=512)
    return y.reshape(batch, seq, D_MODEL)


def kernel(x_prompt, x_sample, ab_w_in, ab_rpb, ab_q_norm_g, ab_w_q_up, ab_kv_norm_g, ab_w_kv_up,
           ab_w_out, ab_ln_g, ab_ln_b, c_w_in, c_w_out, c_ln_g, c_ln_b):
    w_in0, w_q, w_kv, w_out0, w_in1, w_out1 = _prep_weights(
        ab_w_in, ab_w_q_up, ab_w_kv_up, ab_w_out, c_w_in, c_w_out)
    row = lambda a: a.reshape(1, -1).astype(F32)
    w = (w_in0, w_q, w_kv, w_out0, w_in1, w_out1, _na_bias(ab_rpb), row(ab_q_norm_g), row(ab_kv_norm_g),
         row(ab_ln_g), row(ab_ln_b), row(c_ln_g), row(c_ln_b))
    max_seq = max(x_prompt.shape[1], x_sample.shape[1])
    tabs = _rope_tables(max_seq, MLA_ROPE // 2) + _rope_tables(max_seq, HEAD_DIM // 2)
    return tuple(_trunk(x, w, tabs) for x in (x_prompt, x_sample))
```

```python
import functools
import math

import jax
import jax.numpy as jnp
from jax import lax
from jax.experimental import pallas as pl
from jax.experimental.pallas import tpu as pltpu

F32 = jnp.float32
BF16 = jnp.bfloat16

D_MODEL = 2048
DEPTH = 2
GRID_W = 64
HEAD_DIM = 128
NA_HEADS = 8
NA_WIN_ROWS = 8
NA_WIN_COLS = 16
MLA_HEADS = 8
MLA_Q_RANK = 512
MLA_KV_RANK = 512
MLA_NOPE = 128
MLA_ROPE = 64
MLA_V = 128
DIL_HEADS = 16
DIL_PAIRS = ((128, 1), (512, 4), (2048, 16))
DIL_HALF = 1024
WA = NA_HEADS * HEAD_DIM
WB = MLA_HEADS * MLA_V
WC = DIL_HEADS * HEAD_DIM
ROPE_THETA = 10000.0
ALPHA = (2 * DEPTH) ** 0.25
LN_EPS = 1e-5
RMS_EPS = 1e-6
NEG = -1e30
LOG2E = 1.4426950408889634

LANES = 128
MLA_QK_PAD = 2 * LANES
MXU_WIDTH = 256
VMEM_LIMIT = 56 * 1024 * 1024

IN0_TN = WA
IN0_MAIN_BLOCKS = (0, 1, 3, 4)
IN0_TAIL_WIDTH = -(-(WB + MLA_ROPE) // MXU_WIDTH) * MXU_WIDTH
assert MLA_Q_RANK + MLA_KV_RANK == IN0_TN

C0_Q, C0_K, C0_G = 0, 8, 16
C0_QLAT, C0_KVLAT = 6, 7
C0_GB = 0
C0_KROPE = WB // LANES

NA_QROWS = 4
NA_KROWS = NA_QROWS + NA_WIN_ROWS
NA_TQ = NA_QROWS * GRID_W
NA_TK = NA_KROWS * GRID_W
NA_KPARTS = NA_KROWS // NA_QROWS
NA_SUB = 4
NA_GROUP = 4
assert (NA_WIN_ROWS // 2) % NA_QROWS == 0 and NA_KROWS % NA_QROWS == 0


def _params(n_axes):
    return pltpu.CompilerParams(dimension_semantics=("arbitrary",) * n_axes,
                                vmem_limit_bytes=VMEM_LIMIT)


def _proj_f32_kernel(x_ref, w_ref, o_ref, xb_ref):
    @pl.when(pl.program_id(1) == 0)
    def _():
        xb_ref[...] = x_ref[...].astype(BF16)

    o_ref[...] = jnp.dot(xb_ref[...], w_ref[...], preferred_element_type=F32).astype(o_ref.dtype)


def _proj_f32(x, w, tm, tn, w_blocks):
    m, k = x.shape
    n_blocks = len(w_blocks)
    lo, skip = w_blocks[0], [b for b in range(w_blocks[0], w_blocks[-1]) if b not in w_blocks]
    assert len(skip) <= 1 and list(w_blocks) == sorted(w_blocks)

    def w_index(i, j):
        jj = j + lo
        return (0, jj + (jj >= skip[0]) if skip else jj)

    return pl.pallas_call(
        _proj_f32_kernel,
        grid=(m // tm, n_blocks),
        in_specs=[pl.BlockSpec((tm, k), lambda i, j: (i, 0)),
                  pl.BlockSpec((k, tn), w_index)],
        out_specs=[pl.BlockSpec((tm, tn), lambda i, j: (i, j)),
                   pl.BlockSpec((tm, k), lambda i, j: (i, 0))],
        out_shape=[jax.ShapeDtypeStruct((m, n_blocks * tn), BF16), jax.ShapeDtypeStruct((m, k), BF16)],
        compiler_params=_params(2),
    )(x, w)


def _proj_t_kernel(w_ref, x_ref, o_ref):
    dn = (((1,), (1,)), ((), ()))
    o_ref[...] = lax.dot_general(w_ref[...], x_ref[...], dn,
                                 preferred_element_type=F32).astype(o_ref.dtype)


def _proj_t(w_t, x, batch, seq, tm):
    n, k = w_t.shape
    per_seq = seq // tm
    return pl.pallas_call(
        _proj_t_kernel,
        grid=(batch * per_seq,),
        in_specs=[pl.BlockSpec((n, k), lambda i: (0, 0)),
                  pl.BlockSpec((tm, k), lambda i: (i, 0))],
        out_specs=pl.BlockSpec((None, n, tm), lambda i: (i // per_seq, 0, i % per_seq)),
        out_shape=jax.ShapeDtypeStruct((batch, n, seq), BF16),
        compiler_params=_params(1),
    )(w_t, x)


def _proj_bf16_kernel(x_ref, w_ref, o_ref):
    o_ref[...] = jnp.dot(x_ref[...], w_ref[...], preferred_element_type=F32).astype(o_ref.dtype)


def _proj_bf16(x, w, tm, tn, col0, n):
    m, k = x.shape
    blk0 = col0 // tn
    return pl.pallas_call(
        _proj_bf16_kernel,
        grid=(m // tm, n // tn),
        in_specs=[pl.BlockSpec((tm, k), lambda i, j: (i, 0)),
                  pl.BlockSpec((k, tn), lambda i, j: (0, blk0 + j))],
        out_specs=pl.BlockSpec((tm, tn), lambda i, j: (i, j)),
        out_shape=jax.ShapeDtypeStruct((m, n), BF16),
        compiler_params=_params(2),
    )(x, w)


PROJ_SPLIT = 2


def _proj_heads_kernel(x_ref, w_ref, cos_ref, sin_ref, *rest, rope, dils):
    o_refs, scrs = rest[:len(dils)], rest[len(dils):]
    tg = x_ref.shape[0] // PROJ_SPLIT

    def project(g):
        return jnp.dot(x_ref[g * tg:(g + 1) * tg, :], w_ref[...], preferred_element_type=F32)

    def emit(g, y):
        for h in range(y.shape[1] // LANES):
            yh = y[:, h * LANES:(h + 1) * LANES]
            if rope:
                rows = slice(g * tg, (g + 1) * tg)
                yh = yh * cos_ref[rows, :] + pltpu.roll(yh, LANES // 2, 1) * sin_ref[rows, :]
            o_refs[0][h, g * tg:(g + 1) * tg, :] = yh.astype(o_refs[0].dtype)
            scrs[0][h, 0, g * tg:(g + 1) * tg, :] = yh
            for lvl in range(1, len(dils)):
                d_prev, d = dils[lvl - 1], dils[lvl]
                q, n = d // d_prev, tg // d
                for r_prev in range(d_prev):
                    for a in range(q):
                        part = scrs[lvl - 1][h, r_prev, pl.ds(g * tg // d_prev + a, n, stride=q), :]
                        r = d_prev * a + r_prev
                        o_refs[lvl][h, g * n:(g + 1) * n, r * LANES:(r + 1) * LANES] = (
                            part.astype(o_refs[lvl].dtype))
                        if lvl + 1 < len(dils):
                            scrs[lvl][h, r, g * n:(g + 1) * n, :] = part

    y_prev = project(0)
    for g in range(1, PROJ_SPLIT):
        y_next = project(g)
        emit(g - 1, y_prev)
        y_prev = y_next
    emit(PROJ_SPLIT - 1, y_prev)


def _proj_heads(x, w, col0, n, cos_t, sin_t, seq, tm, tn, rope, dils):
    m, k = x.shape
    blk0 = col0 // tn
    pos_blocks = seq // tm
    hb = tn // LANES
    return pl.pallas_call(
        functools.partial(_proj_heads_kernel, rope=rope, dils=dils),
        grid=(m // tm, n // tn),
        in_specs=[pl.BlockSpec((tm, k), lambda i, j: (i, 0)),
                  pl.BlockSpec((k, tn), lambda i, j: (0, blk0 + j)),
                  pl.BlockSpec((tm, LANES), lambda i, j: (i % pos_blocks, 0)),
                  pl.BlockSpec((tm, LANES), lambda i, j: (i % pos_blocks, 0))],
        out_specs=[pl.BlockSpec((hb, tm // d, d * LANES), lambda i, j: (j, i, 0)) for d in dils],
        out_shape=[jax.ShapeDtypeStruct((n // LANES, m // d, d * LANES), BF16) for d in dils],
        scratch_shapes=[pltpu.VMEM((hb, d, tm // d, LANES), F32) for d in dils[:-1]],
        compiler_params=_params(2),
    )(x, w, cos_t, sin_t)


def _na_bias_kernel(rpb_ref, o_ref, t_ref):
    h = pl.program_id(0)
    n_dc = 2 * NA_WIN_COLS - 1
    kc = lax.broadcasted_iota(jnp.int32, (GRID_W, GRID_W), 0)
    c = lax.broadcasted_iota(jnp.int32, (GRID_W, GRID_W), 1)
    d = kc - c + (NA_WIN_COLS - 1)
    cs = jnp.clip(c - NA_WIN_COLS // 2, 0, GRID_W - NA_WIN_COLS)
    col_ok = (kc >= cs) & (kc < cs + NA_WIN_COLS)
    for dr in range(2 * NA_WIN_ROWS - 1):
        t = jnp.zeros((GRID_W, GRID_W), F32)
        for dd in range(n_dc):
            t = jnp.where(d == dd, rpb_ref[h, dr * n_dc + dd] * LOG2E, t)
        t_ref[dr] = jnp.where(col_ok, t, NEG)
    neg_blk = jnp.full((GRID_W, GRID_W), NEG, F32)
    for typ in range(3):
        for a in range(NA_QROWS):
            half = NA_WIN_ROWS // 2
            if typ == 0:
                lo, dr0 = max(a - half, 0), NA_WIN_ROWS - 1 - a
            elif typ == 1:
                lo, dr0 = a, NA_WIN_ROWS - 1 - half - a
            else:
                lo, dr0 = NA_QROWS + min(a - NA_QROWS + half, 0), -1 - a
            for kb in range(NA_KROWS):
                ok = lo <= kb < lo + NA_WIN_ROWS
                blk = t_ref[kb + dr0] if ok else neg_blk
                o_ref[0, typ, kb * GRID_W:(kb + 1) * GRID_W, a * GRID_W:(a + 1) * GRID_W] = blk


def _na_bias(rpb):
    rpb2 = rpb.reshape(NA_HEADS, -1)
    return pl.pallas_call(
        _na_bias_kernel,
        grid=(NA_HEADS,),
        in_specs=[pl.BlockSpec(memory_space=pltpu.SMEM)],
        out_specs=pl.BlockSpec((1, 3, NA_TK, NA_TQ), lambda h: (h, 0, 0, 0)),
        out_shape=jax.ShapeDtypeStruct((NA_HEADS, 3, NA_TK, NA_TQ), F32),
        scratch_shapes=[pltpu.VMEM((2 * NA_WIN_ROWS - 1, GRID_W, GRID_W), F32)],
        compiler_params=_params(1),
    )(rpb2)


def _na_kernel(q_ref, *refs):
    k_refs = [refs[NA_KPARTS * s:NA_KPARTS * (s + 1)] for s in range(NA_SUB)]
    vt_refs = [refs[NA_KPARTS * (NA_SUB + s):NA_KPARTS * (NA_SUB + s + 1)] for s in range(NA_SUB)]
    g_ref = refs[2 * NA_KPARTS * NA_SUB]
    b_refs = refs[2 * NA_KPARTS * NA_SUB + 1:2 * NA_KPARTS * NA_SUB + 1 + NA_SUB]
    o_ref = refs[-1]
    dn = (((1,), (1,)), ((), ()))
    blocks = [(e, s) for e in range(q_ref.shape[0]) for s in range(NA_SUB)]
    sts = []
    for e, s in blocks:
        k = jnp.concatenate([r[e] for r in k_refs[s]], axis=0)
        q = q_ref[e, s * NA_TQ:(s + 1) * NA_TQ, :]
        sts.append(lax.dot_general(k, q, dn, preferred_element_type=F32) + b_refs[s][0, 0])
    for (e, s), st in zip(blocks, sts):
        vt = jnp.concatenate([r[e] for r in vt_refs[s]], axis=1)
        m = jnp.max(st, axis=0, keepdims=True)
        p = jnp.exp2(st - m)
        l = jnp.sum(p, axis=0, keepdims=True)
        ot = jnp.dot(vt, p.astype(BF16), preferred_element_type=F32)
        g = g_ref[e, s * NA_TQ:(s + 1) * NA_TQ, :].astype(F32)
        o_ref[e, s * NA_TQ:(s + 1) * NA_TQ, :] = (
            (ot * (1.0 / l)).T * (g * jax.nn.sigmoid(g))).astype(o_ref.dtype)


def _na_attention(proj, vt_na, bias, batch, seq):
    rows = seq // GRID_W
    n_blocks = rows // NA_QROWS
    kq = NA_TK // NA_KPARTS
    proj3 = proj.reshape(batch, seq, proj.shape[1])
    group = math.gcd(batch, NA_GROUP)

    def kbase(jj):
        return jnp.clip(jj - (NA_WIN_ROWS // 2) // NA_QROWS, 0, n_blocks - NA_KPARTS)

    def btype(jj):
        return jnp.where(jj == 0, 0, jnp.where(jj == n_blocks - 1, 2, 1))

    def k_spec(s, part):
        return pl.BlockSpec((group, kq, LANES),
                            lambda h, j, b: (b, kbase(NA_SUB * j + s) + part, C0_K + h))

    def vt_spec(s, part):
        return pl.BlockSpec((group, HEAD_DIM, kq), lambda h, j, b: (b, h, kbase(NA_SUB * j + s) + part))

    def b_spec(s):
        return pl.BlockSpec((1, 1, NA_TK, NA_TQ), lambda h, j, b: (h, btype(NA_SUB * j + s), 0, 0))

    tok_spec = lambda col0: pl.BlockSpec((group, NA_SUB * NA_TQ, LANES), lambda h, j, b: (b, j, col0 + h))
    parts = [(s, i) for s in range(NA_SUB) for i in range(NA_KPARTS)]
    in_specs = ([tok_spec(C0_Q)] + [k_spec(s, i) for s, i in parts] + [vt_spec(s, i) for s, i in parts]
                + [tok_spec(C0_G)] + [b_spec(s) for s in range(NA_SUB)])
    out = pl.pallas_call(
        _na_kernel,
        grid=(NA_HEADS, n_blocks // NA_SUB, batch // group),
        in_specs=in_specs,
        out_specs=pl.BlockSpec((group, NA_SUB * NA_TQ, LANES), lambda h, j, b: (b, j, h)),
        out_shape=jax.ShapeDtypeStruct((batch, seq, WA), BF16),
        compiler_params=_params(3),
    )(proj3, *([proj3] * len(parts)), *([vt_na] * len(parts)), proj3, *([bias] * NA_SUB))
    return out.reshape(batch * seq, WA)


def _rms(lat_ref, g_ref):
    x = lat_ref[...].astype(F32)
    ms = jnp.mean(x * x, axis=1, keepdims=True)
    return (x * lax.rsqrt(ms + RMS_EPS) * g_ref[...]).astype(BF16)


def _rope64(t, cos_t, sin_t):
    lane = lax.broadcasted_iota(jnp.int32, t.shape, 1)
    half = MLA_ROPE // 2
    partner = jnp.where(lane < half, pltpu.roll(t, LANES - half, 1), pltpu.roll(t, half, 1))
    return t * cos_t + partner * sin_t


def _mla_q_up_kernel(lat_ref, g_ref, w_ref, cos_ref, sin_ref, o_ref):
    y = jnp.dot(_rms(lat_ref, g_ref), w_ref[...], preferred_element_type=F32)
    c = cos_ref[...]
    s = sin_ref[...]
    for h in range(MLA_HEADS):
        base = h * MLA_QK_PAD
        o_ref[:, base:base + LANES] = y[:, base:base + LANES].astype(o_ref.dtype)
        o_ref[:, base + LANES:base + 2 * LANES] = _rope64(
            y[:, base + LANES:base + 2 * LANES], c, s).astype(o_ref.dtype)


def _mla_kv_up_kernel(lat_ref, g_ref, w_ref, kr_ref, cos_ref, sin_ref, k_ref, vt_ref):
    y = jnp.dot(_rms(lat_ref, g_ref), w_ref[...], preferred_element_type=F32)
    k_pe = _rope64(kr_ref[...].astype(F32), cos_ref[...], sin_ref[...]).astype(k_ref.dtype)
    for h in range(MLA_HEADS):
        base = h * (MLA_NOPE + MLA_V)
        k_ref[:, h * MLA_QK_PAD:h * MLA_QK_PAD + LANES] = y[:, base:base + MLA_NOPE].astype(k_ref.dtype)
        k_ref[:, h * MLA_QK_PAD + LANES:(h + 1) * MLA_QK_PAD] = k_pe
        vt_ref[h * MLA_V:(h + 1) * MLA_V, :] = y[:, base + MLA_NOPE:base + MLA_NOPE + MLA_V].T.astype(vt_ref.dtype)


def _mla_up(proj, tail, q_g, w_q, kv_g, w_kv, cos_t, sin_t, seq, tm):
    m = proj.shape[0]
    pos_blocks = seq // tm
    tab = pl.BlockSpec((tm, LANES), lambda i: (i % pos_blocks, 0))
    full = lambda a: pl.BlockSpec(a.shape, lambda i: (0,) * a.ndim)
    q_all = pl.pallas_call(
        _mla_q_up_kernel,
        grid=(m // tm,),
        in_specs=[pl.BlockSpec((tm, MLA_Q_RANK), lambda i: (i, C0_QLAT)), full(q_g), full(w_q), tab, tab],
        out_specs=pl.BlockSpec((tm, MLA_HEADS * MLA_QK_PAD), lambda i: (i, 0)),
        out_shape=jax.ShapeDtypeStruct((m, MLA_HEADS * MLA_QK_PAD), BF16),
        compiler_params=_params(1),
    )(proj, q_g, w_q, cos_t, sin_t)
    k_all, vt_all = pl.pallas_call(
        _mla_kv_up_kernel,
        grid=(m // tm,),
        in_specs=[pl.BlockSpec((tm, MLA_KV_RANK), lambda i: (i, C0_KVLAT)), full(kv_g), full(w_kv),
                  pl.BlockSpec((tm, LANES), lambda i: (i, C0_KROPE)), tab, tab],
        out_specs=[pl.BlockSpec((tm, MLA_HEADS * MLA_QK_PAD), lambda i: (i, 0)),
                   pl.BlockSpec((WB, tm), lambda i: (0, i))],
        out_shape=[jax.ShapeDtypeStruct((m, MLA_HEADS * MLA_QK_PAD), BF16),
                   jax.ShapeDtypeStruct((WB, m), BF16)],
        compiler_params=_params(1),
    )(proj, kv_g, w_kv, tail, cos_t, sin_t)
    return q_all, k_all, vt_all


MLA_UNROLL = 8


def _mla_attn_kernel(q_ref, k_ref, vt_ref, g_ref, o_ref, s0, s1, m_s, l_s, acc_s, *, tk):
    q = q_ref[...]
    tq = q.shape[0]
    n = k_ref.shape[0] // tk
    dn = (((1,), (1,)), ((), ()))

    def scores(c):
        start = pl.multiple_of(c * tk, tk)
        return lax.dot_general(k_ref[pl.ds(start, tk), :], q, dn, preferred_element_type=F32)

    def update(c, st):
        vt = vt_ref[:, pl.ds(pl.multiple_of(c * tk, tk), tk)]
        m = m_s[...]
        m_new = jnp.maximum(m, jnp.max(st, axis=0, keepdims=True))
        alpha = jnp.exp2(m - m_new)
        p = jnp.exp2(st - m_new)
        m_s[...] = m_new
        l_s[...] = alpha * l_s[...] + jnp.sum(p, axis=0, keepdims=True)
        acc_s[...] = alpha * acc_s[...] + jnp.dot(vt, p.astype(BF16), preferred_element_type=F32)

    m_s[...] = jnp.full(m_s.shape, NEG, F32)
    l_s[...] = jnp.zeros(l_s.shape, F32)
    acc_s[...] = jnp.zeros(acc_s.shape, F32)
    s0[...] = scores(0)

    bufs = (s0, s1)
    unroll = math.gcd(MLA_UNROLL, n)

    def body(i, carry):
        c0 = unroll * i
        for u in range(unroll):
            bufs[(u + 1) % 2][...] = scores(jnp.minimum(c0 + u + 1, n - 1))
            update(c0 + u, bufs[u % 2][...])
        return carry

    lax.fori_loop(0, n // unroll, body, 0)
    g = g_ref[...].astype(F32)
    o_ref[...] = ((acc_s[...] * (1.0 / l_s[...])).T * (g * jax.nn.sigmoid(g))).astype(o_ref.dtype)


def _mla_attention(q_all, k_all, vt_all, proj, batch, seq, tq, tk):
    nq = seq // tq
    n = seq // tk
    assert n % 2 == 0 and MLA_UNROLL % 2 == 0
    return pl.pallas_call(
        functools.partial(_mla_attn_kernel, tk=tk),
        grid=(batch, MLA_HEADS, nq),
        in_specs=[pl.BlockSpec((tq, MLA_QK_PAD), lambda b, h, i: (b * nq + i, h)),
                  pl.BlockSpec((seq, MLA_QK_PAD), lambda b, h, i: (b, h)),
                  pl.BlockSpec((MLA_V, seq), lambda b, h, i: (h, b)),
                  pl.BlockSpec((tq, LANES), lambda b, h, i: (b * nq + i, C0_GB + h))],
        out_specs=pl.BlockSpec((tq, MLA_V), lambda b, h, i: (b * nq + i, h)),
        out_shape=jax.ShapeDtypeStruct((batch * seq, WB), BF16),
        scratch_shapes=[pltpu.VMEM((tk, tq), F32), pltpu.VMEM((tk, tq), F32), pltpu.VMEM((1, tq), F32),
                        pltpu.VMEM((1, tq), F32), pltpu.VMEM((MLA_V, tq), F32)],
        compiler_params=_params(3),
    )(q_all, k_all, vt_all, proj)


OUT_LN_SPLIT = 2


def _out_ln_kernel(y1_ref, y2_ref, w_ref, x_ref, g_ref, b_ref, o_ref, ob_ref):
    half = y1_ref.shape[1]
    tm = y1_ref.shape[0]
    sub = tm // OUT_LN_SPLIT

    def project(rows):
        y = jnp.dot(y1_ref[rows, :], w_ref[:half, :], preferred_element_type=F32)
        return y + jnp.dot(y2_ref[rows, :], w_ref[half:, :], preferred_element_type=F32)

    def norm(rows, y):
        z = ALPHA * x_ref[rows, :] + y
        mu = jnp.mean(z, axis=1, keepdims=True)
        zc = z - mu
        var = jnp.mean(zc * zc, axis=1, keepdims=True)
        out = zc * lax.rsqrt(var + LN_EPS) * g_ref[...] + b_ref[...]
        o_ref[rows, :] = out
        ob_ref[rows, :] = out.astype(ob_ref.dtype)

    rows = [slice(i * sub, (i + 1) * sub) for i in range(OUT_LN_SPLIT)]
    y_prev = project(rows[0])
    for i in range(1, OUT_LN_SPLIT):
        y_next = project(rows[i])
        norm(rows[i - 1], y_prev)
        y_prev = y_next
    norm(rows[-1], y_prev)


def _out_ln(y1, y1_col, y2, y2_col, w, x, g, b, tm):
    m, d = x.shape
    half = w.shape[0] // 2
    full = lambda a: pl.BlockSpec(a.shape, lambda i: (0,) * a.ndim)
    return pl.pallas_call(
        _out_ln_kernel,
        grid=(m // tm,),
        in_specs=[pl.BlockSpec((tm, half), lambda i: (i, y1_col)),
                  pl.BlockSpec((tm, half), lambda i: (i, y2_col)),
                  full(w), pl.BlockSpec((tm, d), lambda i: (i, 0)), full(g), full(b)],
        out_specs=[pl.BlockSpec((tm, d), lambda i: (i, 0)), pl.BlockSpec((tm, d), lambda i: (i, 0))],
        out_shape=[jax.ShapeDtypeStruct((m, d), F32), jax.ShapeDtypeStruct((m, d), BF16)],
        compiler_params=_params(1),
    )(y1, y2, w, x, g, b)


DIL_DILS = tuple(d for _, d in DIL_PAIRS)
DIL_SIDE = DIL_PAIRS[0][0] // 2 // DIL_PAIRS[0][1]
DIL_TQ = 128
DIL_TK = DIL_TQ + 2 * DIL_SIDE
DIL_SB_UNIT = DIL_TQ * DIL_DILS[-1]
DIL_SB_MAX = 2 * DIL_SB_UNIT
assert all(w // 2 // d == DIL_SIDE for w, d in DIL_PAIRS)


def _dil_kernel(*refs, nsb, sb_tokens):
    q_refs = refs[0:3]
    k_refs = [refs[3 + 3 * p:6 + 3 * p] for p in range(3)]
    v_refs = [refs[12 + 3 * p:15 + 3 * p] for p in range(3)]
    g_ref, o_ref, bias_s, o_s, m_s, l_s = refs[21:]
    sb = pl.program_id(2)
    dn = (((1,), (1,)), ((), ()))

    qi = lax.broadcasted_iota(jnp.int32, (DIL_TQ, DIL_TK), 0)
    kj = lax.broadcasted_iota(jnp.int32, (DIL_TQ, DIL_TK), 1)
    base = jnp.where((kj >= qi) & (kj <= qi + 2 * DIL_SIDE), 0.0, NEG).astype(F32)
    lo = jnp.where(kj < DIL_SIDE, NEG, 0.0) * (sb == 0).astype(F32)
    hi = jnp.where(kj >= DIL_TK - DIL_SIDE, NEG, 0.0) * (sb == nsb - 1).astype(F32)
    bias_s[0] = base
    bias_s[1] = base + lo
    bias_s[2] = base + hi
    bias_s[3] = base + lo + hi

    def window(trio, lanes, c, n):
        main, prev, nxt = trio
        if n == 1:
            return jnp.concatenate([prev[:, lanes], main[:, lanes], nxt[:, lanes]], axis=0), 3
        if c == 0:
            return jnp.concatenate([prev[:, lanes], main[0:DIL_TK - DIL_SIDE, lanes]], axis=0), 1
        if c == n - 1:
            return jnp.concatenate([main[n * DIL_TQ - (DIL_TK - DIL_SIDE):n * DIL_TQ, lanes],
                                    nxt[:, lanes]], axis=0), 2
        return main[c * DIL_TQ - DIL_SIDE:c * DIL_TQ + DIL_TQ + DIL_SIDE, lanes], 0

    def scores(t):
        pair, r, c, n = t
        lanes = slice(r * LANES, (r + 1) * LANES)
        kw, mask_id = window(k_refs[pair], lanes, c, n)
        q = q_refs[pair][c * DIL_TQ:(c + 1) * DIL_TQ, lanes]
        return lax.dot_general(q, kw, dn, preferred_element_type=F32), mask_id

    def finish(t, s, mask_id):
        pair, r, c, n = t
        dil = DIL_DILS[pair]
        vw, _ = window(v_refs[pair], slice(r * LANES, (r + 1) * LANES), c, n)
        s = s + bias_s[mask_id]
        m = jnp.max(s, axis=1, keepdims=True)
        p = jnp.exp2(s - m)
        l = jnp.sum(p, axis=1, keepdims=True)
        num = jnp.dot(p.astype(BF16), vw, preferred_element_type=F32)
        m = jnp.broadcast_to(m, (DIL_TQ, LANES))
        l = jnp.broadcast_to(l, (DIL_TQ, LANES))
        start = (c * DIL_TQ) * dil + r
        if dil > 1:
            rows = pl.ds(start, DIL_TQ, stride=dil)
            o_s[pair - 1, rows, :] = num
            m_s[pair - 1, rows, :] = m
            l_s[pair - 1, rows, :] = l
            return
        rows = pl.ds(start, DIL_TQ)
        others = range(len(DIL_DILS) - 1)
        ms = [m] + [m_s[i, rows, :] for i in others]
        dens = [l] + [l_s[i, rows, :] for i in others]
        nums = [num] + [o_s[i, rows, :] for i in others]
        top = functools.reduce(jnp.maximum, ms)
        w = [jnp.exp2(x - top) for x in ms]
        num_tot = sum(wi * ni for wi, ni in zip(w, nums))
        den_tot = sum(wi * di for wi, di in zip(w, dens))
        g = g_ref[rows, :].astype(F32)
        o_ref[rows, :] = (num_tot / den_tot * (g * jax.nn.sigmoid(g))).astype(o_ref.dtype)

    for pair in reversed(range(len(DIL_DILS))):
        n = sb_tokens // DIL_DILS[pair] // DIL_TQ
        for r in range(DIL_DILS[pair]):
            for c in range(n):
                t = (pair, r, c, n)
                finish(t, *scores(t))


def _dil_attention(qk_views, v_views, gate, batch, seq):
    tokens = gate.shape[0]
    sb_tokens = math.gcd(seq, DIL_SB_MAX)
    assert sb_tokens % DIL_SB_UNIT == 0
    nsb = seq // sb_tokens
    halo = DIL_SIDE
    specs_main, specs_halo = [], []
    for dil in DIL_DILS:
        rows, width = tokens // dil, dil * LANES
        main_rows = sb_tokens // dil
        per_main = main_rows // halo
        last = rows // halo - 1

        def main_spec(h0, main_rows=main_rows, width=width):
            return pl.BlockSpec((None, main_rows, width), lambda b, h, s: (h0 + h, b * nsb + s, 0))

        def prev_spec(h0, per_main=per_main, width=width):
            return pl.BlockSpec((None, halo, width),
                                lambda b, h, s: (h0 + h, jnp.maximum((b * nsb + s) * per_main - 1, 0), 0))

        def next_spec(h0, per_main=per_main, width=width, last=last):
            return pl.BlockSpec((None, halo, width),
                                lambda b, h, s: (h0 + h, jnp.minimum((b * nsb + s + 1) * per_main, last), 0))

        specs_main.append(main_spec)
        specs_halo.append((prev_spec, next_spec))
    in_specs = [specs_main[p](0) for p in range(3)]
    operands = list(qk_views)
    for h0, views in ((DIL_HEADS, qk_views), (0, v_views)):
        for p in range(3):
            in_specs += [specs_main[p](h0), specs_halo[p][0](h0), specs_halo[p][1](h0)]
            operands += [views[p]] * 3
    in_specs.append(pl.BlockSpec((sb_tokens, LANES), lambda b, h, s: (b * nsb + s, h)))
    operands.append(gate)
    return pl.pallas_call(
        functools.partial(_dil_kernel, nsb=nsb, sb_tokens=sb_tokens),
        grid=(batch, DIL_HEADS, nsb),
        in_specs=in_specs,
        out_specs=pl.BlockSpec((sb_tokens, LANES), lambda b, h, s: (b * nsb + s, h)),
        out_shape=jax.ShapeDtypeStruct((tokens, WC), BF16),
        scratch_shapes=[pltpu.VMEM((4, DIL_TQ, DIL_TK), F32),
                        ] + [pltpu.VMEM((len(DIL_DILS) - 1, sb_tokens, LANES), F32)] * 3,
        compiler_params=_params(3),
    )(*operands)


def _rope_tables(seq, half):
    inv = ROPE_THETA ** (-jnp.arange(half, dtype=F32) / half)
    ang = jnp.arange(seq, dtype=F32)[:, None] * inv[None, :]
    cos, sin = jnp.cos(ang), jnp.sin(ang)
    pad = jnp.zeros((seq, LANES - 2 * half), F32)
    return jnp.concatenate([cos, cos, pad], 1), jnp.concatenate([-sin, sin, pad], 1)


def _cast_scale_kernel(w_ref, o_ref, *, q_cols, scale):
    col = lax.broadcasted_iota(jnp.int32, (1, w_ref.shape[1]), 1)
    o_ref[...] = (w_ref[...] * jnp.where(col < q_cols, scale, 1.0)).astype(o_ref.dtype)


def _cast_scale(w, q_cols, scale, rows):
    r, c = w.shape
    return pl.pallas_call(
        functools.partial(_cast_scale_kernel, q_cols=q_cols, scale=scale),
        grid=(r // rows,),
        in_specs=[pl.BlockSpec((rows, c), lambda i: (i, 0))],
        out_specs=pl.BlockSpec((rows, c), lambda i: (i, 0)),
        out_shape=jax.ShapeDtypeStruct((r, c), BF16),
        compiler_params=_params(1),
    )(w)


def _prep_weights(ab_w_in, ab_w_q_up, ab_w_kv_up, ab_w_out, c_w_in, c_w_out):
    cast_scaling_q = functools.partial(_cast_scale, scale=HEAD_DIM ** -0.5 * LOG2E, rows=256)
    w0 = cast_scaling_q(ab_w_in, WA)
    tail0 = 4 * WA + MLA_Q_RANK + MLA_KV_RANK
    w_tail = jnp.concatenate(
        [w0[:, tail0 + MLA_ROPE:], w0[:, tail0:tail0 + MLA_ROPE],
         jnp.zeros((D_MODEL, IN0_TAIL_WIDTH - WB - MLA_ROPE), BF16)], axis=1)
    w_in0 = (w0, w_tail, w0[:, 2 * WA:3 * WA].T)
    wq = (ab_w_q_up * ((MLA_NOPE + MLA_ROPE) ** -0.5 * LOG2E)).reshape(
        MLA_Q_RANK, MLA_HEADS, MLA_NOPE + MLA_ROPE)
    wq = jnp.pad(wq, ((0, 0), (0, 0), (0, MLA_QK_PAD - MLA_NOPE - MLA_ROPE)))
    w_q = wq.reshape(MLA_Q_RANK, MLA_HEADS * MLA_QK_PAD).astype(BF16)
    w_kv = ab_w_kv_up.astype(BF16)
    w_in1 = cast_scaling_q(c_w_in, WC)
    return w_in0, w_q, w_kv, ab_w_out.astype(BF16), w_in1, c_w_out.astype(BF16)


def _trunk(x, w, tabs):
    batch, seq, _ = x.shape
    (w_in0, w_q, w_kv, w_out0, w_in1, w_out1, na_bias, q_g, kv_g,
     ln0_g, ln0_b, ln1_g, ln1_b) = w
    cos64, sin64, cos128, sin128 = tabs
    x2 = x.reshape(batch * seq, D_MODEL)
    w0, w_tail, w_va_t = w_in0
    proj0, xb = _proj_f32(x2, w0, tm=1024, tn=IN0_TN, w_blocks=IN0_MAIN_BLOCKS)
    tail0 = _proj_bf16(xb, w_tail, tm=1024, tn=IN0_TAIL_WIDTH, col0=0, n=IN0_TAIL_WIDTH)
    vt_na = _proj_t(w_va_t, xb, batch, seq, tm=1024)
    ya = _na_attention(proj0, vt_na, na_bias, batch, seq)
    q_all, k_all, vt_all = _mla_up(proj0, tail0, q_g, w_q, kv_g, w_kv, cos64, sin64, seq, tm=512)
    yb = _mla_attention(q_all, k_all, vt_all, tail0, batch, seq, tq=1024, tk=1024)
    x1, x1b = _out_ln(ya, 0, yb, 0, w_out0, x2, ln0_g, ln0_b, tm=512)
    proj1 = functools.partial(_proj_heads, x1b, w_in1, cos_t=cos128, sin_t=sin128, seq=seq,
                              tm=1024, tn=1024, dils=DIL_DILS)
    qk1 = proj1(col0=0, n=2 * WC, rope=True)
    v1 = proj1(col0=2 * WC, n=WC, rope=False)
    gate1 = _proj_bf16(x1b, w_in1, tm=1024, tn=1024, col0=3 * WC, n=WC)
    yc = _dil_attention(qk1, v1, gate1, batch, seq)
    y, _ = _out_ln(yc, 0, yc, 1, w_out1, x1, ln1_g, ln1_b, tm=512)
    return y.reshape(batch, seq, D_MODEL)


def kernel(x_prompt, x_sample, ab_w_in, ab_rpb, ab_q_norm_g, ab_w_q_up, ab_kv_norm_g, ab_w_kv_up,
           ab_w_out, ab_ln_g, ab_ln_b, c_w_in, c_w_out, c_ln_g, c_ln_b):
    w_in0, w_q, w_kv, w_out0, w_in1, w_out1 = _prep_weights(
        ab_w_in, ab_w_q_up, ab_w_kv_up, ab_w_out, c_w_in, c_w_out)
    row = lambda a: a.reshape(1, -1).astype(F32)
    w = (w_in0, w_q, w_kv, w_out0, w_in1, w_out1, _na_bias(ab_rpb), row(ab_q_norm_g), row(ab_kv_norm_g),
         row(ab_ln_g), row(ab_ln_b), row(c_ln_g), row(c_ln_b))
    max_seq = max(x_prompt.shape[1], x_sample.shape[1])
    tabs = _rope_tables(max_seq, MLA_ROPE // 2) + _rope_tables(max_seq, HEAD_DIM // 2)
    return tuple(_trunk(x, w, tabs) for x in (x_prompt, x_sample))
```

```python
import functools
import math

import jax
import jax.numpy as jnp
from jax import lax
from jax.experimental import pallas as pl
from jax.experimental.pallas import tpu as pltpu

F32 = jnp.float32
BF16 = jnp.bfloat16

D_MODEL = 2048
DEPTH = 2
GRID_W = 64
HEAD_DIM = 128
NA_HEADS = 8
NA_WIN_ROWS = 8
NA_WIN_COLS = 16
MLA_HEADS = 8
MLA_Q_RANK = 512
MLA_KV_RANK = 512
MLA_NOPE = 128
MLA_ROPE = 64
MLA_V = 128
DIL_HEADS = 16
DIL_PAIRS = ((128, 1), (512, 4), (2048, 16))
DIL_HALF = 1024
WA = NA_HEADS * HEAD_DIM
WB = MLA_HEADS * MLA_V
WC = DIL_HEADS * HEAD_DIM
ROPE_THETA = 10000.0
ALPHA = (2 * DEPTH) ** 0.25
LN_EPS = 1e-5
RMS_EPS = 1e-6
NEG = -1e30
LOG2E = 1.4426950408889634

LANES = 128
MLA_QK_PAD = 2 * LANES
MXU_WIDTH = 256
VMEM_LIMIT = 56 * 1024 * 1024

IN0_TN = WA
IN0_MAIN_BLOCKS = (0, 1, 3, 4)
IN0_TAIL_WIDTH = -(-(WB + MLA_ROPE) // MXU_WIDTH) * MXU_WIDTH
assert MLA_Q_RANK + MLA_KV_RANK == IN0_TN

C0_Q, C0_K, C0_G = 0, 8, 16
C0_QLAT, C0_KVLAT = 6, 7
C0_GB = 0
C0_KROPE = WB // LANES

NA_QROWS = 4
NA_KROWS = NA_QROWS + NA_WIN_ROWS
NA_TQ = NA_QROWS * GRID_W
NA_TK = NA_KROWS * GRID_W
NA_KPARTS = NA_KROWS // NA_QROWS
NA_SUB = 4
NA_GROUP = 4
assert (NA_WIN_ROWS // 2) % NA_QROWS == 0 and NA_KROWS % NA_QROWS == 0


def _params(n_axes):
    return pltpu.CompilerParams(dimension_semantics=("arbitrary",) * n_axes,
                                vmem_limit_bytes=VMEM_LIMIT)


def _proj_f32_kernel(x_ref, w_ref, o_ref, xb_ref):
    @pl.when(pl.program_id(1) == 0)
    def _():
        xb_ref[...] = x_ref[...].astype(BF16)

    o_ref[...] = lax.dot_general(xb_ref[...], w_ref[...], (((1,), (1,)), ((), ())),
                                 preferred_element_type=F32).astype(o_ref.dtype)


def _proj_f32(x, w_t, tm, tn, w_blocks):
    m, k = x.shape
    n_blocks = len(w_blocks)
    lo, skip = w_blocks[0], [b for b in range(w_blocks[0], w_blocks[-1]) if b not in w_blocks]
    assert len(skip) <= 1 and list(w_blocks) == sorted(w_blocks)

    def w_index(i, j):
        jj = j + lo
        return (jj + (jj >= skip[0]) if skip else jj, 0)

    return pl.pallas_call(
        _proj_f32_kernel,
        grid=(m // tm, n_blocks),
        in_specs=[pl.BlockSpec((tm, k), lambda i, j: (i, 0)),
                  pl.BlockSpec((tn, k), w_index)],
        out_specs=[pl.BlockSpec((tm, tn), lambda i, j: (i, j)),
                   pl.BlockSpec((tm, k), lambda i, j: (i, 0))],
        out_shape=[jax.ShapeDtypeStruct((m, n_blocks * tn), BF16), jax.ShapeDtypeStruct((m, k), BF16)],
        compiler_params=_params(2),
    )(x, w_t)


def _proj_t_kernel(w_ref, x_ref, o_ref):
    dn = (((1,), (1,)), ((), ()))
    o_ref[...] = lax.dot_general(w_ref[...], x_ref[...], dn,
                                 preferred_element_type=F32).astype(o_ref.dtype)


def _proj_t(w_t, x, batch, seq, tm):
    n, k = w_t.shape
    per_seq = seq // tm
    return pl.pallas_call(
        _proj_t_kernel,
        grid=(batch * per_seq,),
        in_specs=[pl.BlockSpec((n, k), lambda i: (0, 0)),
                  pl.BlockSpec((tm, k), lambda i: (i, 0))],
        out_specs=pl.BlockSpec((None, n, tm), lambda i: (i // per_seq, 0, i % per_seq)),
        out_shape=jax.ShapeDtypeStruct((batch, n, seq), BF16),
        compiler_params=_params(1),
    )(w_t, x)


def _proj_bf16_kernel(x_ref, w_ref, o_ref, *, w_transposed):
    dn = (((1,), (1 if w_transposed else 0,)), ((), ()))
    o_ref[...] = lax.dot_general(x_ref[...], w_ref[...], dn,
                                 preferred_element_type=F32).astype(o_ref.dtype)


def _proj_bf16(x, w, tm, tn, col0, n, w_transposed=False):
    m, k = x.shape
    blk0 = col0 // tn
    w_spec = (pl.BlockSpec((tn, k), lambda i, j: (blk0 + j, 0)) if w_transposed
              else pl.BlockSpec((k, tn), lambda i, j: (0, blk0 + j)))
    return pl.pallas_call(
        functools.partial(_proj_bf16_kernel, w_transposed=w_transposed),
        grid=(m // tm, n // tn),
        in_specs=[pl.BlockSpec((tm, k), lambda i, j: (i, 0)), w_spec],
        out_specs=pl.BlockSpec((tm, tn), lambda i, j: (i, j)),
        out_shape=jax.ShapeDtypeStruct((m, n), BF16),
        compiler_params=_params(2),
    )(x, w)


PROJ_SPLIT = 2


def _proj_heads_kernel(x_ref, w_ref, cos_ref, sin_ref, *rest, rope, dils):
    o_refs, scrs = rest[:len(dils)], rest[len(dils):]
    tg = x_ref.shape[0] // PROJ_SPLIT

    def project(g):
        return jnp.dot(x_ref[g * tg:(g + 1) * tg, :], w_ref[...], preferred_element_type=F32)

    def emit(g, y):
        for h in range(y.shape[1] // LANES):
            yh = y[:, h * LANES:(h + 1) * LANES]
            if rope:
                rows = slice(g * tg, (g + 1) * tg)
                yh = yh * cos_ref[rows, :] + pltpu.roll(yh, LANES // 2, 1) * sin_ref[rows, :]
            o_refs[0][h, g * tg:(g + 1) * tg, :] = yh.astype(o_refs[0].dtype)
            scrs[0][h, 0, g * tg:(g + 1) * tg, :] = yh
            for lvl in range(1, len(dils)):
                d_prev, d = dils[lvl - 1], dils[lvl]
                q, n = d // d_prev, tg // d
                for r_prev in range(d_prev):
                    for a in range(q):
                        part = scrs[lvl - 1][h, r_prev, pl.ds(g * tg // d_prev + a, n, stride=q), :]
                        r = d_prev * a + r_prev
                        o_refs[lvl][h, g * n:(g + 1) * n, r * LANES:(r + 1) * LANES] = (
                            part.astype(o_refs[lvl].dtype))
                        if lvl + 1 < len(dils):
                            scrs[lvl][h, r, g * n:(g + 1) * n, :] = part

    y_prev = project(0)
    for g in range(1, PROJ_SPLIT):
        y_next = project(g)
        emit(g - 1, y_prev)
        y_prev = y_next
    emit(PROJ_SPLIT - 1, y_prev)


def _proj_heads(x, w, col0, n, cos_t, sin_t, seq, tm, tn, rope, dils):
    m, k = x.shape
    blk0 = col0 // tn
    pos_blocks = seq // tm
    hb = tn // LANES
    return pl.pallas_call(
        functools.partial(_proj_heads_kernel, rope=rope, dils=dils),
        grid=(m // tm, n // tn),
        in_specs=[pl.BlockSpec((tm, k), lambda i, j: (i, 0)),
                  pl.BlockSpec((k, tn), lambda i, j: (0, blk0 + j)),
                  pl.BlockSpec((tm, LANES), lambda i, j: (i % pos_blocks, 0)),
                  pl.BlockSpec((tm, LANES), lambda i, j: (i % pos_blocks, 0))],
        out_specs=[pl.BlockSpec((hb, tm // d, d * LANES), lambda i, j: (j, i, 0)) for d in dils],
        out_shape=[jax.ShapeDtypeStruct((n // LANES, m // d, d * LANES), BF16) for d in dils],
        scratch_shapes=[pltpu.VMEM((hb, d, tm // d, LANES), F32) for d in dils[:-1]],
        compiler_params=_params(2),
    )(x, w, cos_t, sin_t)


def _na_bias_kernel(rpb_ref, o_ref, t_ref):
    h = pl.program_id(0)
    n_dc = 2 * NA_WIN_COLS - 1
    kc = lax.broadcasted_iota(jnp.int32, (GRID_W, GRID_W), 0)
    c = lax.broadcasted_iota(jnp.int32, (GRID_W, GRID_W), 1)
    d = kc - c + (NA_WIN_COLS - 1)
    cs = jnp.clip(c - NA_WIN_COLS // 2, 0, GRID_W - NA_WIN_COLS)
    col_ok = (kc >= cs) & (kc < cs + NA_WIN_COLS)
    for dr in range(2 * NA_WIN_ROWS - 1):
        t = jnp.zeros((GRID_W, GRID_W), F32)
        for dd in range(n_dc):
            t = jnp.where(d == dd, rpb_ref[h, dr * n_dc + dd] * LOG2E, t)
        t_ref[dr] = jnp.where(col_ok, t, NEG)
    neg_blk = jnp.full((GRID_W, GRID_W), NEG, F32)
    for typ in range(3):
        for a in range(NA_QROWS):
            half = NA_WIN_ROWS // 2
            if typ == 0:
                lo, dr0 = max(a - half, 0), NA_WIN_ROWS - 1 - a
            elif typ == 1:
                lo, dr0 = a, NA_WIN_ROWS - 1 - half - a
            else:
                lo, dr0 = NA_QROWS + min(a - NA_QROWS + half, 0), -1 - a
            for kb in range(NA_KROWS):
                ok = lo <= kb < lo + NA_WIN_ROWS
                blk = t_ref[kb + dr0] if ok else neg_blk
                o_ref[0, typ, kb * GRID_W:(kb + 1) * GRID_W, a * GRID_W:(a + 1) * GRID_W] = blk


def _na_bias(rpb):
    rpb2 = rpb.reshape(NA_HEADS, -1)
    return pl.pallas_call(
        _na_bias_kernel,
        grid=(NA_HEADS,),
        in_specs=[pl.BlockSpec(memory_space=pltpu.SMEM)],
        out_specs=pl.BlockSpec((1, 3, NA_TK, NA_TQ), lambda h: (h, 0, 0, 0)),
        out_shape=jax.ShapeDtypeStruct((NA_HEADS, 3, NA_TK, NA_TQ), F32),
        scratch_shapes=[pltpu.VMEM((2 * NA_WIN_ROWS - 1, GRID_W, GRID_W), F32)],
        compiler_params=_params(1),
    )(rpb2)


def _na_kernel(q_ref, *refs):
    k_refs = [refs[NA_KPARTS * s:NA_KPARTS * (s + 1)] for s in range(NA_SUB)]
    vt_refs = [refs[NA_KPARTS * (NA_SUB + s):NA_KPARTS * (NA_SUB + s + 1)] for s in range(NA_SUB)]
    g_ref = refs[2 * NA_KPARTS * NA_SUB]
    b_refs = refs[2 * NA_KPARTS * NA_SUB + 1:2 * NA_KPARTS * NA_SUB + 1 + NA_SUB]
    o_ref = refs[-1]
    dn = (((1,), (1,)), ((), ()))
    blocks = [(e, s) for e in range(q_ref.shape[0]) for s in range(NA_SUB)]
    sts = []
    for e, s in blocks:
        k = jnp.concatenate([r[e] for r in k_refs[s]], axis=0)
        q = q_ref[e, s * NA_TQ:(s + 1) * NA_TQ, :]
        sts.append(lax.dot_general(k, q, dn, preferred_element_type=F32) + b_refs[s][0, 0])
    for (e, s), st in zip(blocks, sts):
        vt = jnp.concatenate([r[e] for r in vt_refs[s]], axis=1)
        m = jnp.max(st, axis=0, keepdims=True)
        p = jnp.exp2(st - m)
        l = jnp.sum(p, axis=0, keepdims=True)
        ot = jnp.dot(vt, p.astype(BF16), preferred_element_type=F32)
        g = g_ref[e, s * NA_TQ:(s + 1) * NA_TQ, :].astype(F32)
        o_ref[e, s * NA_TQ:(s + 1) * NA_TQ, :] = (
            (ot * (1.0 / l)).T * (g * jax.nn.sigmoid(g))).astype(o_ref.dtype)


def _na_attention(proj, vt_na, bias, batch, seq):
    rows = seq // GRID_W
    n_blocks = rows // NA_QROWS
    kq = NA_TK // NA_KPARTS
    proj3 = proj.reshape(batch, seq, proj.shape[1])
    group = math.gcd(batch, NA_GROUP)

    def kbase(jj):
        return jnp.clip(jj - (NA_WIN_ROWS // 2) // NA_QROWS, 0, n_blocks - NA_KPARTS)

    def btype(jj):
        return jnp.where(jj == 0, 0, jnp.where(jj == n_blocks - 1, 2, 1))

    def k_spec(s, part):
        return pl.BlockSpec((group, kq, LANES),
                            lambda h, j, b: (b, kbase(NA_SUB * j + s) + part, C0_K + h))

    def vt_spec(s, part):
        return pl.BlockSpec((group, HEAD_DIM, kq), lambda h, j, b: (b, h, kbase(NA_SUB * j + s) + part))

    def b_spec(s):
        return pl.BlockSpec((1, 1, NA_TK, NA_TQ), lambda h, j, b: (h, btype(NA_SUB * j + s), 0, 0))

    tok_spec = lambda col0: pl.BlockSpec((group, NA_SUB * NA_TQ, LANES), lambda h, j, b: (b, j, col0 + h))
    parts = [(s, i) for s in range(NA_SUB) for i in range(NA_KPARTS)]
    in_specs = ([tok_spec(C0_Q)] + [k_spec(s, i) for s, i in parts] + [vt_spec(s, i) for s, i in parts]
                + [tok_spec(C0_G)] + [b_spec(s) for s in range(NA_SUB)])
    out = pl.pallas_call(
        _na_kernel,
        grid=(NA_HEADS, n_blocks // NA_SUB, batch // group),
        in_specs=in_specs,
        out_specs=pl.BlockSpec((group, NA_SUB * NA_TQ, LANES), lambda h, j, b: (b, j, h)),
        out_shape=jax.ShapeDtypeStruct((batch, seq, WA), BF16),
        compiler_params=_params(3),
    )(proj3, *([proj3] * len(parts)), *([vt_na] * len(parts)), proj3, *([bias] * NA_SUB))
    return out.reshape(batch * seq, WA)


def _rms(lat_ref, g_ref):
    x = lat_ref[...].astype(F32)
    ms = jnp.mean(x * x, axis=1, keepdims=True)
    return (x * lax.rsqrt(ms + RMS_EPS) * g_ref[...]).astype(BF16)


def _rope64(t, cos_t, sin_t):
    lane = lax.broadcasted_iota(jnp.int32, t.shape, 1)
    half = MLA_ROPE // 2
    partner = jnp.where(lane < half, pltpu.roll(t, LANES - half, 1), pltpu.roll(t, half, 1))
    return t * cos_t + partner * sin_t


def _mla_q_up_kernel(lat_ref, g_ref, w_ref, cos_ref, sin_ref, o_ref):
    y = jnp.dot(_rms(lat_ref, g_ref), w_ref[...], preferred_element_type=F32)
    c = cos_ref[...]
    s = sin_ref[...]
    for h in range(MLA_HEADS):
        base = h * MLA_QK_PAD
        o_ref[:, base:base + LANES] = y[:, base:base + LANES].astype(o_ref.dtype)
        o_ref[:, base + LANES:base + 2 * LANES] = _rope64(
            y[:, base + LANES:base + 2 * LANES], c, s).astype(o_ref.dtype)


def _mla_kv_up_kernel(lat_ref, g_ref, w_ref, kr_ref, cos_ref, sin_ref, k_ref, vt_ref):
    y = jnp.dot(_rms(lat_ref, g_ref), w_ref[...], preferred_element_type=F32)
    k_pe = _rope64(kr_ref[...].astype(F32), cos_ref[...], sin_ref[...]).astype(k_ref.dtype)
    for h in range(MLA_HEADS):
        base = h * (MLA_NOPE + MLA_V)
        k_ref[:, h * MLA_QK_PAD:h * MLA_QK_PAD + LANES] = y[:, base:base + MLA_NOPE].astype(k_ref.dtype)
        k_ref[:, h * MLA_QK_PAD + LANES:(h + 1) * MLA_QK_PAD] = k_pe
        vt_ref[h * MLA_V:(h + 1) * MLA_V, :] = y[:, base + MLA_NOPE:base + MLA_NOPE + MLA_V].T.astype(vt_ref.dtype)


def _mla_up(proj, tail, q_g, w_q, kv_g, w_kv, cos_t, sin_t, seq, tm):
    m = proj.shape[0]
    pos_blocks = seq // tm
    tab = pl.BlockSpec((tm, LANES), lambda i: (i % pos_blocks, 0))
    full = lambda a: pl.BlockSpec(a.shape, lambda i: (0,) * a.ndim)
    q_all = pl.pallas_call(
        _mla_q_up_kernel,
        grid=(m // tm,),
        in_specs=[pl.BlockSpec((tm, MLA_Q_RANK), lambda i: (i, C0_QLAT)), full(q_g), full(w_q), tab, tab],
        out_specs=pl.BlockSpec((tm, MLA_HEADS * MLA_QK_PAD), lambda i: (i, 0)),
        out_shape=jax.ShapeDtypeStruct((m, MLA_HEADS * MLA_QK_PAD), BF16),
        compiler_params=_params(1),
    )(proj, q_g, w_q, cos_t, sin_t)
    k_all, vt_all = pl.pallas_call(
        _mla_kv_up_kernel,
        grid=(m // tm,),
        in_specs=[pl.BlockSpec((tm, MLA_KV_RANK), lambda i: (i, C0_KVLAT)), full(kv_g), full(w_kv),
                  pl.BlockSpec((tm, LANES), lambda i: (i, C0_KROPE)), tab, tab],
        out_specs=[pl.BlockSpec((tm, MLA_HEADS * MLA_QK_PAD), lambda i: (i, 0)),
                   pl.BlockSpec((WB, tm), lambda i: (0, i))],
        out_shape=[jax.ShapeDtypeStruct((m, MLA_HEADS * MLA_QK_PAD), BF16),
                   jax.ShapeDtypeStruct((WB, m), BF16)],
        compiler_params=_params(1),
    )(proj, kv_g, w_kv, tail, cos_t, sin_t)
    return q_all, k_all, vt_all


MLA_UNROLL = 8


def _mla_attn_kernel(q_ref, k_ref, vt_ref, g_ref, o_ref, s0, s1, m_s, l_s, acc_s, *, tk):
    q = q_ref[...]
    tq = q.shape[0]
    n = k_ref.shape[0] // tk
    dn = (((1,), (1,)), ((), ()))

    def scores(c):
        start = pl.multiple_of(c * tk, tk)
        return lax.dot_general(k_ref[pl.ds(start, tk), :], q, dn, preferred_element_type=F32)

    def update(c, st):
        vt = vt_ref[:, pl.ds(pl.multiple_of(c * tk, tk), tk)]
        m = m_s[...]
        m_new = jnp.maximum(m, jnp.max(st, axis=0, keepdims=True))
        alpha = jnp.exp2(m - m_new)
        p = jnp.exp2(st - m_new)
        m_s[...] = m_new
        l_s[...] = alpha * l_s[...] + jnp.sum(p, axis=0, keepdims=True)
        acc_s[...] = alpha * acc_s[...] + jnp.dot(vt, p.astype(BF16), preferred_element_type=F32)

    m_s[...] = jnp.full(m_s.shape, NEG, F32)
    l_s[...] = jnp.zeros(l_s.shape, F32)
    acc_s[...] = jnp.zeros(acc_s.shape, F32)
    s0[...] = scores(0)

    bufs = (s0, s1)
    unroll = math.gcd(MLA_UNROLL, n)

    def body(i, carry):
        c0 = unroll * i
        for u in range(unroll):
            bufs[(u + 1) % 2][...] = scores(jnp.minimum(c0 + u + 1, n - 1))
            update(c0 + u, bufs[u % 2][...])
        return carry

    lax.fori_loop(0, n // unroll, body, 0)
    g = g_ref[...].astype(F32)
    o_ref[...] = ((acc_s[...] * (1.0 / l_s[...])).T * (g * jax.nn.sigmoid(g))).astype(o_ref.dtype)


def _mla_attention(q_all, k_all, vt_all, proj, batch, seq, tq, tk):
    nq = seq // tq
    n = seq // tk
    assert n % 2 == 0 and MLA_UNROLL % 2 == 0
    return pl.pallas_call(
        functools.partial(_mla_attn_kernel, tk=tk),
        grid=(batch, MLA_HEADS, nq),
        in_specs=[pl.BlockSpec((tq, MLA_QK_PAD), lambda b, h, i: (b * nq + i, h)),
                  pl.BlockSpec((seq, MLA_QK_PAD), lambda b, h, i: (b, h)),
                  pl.BlockSpec((MLA_V, seq), lambda b, h, i: (h, b)),
                  pl.BlockSpec((tq, LANES), lambda b, h, i: (b * nq + i, C0_GB + h))],
        out_specs=pl.BlockSpec((tq, MLA_V), lambda b, h, i: (b * nq + i, h)),
        out_shape=jax.ShapeDtypeStruct((batch * seq, WB), BF16),
        scratch_shapes=[pltpu.VMEM((tk, tq), F32), pltpu.VMEM((tk, tq), F32), pltpu.VMEM((1, tq), F32),
                        pltpu.VMEM((1, tq), F32), pltpu.VMEM((MLA_V, tq), F32)],
        compiler_params=_params(3),
    )(q_all, k_all, vt_all, proj)


OUT_LN_SPLIT = 2


def _out_ln_kernel(y1_ref, y2_ref, w_ref, x_ref, g_ref, b_ref, o_ref, ob_ref):
    half = y1_ref.shape[1]
    tm = y1_ref.shape[0]
    sub = tm // OUT_LN_SPLIT

    def project(rows):
        y = jnp.dot(y1_ref[rows, :], w_ref[:half, :], preferred_element_type=F32)
        return y + jnp.dot(y2_ref[rows, :], w_ref[half:, :], preferred_element_type=F32)

    def norm(rows, y):
        z = ALPHA * x_ref[rows, :] + y
        mu = jnp.mean(z, axis=1, keepdims=True)
        zc = z - mu
        var = jnp.mean(zc * zc, axis=1, keepdims=True)
        out = zc * lax.rsqrt(var + LN_EPS) * g_ref[...] + b_ref[...]
        o_ref[rows, :] = out
        ob_ref[rows, :] = out.astype(ob_ref.dtype)

    rows = [slice(i * sub, (i + 1) * sub) for i in range(OUT_LN_SPLIT)]
    y_prev = project(rows[0])
    for i in range(1, OUT_LN_SPLIT):
        y_next = project(rows[i])
        norm(rows[i - 1], y_prev)
        y_prev = y_next
    norm(rows[-1], y_prev)


def _out_ln(y1, y1_col, y2, y2_col, w, x, g, b, tm):
    m, d = x.shape
    half = w.shape[0] // 2
    full = lambda a: pl.BlockSpec(a.shape, lambda i: (0,) * a.ndim)
    return pl.pallas_call(
        _out_ln_kernel,
        grid=(m // tm,),
        in_specs=[pl.BlockSpec((tm, half), lambda i: (i, y1_col)),
                  pl.BlockSpec((tm, half), lambda i: (i, y2_col)),
                  full(w), pl.BlockSpec((tm, d), lambda i: (i, 0)), full(g), full(b)],
        out_specs=[pl.BlockSpec((tm, d), lambda i: (i, 0)), pl.BlockSpec((tm, d), lambda i: (i, 0))],
        out_shape=[jax.ShapeDtypeStruct((m, d), F32), jax.ShapeDtypeStruct((m, d), BF16)],
        compiler_params=_params(1),
    )(y1, y2, w, x, g, b)


DIL_DILS = tuple(d for _, d in DIL_PAIRS)
DIL_SIDE = DIL_PAIRS[0][0] // 2 // DIL_PAIRS[0][1]
DIL_TQ = 128
DIL_TK = DIL_TQ + 2 * DIL_SIDE
DIL_SB_UNIT = DIL_TQ * DIL_DILS[-1]
DIL_SB_MAX = 2 * DIL_SB_UNIT
assert all(w // 2 // d == DIL_SIDE for w, d in DIL_PAIRS)


def _dil_kernel(*refs, nsb, sb_tokens):
    q_refs = refs[0:3]
    k_refs = [refs[3 + 3 * p:6 + 3 * p] for p in range(3)]
    v_refs = [refs[12 + 3 * p:15 + 3 * p] for p in range(3)]
    g_ref, o_ref, bias_s, o_s, m_s, l_s = refs[21:]
    sb = pl.program_id(2)
    dn = (((1,), (1,)), ((), ()))

    qi = lax.broadcasted_iota(jnp.int32, (DIL_TQ, DIL_TK), 0)
    kj = lax.broadcasted_iota(jnp.int32, (DIL_TQ, DIL_TK), 1)
    base = jnp.where((kj >= qi) & (kj <= qi + 2 * DIL_SIDE), 0.0, NEG).astype(F32)
    lo = jnp.where(kj < DIL_SIDE, NEG, 0.0) * (sb == 0).astype(F32)
    hi = jnp.where(kj >= DIL_TK - DIL_SIDE, NEG, 0.0) * (sb == nsb - 1).astype(F32)
    bias_s[0] = base
    bias_s[1] = base + lo
    bias_s[2] = base + hi
    bias_s[3] = base + lo + hi

    def window(trio, lanes, c, n):
        main, prev, nxt = trio
        if n == 1:
            return jnp.concatenate([prev[:, lanes], main[:, lanes], nxt[:, lanes]], axis=0), 3
        if c == 0:
            return jnp.concatenate([prev[:, lanes], main[0:DIL_TK - DIL_SIDE, lanes]], axis=0), 1
        if c == n - 1:
            return jnp.concatenate([main[n * DIL_TQ - (DIL_TK - DIL_SIDE):n * DIL_TQ, lanes],
                                    nxt[:, lanes]], axis=0), 2
        return main[c * DIL_TQ - DIL_SIDE:c * DIL_TQ + DIL_TQ + DIL_SIDE, lanes], 0

    def scores(t):
        pair, r, c, n = t
        lanes = slice(r * LANES, (r + 1) * LANES)
        kw, mask_id = window(k_refs[pair], lanes, c, n)
        q = q_refs[pair][c * DIL_TQ:(c + 1) * DIL_TQ, lanes]
        return lax.dot_general(q, kw, dn, preferred_element_type=F32), mask_id

    def finish(t, s, mask_id):
        pair, r, c, n = t
        dil = DIL_DILS[pair]
        vw, _ = window(v_refs[pair], slice(r * LANES, (r + 1) * LANES), c, n)
        s = s + bias_s[mask_id]
        m = jnp.max(s, axis=1, keepdims=True)
        p = jnp.exp2(s - m)
        l = jnp.sum(p, axis=1, keepdims=True)
        num = jnp.dot(p.astype(BF16), vw, preferred_element_type=F32)
        m = jnp.broadcast_to(m, (DIL_TQ, LANES))
        l = jnp.broadcast_to(l, (DIL_TQ, LANES))
        start = (c * DIL_TQ) * dil + r
        if dil > 1:
            rows = pl.ds(start, DIL_TQ, stride=dil)
            o_s[pair - 1, rows, :] = num
            m_s[pair - 1, rows, :] = m
            l_s[pair - 1, rows, :] = l
            return
        rows = pl.ds(start, DIL_TQ)
        others = range(len(DIL_DILS) - 1)
        ms = [m] + [m_s[i, rows, :] for i in others]
        dens = [l] + [l_s[i, rows, :] for i in others]
        nums = [num] + [o_s[i, rows, :] for i in others]
        top = functools.reduce(jnp.maximum, ms)
        w = [jnp.exp2(x - top) for x in ms]
        num_tot = sum(wi * ni for wi, ni in zip(w, nums))
        den_tot = sum(wi * di for wi, di in zip(w, dens))
        g = g_ref[rows, :].astype(F32)
        o_ref[rows, :] = (num_tot / den_tot * (g * jax.nn.sigmoid(g))).astype(o_ref.dtype)

    for pair in reversed(range(len(DIL_DILS))):
        n = sb_tokens // DIL_DILS[pair] // DIL_TQ
        for r in range(DIL_DILS[pair]):
            for c in range(n):
                t = (pair, r, c, n)
                finish(t, *scores(t))


def _dil_attention(qk_views, v_views, gate, batch, seq):
    tokens = gate.shape[0]
    sb_tokens = math.gcd(seq, DIL_SB_MAX)
    assert sb_tokens % DIL_SB_UNIT == 0
    nsb = seq // sb_tokens
    halo = DIL_SIDE
    specs_main, specs_halo = [], []
    for dil in DIL_DILS:
        rows, width = tokens // dil, dil * LANES
        main_rows = sb_tokens // dil
        per_main = main_rows // halo
        last = rows // halo - 1

        def main_spec(h0, main_rows=main_rows, width=width):
            return pl.BlockSpec((None, main_rows, width), lambda b, h, s: (h0 + h, b * nsb + s, 0))

        def prev_spec(h0, per_main=per_main, width=width):
            return pl.BlockSpec((None, halo, width),
                                lambda b, h, s: (h0 + h, jnp.maximum((b * nsb + s) * per_main - 1, 0), 0))

        def next_spec(h0, per_main=per_main, width=width, last=last):
            return pl.BlockSpec((None, halo, width),
                                lambda b, h, s: (h0 + h, jnp.minimum((b * nsb + s + 1) * per_main, last), 0))

        specs_main.append(main_spec)
        specs_halo.append((prev_spec, next_spec))
    in_specs = [specs_main[p](0) for p in range(3)]
    operands = list(qk_views)
    for h0, views in ((DIL_HEADS, qk_views), (0, v_views)):
        for p in range(3):
            in_specs += [specs_main[p](h0), specs_halo[p][0](h0), specs_halo[p][1](h0)]
            operands += [views[p]] * 3
    in_specs.append(pl.BlockSpec((sb_tokens, LANES), lambda b, h, s: (b * nsb + s, h)))
    operands.append(gate)
    return pl.pallas_call(
        functools.partial(_dil_kernel, nsb=nsb, sb_tokens=sb_tokens),
        grid=(batch, DIL_HEADS, nsb),
        in_specs=in_specs,
        out_specs=pl.BlockSpec((sb_tokens, LANES), lambda b, h, s: (b * nsb + s, h)),
        out_shape=jax.ShapeDtypeStruct((tokens, WC), BF16),
        scratch_shapes=[pltpu.VMEM((4, DIL_TQ, DIL_TK), F32),
                        ] + [pltpu.VMEM((len(DIL_DILS) - 1, sb_tokens, LANES), F32)] * 3,
        compiler_params=_params(3),
    )(*operands)


def _rope_tables(seq, half):
    inv = ROPE_THETA ** (-jnp.arange(half, dtype=F32) / half)
    ang = jnp.arange(seq, dtype=F32)[:, None] * inv[None, :]
    cos, sin = jnp.cos(ang), jnp.sin(ang)
    pad = jnp.zeros((seq, LANES - 2 * half), F32)
    return jnp.concatenate([cos, cos, pad], 1), jnp.concatenate([-sin, sin, pad], 1)


def _cast_scale_kernel(w_ref, o_ref, *, n_q, scale, axis):
    idx = lax.broadcasted_iota(jnp.int32, w_ref.shape, axis)
    if axis == 0:
        idx = idx + pl.program_id(0) * w_ref.shape[0]
    o_ref[...] = (w_ref[...] * jnp.where(idx < n_q, scale, 1.0)).astype(o_ref.dtype)


def _cast_scale(w, n_q, scale, rows, axis=1):
    r, c = w.shape
    return pl.pallas_call(
        functools.partial(_cast_scale_kernel, n_q=n_q, scale=scale, axis=axis),
        grid=(r // rows,),
        in_specs=[pl.BlockSpec((rows, c), lambda i: (i, 0))],
        out_specs=pl.BlockSpec((rows, c), lambda i: (i, 0)),
        out_shape=jax.ShapeDtypeStruct((r, c), BF16),
        compiler_params=_params(1),
    )(w)


def _prep_weights(ab_w_in, ab_w_q_up, ab_w_kv_up, ab_w_out, c_w_in, c_w_out):
    q_scale = HEAD_DIM ** -0.5 * LOG2E
    w0_t = _cast_scale(ab_w_in.T, WA, q_scale, rows=ab_w_in.shape[1] // 8, axis=0)
    tail0 = 4 * WA + MLA_Q_RANK + MLA_KV_RANK
    w_tail_t = jnp.concatenate(
        [w0_t[tail0 + MLA_ROPE:], w0_t[tail0:tail0 + MLA_ROPE],
         jnp.zeros((IN0_TAIL_WIDTH - WB - MLA_ROPE, D_MODEL), BF16)], axis=0)
    w_in0 = (w0_t, w_tail_t, w0_t[2 * WA:3 * WA])
    wq = (ab_w_q_up * ((MLA_NOPE + MLA_ROPE) ** -0.5 * LOG2E)).reshape(
        MLA_Q_RANK, MLA_HEADS, MLA_NOPE + MLA_ROPE)
    wq = jnp.pad(wq, ((0, 0), (0, 0), (0, MLA_QK_PAD - MLA_NOPE - MLA_ROPE)))
    w_q = wq.reshape(MLA_Q_RANK, MLA_HEADS * MLA_QK_PAD).astype(BF16)
    w_kv = ab_w_kv_up.astype(BF16)
    w_in1 = _cast_scale(c_w_in, WC, q_scale, rows=256)
    return w_in0, w_q, w_kv, ab_w_out.astype(BF16), w_in1, c_w_out.astype(BF16)


def _trunk(x, w, tabs):
    batch, seq, _ = x.shape
    (w_in0, w_q, w_kv, w_out0, w_in1, w_out1, na_bias, q_g, kv_g,
     ln0_g, ln0_b, ln1_g, ln1_b) = w
    cos64, sin64, cos128, sin128 = tabs
    x2 = x.reshape(batch * seq, D_MODEL)
    w0_t, w_tail_t, w_va_t = w_in0
    proj0, xb = _proj_f32(x2, w0_t, tm=1024, tn=IN0_TN, w_blocks=IN0_MAIN_BLOCKS)
    tail0 = _proj_bf16(xb, w_tail_t, tm=1024, tn=IN0_TAIL_WIDTH, col0=0, n=IN0_TAIL_WIDTH,
                       w_transposed=True)
    vt_na = _proj_t(w_va_t, xb, batch, seq, tm=1024)
    ya = _na_attention(proj0, vt_na, na_bias, batch, seq)
    q_all, k_all, vt_all = _mla_up(proj0, tail0, q_g, w_q, kv_g, w_kv, cos64, sin64, seq, tm=512)
    yb = _mla_attention(q_all, k_all, vt_all, tail0, batch, seq, tq=1024, tk=1024)
    x1, x1b = _out_ln(ya, 0, yb, 0, w_out0, x2, ln0_g, ln0_b, tm=512)
    proj1 = functools.partial(_proj_heads, x1b, w_in1, cos_t=cos128, sin_t=sin128, seq=seq,
                              tm=1024, tn=1024, dils=DIL_DILS)
    qk1 = proj1(col0=0, n=2 * WC, rope=True)
    v1 = proj1(col0=2 * WC, n=WC, rope=False)
    gate1 = _proj_bf16(x1b, w_in1, tm=1024, tn=1024, col0=3 * WC, n=WC)
    yc = _dil_attention(qk1, v1, gate1, batch, seq)
    y, _ = _out_ln(yc, 0, yc, 1, w_out1, x1, ln1_g, ln1_b, tm=512)
    return y.reshape(batch, seq, D_MODEL)


def kernel(x_prompt, x_sample, ab_w_in, ab_rpb, ab_q_norm_g, ab_w_q_up, ab_kv_norm_g, ab_w_kv_up,
           ab_w_out, ab_ln_g, ab_ln_b, c_w_in, c_w_out, c_ln_g, c_ln_b):
    w_in0, w_q, w_kv, w_out0, w_in1, w_out1 = _prep_weights(
        ab_w_in, ab_w_q_up, ab_w_kv_up, ab_w_out, c_w_in, c_w_out)
    row = lambda a: a.reshape(1, -1).astype(F32)
    w = (w_in0, w_q, w_kv, w_out0, w_in1, w_out1, _na_bias(ab_rpb), row(ab_q_norm_g), row(ab_kv_norm_g),
         row(ab_ln_g), row(ab_ln_b), row(c_ln_g), row(c_ln_b))
    max_seq = max(x_prompt.shape[1], x_sample.shape[1])
    tabs = _rope_tables(max_seq, MLA_ROPE // 2) + _rope_tables(max_seq, HEAD_DIM // 2)
    return tuple(_trunk(x, w, tabs) for x in (x_prompt, x_sample))
```
